```python
import jax, jax.numpy as jnp
from jax import lax
import numpy as np

D_MODEL = 1024
BATCH = 16
SEQ = 256
DEPTH = 2
DEC_BATCH = 2
DEC_SEQ = 4096
PAST_LEN = 256

GRID_W = 64
EPS = 1e-6
TINY = 1e-30
W_A = 256
W_B = 256
W_C = 256
W_D = 256
MIX_W = W_A + W_B + W_C + W_D
A_HEADS = 4
A_HD = W_A // A_HEADS
CONV_W = 4
CONV_LEFT = 1
LRU_C = 8.0
B_GROUPS = 4
B_GD = W_B // B_GROUPS
POOL_WINDOWS = (2, 4, 8, 16)
C_HEADS = 4
C_DK = W_C // C_HEADS
C_DV = W_C // C_HEADS
CHUNK = 64
D_HEADS = 4
D_NOPE = 64
D_ROPE = 32
D_V = 64
QK_DIM = D_NOPE + D_ROPE
Q_RANK = 256
KV_RANK = 128
ROPE_THETA = 10000.0
Q_BLOCK = 128
N_EXPERTS = 16
EXPERT_FF = 1024
CAPACITY_FACTOR = 2
IN_SPLITS = (W_A, W_A, W_B, W_C, W_C, W_C, W_C, W_C, Q_RANK, KV_RANK, D_ROPE)
IN_W = 2 * W_A + W_B + 5 * W_C + Q_RANK + KV_RANK + D_ROPE

kernel_name = "hybrid_diffusion_parallel_groups_step"


def rms_norm(x, w):
    xf = x.astype(jnp.float32)
    y = xf * lax.rsqrt(jnp.mean(xf * xf, axis=-1, keepdims=True) + EPS)
    return (y * w.astype(jnp.float32)).astype(x.dtype)


def centred_dwconv(x, w, b):
    T = x.shape[1]
    xp = jnp.pad(x, ((0, 0), (CONV_LEFT, CONV_W - 1 - CONV_LEFT), (0, 0)))
    out = b
    for j in range(CONV_W):
        out = out + xp[:, j:j + T] * w[j]
    return out


def block_diag_linear(x, w, b):
    xh = x.reshape(x.shape[:-1] + w.shape[:2])
    return jnp.einsum('bthi,hij->bthj', xh, w).reshape(x.shape) + b


def linear_recurrence(a, u, h0):
    def combine(left, right):
        a_l, u_l = left
        a_r, u_r = right
        return a_l * a_r, a_r * u_l + u_r
    a_cum, h = lax.associative_scan(combine, (a, u), axis=1)
    return h + a_cum * h0[:, None, :]


def rglru_direction(x, w_a, b_a, w_x, b_x, lam, h0):
    r = jax.nn.sigmoid(block_diag_linear(x, w_a, b_a))
    i = jax.nn.sigmoid(block_diag_linear(x, w_x, b_x))
    log_a = -LRU_C * r * jax.nn.softplus(-lam)
    a = jnp.exp(log_a)
    u = jnp.sqrt(jnp.maximum(-jnp.expm1(2.0 * log_a), 0.0)) * (i * x)
    return linear_recurrence(a, u, h0)


def rglru_group(x_in, gate, conv_w, conv_b, w_a, b_a, w_x, b_x, lam, h0):
    xc = centred_dwconv(x_in, conv_w, conv_b).astype(jnp.float32)
    h0 = h0.astype(jnp.float32)
    h_f = rglru_direction(xc, w_a[0], b_a[0], w_x[0], b_x[0], lam[0], h0[:, 0])
    h_b = rglru_direction(jnp.flip(xc, 1), w_a[1], b_a[1], w_x[1], b_x[1], lam[1], h0[:, 1])
    y = (h_f + jnp.flip(h_b, 1)) * jax.nn.gelu(gate.astype(jnp.float32))
    return y.astype(x_in.dtype), jnp.stack([h_f[:, -1], h_b[:, -1]], axis=1)


def pool_group(x, w_pool, scale):
    B, T, _ = x.shape
    xf = x.astype(jnp.float32)
    cs = jnp.pad(jnp.cumsum(xf, axis=1), ((0, 0), (1, 0), (0, 0)))
    t = jnp.arange(T)
    groups = []
    for g, w in enumerate(POOL_WINDOWS):
        lo = jnp.clip(t - w // 2, 0, T)
        hi = jnp.clip(t + w // 2, 0, T)
        seg = cs[..., g * B_GD:(g + 1) * B_GD]
        mean = (seg[:, hi] - seg[:, lo]) / (hi - lo).astype(jnp.float32)[None, :, None]
        groups.append(mean - xf[..., g * B_GD:(g + 1) * B_GD])
    pooled = jnp.stack(groups, axis=2)
    y = jnp.einsum('btgi,gij->btgj', pooled, w_pool).reshape(B, T, W_B) * scale
    return y.astype(x.dtype)


def hgrn2_chunkwise(q, k, v, log_f, s0):
    B, T, H, DK = q.shape
    n = T // CHUNK

    def to_chunks(t):
        return jnp.moveaxis(t.reshape(B, n, CHUNK, H, t.shape[-1]), 1, 0)

    mask = jnp.tril(jnp.ones((CHUNK, CHUNK), bool))[None, :, :, None, None]

    def step(S, inp):
        qb, kb, vb, lf = inp
        g = jnp.cumsum(lf, axis=1)
        diff = g[:, :, None] - g[:, None, :]
        decay = jnp.where(mask, jnp.exp(jnp.where(mask, diff, 0.0)), 0.0)
        att = jnp.einsum('bthd,btshd,bshd->bhts', qb, decay, kb)
        o = jnp.einsum('bhts,bshv->bthv', att, vb)
        o = o + jnp.einsum('bthd,bhdv->bthv', qb * jnp.exp(g), S)
        g_last = g[:, -1]
        k_dec = kb * jnp.exp(g_last[:, None] - g)
        S_new = jnp.exp(g_last)[..., None] * S + jnp.einsum('bshd,bshv->bhdv', k_dec, vb)
        return S_new, o

    S_fin, o = lax.scan(step, s0.astype(jnp.float32), tuple(map(to_chunks, (q, k, v, log_f))))
    return jnp.moveaxis(o, 0, 1).reshape(B, T, H, v.shape[-1]), S_fin


def hgrn2_group(q_raw, ff_raw, fb_raw, v_raw, g_raw, lower, norm_w, s0):
    B, T, _ = q_raw.shape

    def heads(t):
        return t.astype(jnp.float32).reshape(B, T, C_HEADS, -1)

    q = heads(jax.nn.silu(q_raw.astype(jnp.float32))) * C_DK ** -0.5
    v = heads(v_raw)
    outs, finals = [], []
    for d, f_raw in enumerate((ff_raw, fb_raw)):
        lb = lower[d]
        f_val = lb + (1.0 - lb) * jax.nn.sigmoid(f_raw.astype(jnp.float32))
        log_f = heads(jnp.log(jnp.maximum(f_val, TINY)))
        k = heads(1.0 - f_val)
        seq = (q, k, v, log_f) if d == 0 else tuple(jnp.flip(t, 1) for t in (q, k, v, log_f))
        o, s_fin = hgrn2_chunkwise(*seq, s0[:, d])
        outs.append(o if d == 0 else jnp.flip(o, 1))
        finals.append(s_fin)
    o = rms_norm(outs[0] + outs[1], norm_w) * jax.nn.silu(heads(g_raw))
    return o.reshape(B, T, W_C).astype(q_raw.dtype), jnp.stack(finals, axis=1)


def axial_rope(n_tokens):
    n_rows = n_tokens // GRID_W
    rows, cols = jnp.meshgrid(jnp.arange(n_rows), jnp.arange(GRID_W), indexing='ij')
    rows = rows.reshape(-1).astype(jnp.float32)
    cols = cols.reshape(-1).astype(jnp.float32)
    n_freq = D_ROPE // 4
    inv = ROPE_THETA ** (-jnp.arange(n_freq, dtype=jnp.float32) / n_freq)
    ang = jnp.concatenate([rows[:, None] * inv, cols[:, None] * inv], axis=-1)
    return jnp.cos(ang), jnp.sin(ang)


def apply_rope(x, cos, sin):
    xr = x.reshape(x.shape[:-1] + (D_ROPE // 2, 2))
    x1, x2 = xr[..., 0], xr[..., 1]
    c = cos[None, :, None, :]
    s = sin[None, :, None, :]
    return jnp.stack([x1 * c - x2 * s, x1 * s + x2 * c], axis=-1).reshape(x.shape)


def mla_keys_values(ckv_n, k_rope, w_ukv, k_norm_w, rope):
    B, S, _ = ckv_n.shape
    kv = (ckv_n @ w_ukv).reshape(B, S, D_HEADS, D_NOPE + D_V)
    k_nope, v = kv[..., :D_NOPE], kv[..., D_NOPE:]
    k_r = jnp.broadcast_to(k_rope[:, :, None, :], (B, S, D_HEADS, D_ROPE)).astype(k_nope.dtype)
    k = rms_norm(jnp.concatenate([k_nope, k_r], axis=-1), k_norm_w)
    if rope is not None:
        k = jnp.concatenate([k[..., :D_NOPE], apply_rope(k[..., D_NOPE:], *rope)], axis=-1)
    return k, v


def blocked_attention(q, k, v):
    B, Tq, H, Dk = q.shape
    nb = Tq // Q_BLOCK
    qb = jnp.moveaxis(q.reshape(B, nb, Q_BLOCK, H, Dk), 1, 0)
    scale = Dk ** -0.5

    def one_block(qblk):
        s = jnp.einsum('bqhd,bkhd->bhqk', qblk, k).astype(jnp.float32) * scale
        p = jax.nn.softmax(s, axis=-1)
        return jnp.einsum('bhqk,bkhv->bqhv', p.astype(v.dtype), v)

    o = lax.map(one_block, qb)
    return jnp.moveaxis(o, 0, 1).reshape(B, Tq, H, v.shape[-1])


def token_mixers(h, lp, rope, ctx):
    B, T, _ = h.shape
    proj = h @ lp['w_in']
    offs = np.cumsum(IN_SPLITS)[:-1].tolist()
    a_x, a_gate, b_x, c_q, c_ff, c_fb, c_v, c_g, d_cq, d_ckv, d_kr = jnp.split(proj, offs, axis=-1)
    if ctx is None:
        h_lru0 = jnp.zeros((B, 2, W_A), jnp.float32)
        s_hgrn0 = jnp.zeros((B, 2, C_HEADS, C_DK, C_DV), jnp.float32)
    else:
        h_lru0, s_hgrn0, ckv_ctx, kr_ctx = ctx
    o_a, lru_fin = rglru_group(a_x, a_gate, lp['conv_w'], lp['conv_b'], lp['lru_wa'], lp['lru_ba'],
                               lp['lru_wx'], lp['lru_bx'], lp['lru_lambda'], h_lru0)
    o_b = pool_group(b_x, lp['pool_w'], lp['pool_scale'])
    o_c, hgrn_fin = hgrn2_group(c_q, c_ff, c_fb, c_v, c_g, lp['hgrn_lower'], lp['hgrn_norm'], s_hgrn0)
    ckv_n = rms_norm(d_ckv, lp['mla_kv_norm'])
    q = (rms_norm(d_cq, lp['mla_q_norm']) @ lp['mla_w_uq']).reshape(B, T, D_HEADS, QK_DIM)
    q = rms_norm(q, lp['mla_qn'])
    if rope is not None:
        q = jnp.concatenate([q[..., :D_NOPE], apply_rope(q[..., D_NOPE:], *rope)], axis=-1)
    k, v = mla_keys_values(ckv_n, d_kr, lp['mla_w_ukv'], lp['mla_kn'], rope)
    if ctx is not None:
        k_c, v_c = mla_keys_values(ckv_ctx, kr_ctx, lp['mla_w_ukv'], lp['mla_kn'], None)
        k = jnp.concatenate([k_c.astype(k.dtype), k], axis=1)
        v = jnp.concatenate([v_c.astype(v.dtype), v], axis=1)
    o_d = blocked_attention(q, k, v).reshape(B, T, W_D)
    y = jnp.concatenate([o_a, o_b, o_c, o_d.astype(o_a.dtype)], axis=-1) @ lp['w_out']
    ctx_out = (lru_fin, hgrn_fin, ckv_n, d_kr) if ctx is None else None
    return y, ctx_out


def expert_choice_ffn(h, w_router, w_gate, w_up, w_down):
    B, T, D = h.shape
    cap = CAPACITY_FACTOR * T // N_EXPERTS
    aff = jax.nn.softmax((h @ w_router).astype(jnp.float32), axis=-1)
    gates, idx = lax.top_k(jnp.swapaxes(aff, 1, 2), cap)
    xs = jax.vmap(lambda hb, ib: hb[ib])(h, idx)
    a = jnp.einsum('becd,edf->becf', xs, w_gate)
    u = jnp.einsum('becd,edf->becf', xs, w_up)
    y = jnp.einsum('becf,efd->becd', jax.nn.silu(a) * u, w_down)
    y = (y * gates[..., None]).astype(h.dtype)
    return jax.vmap(lambda ib, yb: jnp.zeros((T, D), yb.dtype).at[ib.reshape(-1)].add(yb.reshape(-1, D)))(idx, y)


def trunk_layer(x, cond, lp, rope, ctx):
    mod = jax.nn.silu(cond) @ lp['w_ada'] + lp['b_ada']
    sh1, sc1, g1, sh2, sc2, g2 = jnp.split(mod[:, None, :], 6, axis=-1)
    h = rms_norm(x, lp['norm1']) * (1 + sc1) + sh1
    m, ctx_out = token_mixers(h, lp, rope, ctx)
    x = x + g1 * m
    h = rms_norm(x, lp['norm2']) * (1 + sc2) + sh2
    x = x + g2 * expert_choice_ffn(h, lp['w_router'], lp['w_exp_gate'], lp['w_exp_up'], lp['w_exp_down'])
    return x, ctx_out


def setup_inputs(seed: int = 0) -> dict:
    key = jax.random.key(seed)
    ks = iter(jax.random.split(key, 40))

    def nrm(shape, scale):
        return jax.random.normal(next(ks), shape, jnp.float32) * scale

    u = jax.random.uniform(next(ks), (DEPTH, 2, W_A), jnp.float32, 0.9, 0.999)
    s = u ** (1.0 / LRU_C)
    lru_lambda = jnp.log(s) - jnp.log1p(-s)
    return {
        "x_prompt": nrm((BATCH, SEQ, D_MODEL), 1.0),
        "x_sample": nrm((DEC_BATCH, DEC_SEQ, D_MODEL), 1.0),
        "cache_mla_ckv": nrm((DEC_BATCH, DEPTH, PAST_LEN, KV_RANK), 1.0),
        "cache_mla_krope": nrm((DEC_BATCH, DEPTH, PAST_LEN, D_ROPE), 1.0),
        "state_rglru": nrm((DEC_BATCH, DEPTH, 2, W_A), 0.5),
        "state_hgrn": nrm((DEC_BATCH, DEPTH, 2, C_HEADS, C_DK, C_DV), 0.3),
        "c": nrm((DEC_BATCH, D_MODEL), 1.0),
        "c_ctx": nrm((D_MODEL,), 1.0),
        "norm1_w": 1.0 + nrm((DEPTH, D_MODEL), 0.05),
        "norm2_w": 1.0 + nrm((DEPTH, D_MODEL), 0.05),
        "w_ada": nrm((DEPTH, D_MODEL, 6 * D_MODEL), 0.5 * D_MODEL ** -0.5),
        "b_ada": nrm((DEPTH, 6 * D_MODEL), 0.01),
        "w_in": nrm((DEPTH, D_MODEL, IN_W), D_MODEL ** -0.5),
        "conv_w": nrm((DEPTH, CONV_W, W_A), CONV_W ** -0.5),
        "conv_b": nrm((DEPTH, W_A), 0.01),
        "lru_wa": nrm((DEPTH, 2, A_HEADS, A_HD, A_HD), A_HD ** -0.5),
        "lru_ba": nrm((DEPTH, 2, W_A), 0.1),
        "lru_wx": nrm((DEPTH, 2, A_HEADS, A_HD, A_HD), A_HD ** -0.5),
        "lru_bx": nrm((DEPTH, 2, W_A), 0.1),
        "lru_lambda": lru_lambda,
        "pool_w": nrm((DEPTH, B_GROUPS, B_GD, B_GD), B_GD ** -0.5),
        "pool_scale": 1.0 + nrm((DEPTH, W_B), 0.1),
        "hgrn_lower_bounds": nrm((2, DEPTH, W_C), 0.1),
        "hgrn_norm_w": 1.0 + nrm((DEPTH, C_DV), 0.05),
        "mla_q_norm": 1.0 + nrm((DEPTH, Q_RANK), 0.05),
        "mla_w_uq": nrm((DEPTH, Q_RANK, D_HEADS * QK_DIM), Q_RANK ** -0.5),
        "mla_kv_norm": 1.0 + nrm((DEPTH, KV_RANK), 0.05),
        "mla_w_ukv": nrm((DEPTH, KV_RANK, D_HEADS * (D_NOPE + D_V)), KV_RANK ** -0.5),
        "mla_qk_norm_q": 1.0 + nrm((DEPTH, QK_DIM), 0.05),
        "mla_qk_norm_k": 1.0 + nrm((DEPTH, QK_DIM), 0.05),
        "w_out": nrm((DEPTH, MIX_W, D_MODEL), MIX_W ** -0.5),
        "w_router": nrm((DEPTH, D_MODEL, N_EXPERTS), D_MODEL ** -0.5),
        "w_exp_gate": nrm((DEPTH, N_EXPERTS, D_MODEL, EXPERT_FF), D_MODEL ** -0.5),
        "w_exp_up": nrm((DEPTH, N_EXPERTS, D_MODEL, EXPERT_FF), D_MODEL ** -0.5),
        "w_exp_down": nrm((DEPTH, N_EXPERTS, EXPERT_FF, D_MODEL), EXPERT_FF ** -0.5),
    }


def reference(x_prompt, x_sample, cache_mla_ckv, cache_mla_krope, state_rglru, state_hgrn, c, c_ctx,
              norm1_w, norm2_w, w_ada, b_ada, w_in, conv_w, conv_b,
              lru_wa, lru_ba, lru_wx, lru_bx, lru_lambda,
              pool_w, pool_scale, hgrn_lower_bounds, hgrn_norm_w,
              mla_q_norm, mla_w_uq, mla_kv_norm, mla_w_ukv, mla_qk_norm_q, mla_qk_norm_k,
              w_out, w_router, w_exp_gate, w_exp_up, w_exp_down):
    lb_soft = jax.nn.softmax(hgrn_lower_bounds.astype(jnp.float32), axis=1)
    lower = jnp.cumsum(lb_soft, axis=1) - lb_soft[:, :1]
    rope_lat = axial_rope(x_sample.shape[1])
    cond_ctx = jnp.broadcast_to(c_ctx, (x_prompt.shape[0], D_MODEL))
    y_p, y_s = x_prompt, x_sample
    ckvs, krs, lru_states, hgrn_states = [], [], [], []
    for l in range(DEPTH):
        lp = dict(norm1=norm1_w[l], norm2=norm2_w[l], w_ada=w_ada[l], b_ada=b_ada[l], w_in=w_in[l],
                  conv_w=conv_w[l], conv_b=conv_b[l], lru_wa=lru_wa[l], lru_ba=lru_ba[l],
                  lru_wx=lru_wx[l], lru_bx=lru_bx[l], lru_lambda=lru_lambda[l],
                  pool_w=pool_w[l], pool_scale=pool_scale[l],
                  hgrn_lower=lower[:, l], hgrn_norm=hgrn_norm_w[l],
                  mla_q_norm=mla_q_norm[l], mla_w_uq=mla_w_uq[l], mla_kv_norm=mla_kv_norm[l],
                  mla_w_ukv=mla_w_ukv[l], mla_qn=mla_qk_norm_q[l], mla_kn=mla_qk_norm_k[l],
                  w_out=w_out[l], w_router=w_router[l], w_exp_gate=w_exp_gate[l],
                  w_exp_up=w_exp_up[l], w_exp_down=w_exp_down[l])
        y_p, (s_lru, s_hgrn, ckv, kr) = trunk_layer(y_p, cond_ctx, lp, None, None)
        ckvs.append(ckv)
        krs.append(kr)
        lru_states.append(s_lru)
        hgrn_states.append(s_hgrn)
        y_s, _ = trunk_layer(y_s, c, lp, rope_lat,
                             (state_rglru[:, l], state_hgrn[:, l], cache_mla_ckv[:, l], cache_mla_krope[:, l]))
    return (y_p, y_s, jnp.stack(ckvs, axis=1), jnp.stack(krs, axis=1),
            jnp.stack(lru_states, axis=1), jnp.stack(hgrn_states, axis=1))
```

```python
import functools

import numpy as np
import jax
import jax.numpy as jnp
from jax import lax
from jax.experimental import pallas as pl
from jax.experimental.pallas import tpu as pltpu

F32 = jnp.float32
BF16 = jnp.bfloat16
I32 = jnp.int32

D = 1024
NTOK = 12288
SEG = 4096
NSEG = 3
EPS = 1e-6
TINY = 1e-30
LRU_C = 8.0
N_EXP = 16
CAP_SEG = 512
SLOT_PAD = 640
TILE = 256
TK = 512
WIN = 128
HEXT = D + 128
VMEM_LIMIT = 56 * 1024 * 1024

PA_W, PB_W, PC_W, PD_W = 512, 256, 1280, 640
IN_PAD_W = PA_W + PB_W + PC_W + PD_W


def _cparams(sem, vmem=None):
    return pltpu.CompilerParams(dimension_semantics=sem, vmem_limit_bytes=vmem)


def _dot(a, b):
    return jnp.dot(a, b, preferred_element_type=F32)


def _dot_nt(a, b):
    return lax.dot_general(a, b, (((1,), (1,)), ((), ())), preferred_element_type=F32)


def _dot_tn(a, b):
    return lax.dot_general(a, b, (((0,), (0,)), ((), ())), preferred_element_type=F32)


def _rms(x, w):
    ms = jnp.mean(x * x, axis=-1, keepdims=True)
    return x * lax.rsqrt(ms + EPS) * w


def _silu(x):
    return x * jax.nn.sigmoid(x)


def _split_bf16(x):
    hi = x.astype(BF16)
    lo = (x - hi.astype(F32)).astype(BF16)
    return hi, lo


def _segsum2(x, ones_blk):
    hi, lo = _split_bf16(x)
    return _dot(hi, ones_blk) + _dot(lo, ones_blk)


def _cumsum_rows(x, r8, rev):
    n = x.shape[0]
    for s in (1, 2, 4):
        if not rev:
            x = jnp.where(r8 >= s, x + pltpu.roll(x, s, 0), x)
        else:
            x = jnp.where(r8 < 8 - s, x + pltpu.roll(x, n - s, 0), x)
    ng = n // 8
    outs = [None] * ng
    c = None
    for g in (range(ng) if not rev else reversed(range(ng))):
        xg = x[8 * g:8 * g + 8]
        if c is not None:
            xg = xg + c
        c = xg[7:8] if not rev else xg[0:1]
        outs[g] = xg
    return jnp.concatenate(outs, axis=0)


def _ada_body(c_ref, w_ref, b_ref, o_ref):
    s = _silu(c_ref[...])
    o_ref[0] = _dot(s.astype(BF16), w_ref[0].astype(BF16)) + b_ref[0]


def _ada_call(cond8, w_ada, b_ada):
    nj = 4
    wj = 6 * D // nj
    return pl.pallas_call(
        _ada_body,
        grid=(2, nj),
        in_specs=[pl.BlockSpec((8, D), lambda l, j: (0, 0)),
                  pl.BlockSpec((1, D, wj), lambda l, j: (l, 0, j)),
                  pl.BlockSpec((1, 1, wj), lambda l, j: (l, 0, j))],
        out_specs=pl.BlockSpec((1, 8, wj), lambda l, j: (l, 0, j)),
        out_shape=jax.ShapeDtypeStruct((2, 8, 6 * D), F32),
        compiler_params=_cparams(("arbitrary", "arbitrary"), VMEM_LIMIT),
        name="ada",
    )(cond8, w_ada, b_ada.reshape(2, 1, 6 * D))


def _in_body(fuse, *refs):
    if fuse:
        x1_ref, moe_ref, g2_ref, sc_ref, sh_ref, nw_ref, w_ref, x_ref, pa_ref, pb_ref, pc_ref, pd_ref = refs
        x = x1_ref[...] + g2_ref[0] * moe_ref[...]
        x_ref[...] = x
    else:
        x_ref, sc_ref, sh_ref, nw_ref, w_ref, pa_ref, pb_ref, pc_ref, pd_ref = refs
        x = x_ref[...]
    h = _rms(x, nw_ref[...]) * (1.0 + sc_ref[0]) + sh_ref[0]
    hb = h.astype(BF16)
    o = 0
    for ref, w in ((pa_ref, PA_W), (pb_ref, PB_W), (pc_ref, PC_W), (pd_ref, PD_W)):
        ref[...] = _dot(hb, w_ref[:, o:o + w])
        o += w


def _in_call(x, sc1, sh1, norm1, w_in_b, fuse_args=None):
    tm = 512
    n = NTOK // tm
    per_seg = SEG // tm
    row = lambda i: (i, 0)
    modspec = pl.BlockSpec((1, 1, D), lambda i: (i // per_seg, 0, 0))
    tokspec = pl.BlockSpec((tm, D), row)
    outs = [jax.ShapeDtypeStruct((NTOK, w), F32) for w in (PA_W, PB_W, PC_W, PD_W)]
    out_specs = [pl.BlockSpec((tm, w), row) for w in (PA_W, PB_W, PC_W, PD_W)]
    common_specs = [modspec, modspec, pl.BlockSpec((1, D), lambda i: (0, 0)),
                    pl.BlockSpec((D, IN_PAD_W), lambda i: (0, 0))]
    if fuse_args is None:
        args = (x, sc1, sh1, norm1, w_in_b)
        in_specs = [tokspec] + common_specs
    else:
        moe, g2 = fuse_args
        args = (x, moe, g2, sc1, sh1, norm1, w_in_b)
        in_specs = [tokspec, tokspec, modspec] + common_specs
        outs = [jax.ShapeDtypeStruct((NTOK, D), F32)] + outs
        out_specs = [tokspec] + out_specs
    return pl.pallas_call(
        functools.partial(_in_body, fuse_args is not None),
        grid=(n,), in_specs=in_specs, out_specs=out_specs, out_shape=outs,
        compiler_params=_cparams(("arbitrary",), VMEM_LIMIT),
        name="in_proj",
    )(*args)


def _halo_tile(ref, c0, c1, t0, t_len, static_single):
    xa = ref[pl.ds(t0, TILE), c0:c1]
    if static_single:
        z = jnp.zeros((8, c1 - c0), F32)
        return xa, jnp.concatenate([z, xa, z], axis=0)
    ps = pl.multiple_of(jnp.maximum(t0 - 8, 0), 8)
    ns = pl.multiple_of(jnp.minimum(t0 + TILE, t_len - 8), 8)
    prev = jnp.where(t0 > 0, ref[pl.ds(ps, 8), c0:c1], 0.0)
    nxt = jnp.where(t0 + TILE < t_len, ref[pl.ds(ns, 8), c0:c1], 0.0)
    return xa, jnp.concatenate([prev, xa, nxt], axis=0)


def _seq_spec(t_len, width, blk_off):
    idx = lambda b: (b + blk_off, 0)
    if t_len > TILE:
        return pl.BlockSpec((t_len, width), idx, pipeline_mode=pl.Buffered(1))
    return pl.BlockSpec((t_len, width), idx)


def _with_prev(body, n_in, has_prev):
    if not has_prev:
        return body

    def wrapped(*refs):
        return body(*refs[:n_in], *refs[n_in + 1:])

    return wrapped


def _seq_call(body, n_in, in_specs, args, prev, **kw):
    if prev is not None:
        in_specs = in_specs + [pl.BlockSpec(memory_space=pl.ANY)]
        args = args + (prev,)
        kw["input_output_aliases"] = {n_in: 0}
    return pl.pallas_call(_with_prev(body, n_in, prev is not None), in_specs=in_specs, **kw)(*args)


def _gelu_tanh(x):
    return 0.5 * x * (1.0 + jnp.tanh(0.7978845608028654 * (x + 0.044715 * (x * x * x))))


def _softplus(x):
    return jnp.maximum(x, 0.0) + jnp.log1p(jnp.exp(-jnp.abs(x)))


def _lru_scan(a, u, c, r8, rev):
    n = a.shape[0]
    for s in (1, 2, 4):
        if not rev:
            m = r8 >= s
            a_sh, u_sh = pltpu.roll(a, s, 0), pltpu.roll(u, s, 0)
        else:
            m = r8 < 8 - s
            a_sh, u_sh = pltpu.roll(a, n - s, 0), pltpu.roll(u, n - s, 0)
        u = jnp.where(m, a * u_sh + u, u)
        a = jnp.where(m, a * a_sh, a)
    ng = n // 8
    outs = [None] * ng
    for g in (range(ng) if not rev else reversed(range(ng))):
        hg = u[8 * g:8 * g + 8] + a[8 * g:8 * g + 8] * c
        c = hg[7:8] if not rev else hg[0:1]
        outs[g] = hg
    return jnp.concatenate(outs, axis=0), c


def _lru_body(t_len, pa_ref, cw_ref, cb_ref, wa_ref, ba_ref, wx_ref, bx_ref, lam_ref, h0_ref,
              oa_ref, hfin_ref, hf_s, ab_s, ub_s):
    nt = t_len // TILE
    single = nt == 1
    r8 = lax.broadcasted_iota(I32, (TILE, 1), 0) & 7
    n_ext = TILE + 16

    def gates(xc, xb, d):
        r = jax.nn.sigmoid(_dot(xb, wa_ref[d]) + ba_ref[d])
        i = jax.nn.sigmoid(_dot(xb, wx_ref[d]) + bx_ref[d])
        log_a = -LRU_C * r * _softplus(-lam_ref[d])
        a = jnp.exp(log_a)
        th = jnp.tanh(log_a)
        mult = jnp.sqrt(jnp.maximum(-2.0 * th / (1.0 - th), 0.0))
        return a, mult * (i * xc)

    def fwd_tile(i, c):
        t0 = pl.multiple_of(i * TILE, TILE)
        xa, ext = _halo_tile(pa_ref, 0, 256, t0, t_len, single)
        xc = cb_ref[...] + xa * cw_ref[1:2, :]
        for j in (0, 2, 3):
            xc = xc + pltpu.roll(ext, n_ext - 7 - j, 0)[0:TILE] * cw_ref[j:j + 1, :]
        xb = xc.astype(BF16)
        a_f, u_f = gates(xc, xb, 0)
        h, c = _lru_scan(a_f, u_f, c, r8, False)
        hf_s[pl.ds(t0, TILE), :] = h
        a_b, u_b = gates(xc, xb, 1)
        ab_s[pl.ds(t0, TILE), :] = a_b
        ub_s[pl.ds(t0, TILE), :] = u_b
        return c

    def bwd_tile(k, c):
        t0 = pl.multiple_of((nt - 1 - k) * TILE, TILE)
        h_b, c = _lru_scan(ab_s[pl.ds(t0, TILE), :], ub_s[pl.ds(t0, TILE), :], c, r8, True)
        gate = pa_ref[pl.ds(t0, TILE), 256:512]
        oa_ref[pl.ds(t0, TILE), :] = (hf_s[pl.ds(t0, TILE), :] + h_b) * _gelu_tanh(gate)
        return c

    h0 = h0_ref[0]
    if single:
        c_f = fwd_tile(0, h0[0:1])
        c_b = bwd_tile(0, h0[1:2])
    else:
        c_f = lax.fori_loop(0, nt, fwd_tile, h0[0:1])
        c_b = lax.fori_loop(0, nt, bwd_tile, h0[1:2])
    hfin_ref[0, 0:1, :] = c_f
    hfin_ref[0, 1:2, :] = c_b


def _lru_call(pa, t_len, nb, blk_off, prev, conv_w, conv_b, wa_bd, ba, wx_bd, bx, lam, h0):
    full2 = lambda shp: pl.BlockSpec(shp, lambda b: (0,) * len(shp))
    return _seq_call(
        functools.partial(_lru_body, t_len), 9,
        [_seq_spec(t_len, PA_W, blk_off),
         full2((4, 256)), full2((1, 256)), full2((2, 256, 256)), full2((2, 1, 256)),
         full2((2, 256, 256)), full2((2, 1, 256)), full2((2, 1, 256)),
         pl.BlockSpec((1, 2, 256), lambda b: (b, 0, 0))],
        (pa, conv_w, conv_b, wa_bd, ba, wx_bd, bx, lam, h0), prev,
        grid=(nb,),
        out_specs=[pl.BlockSpec((t_len, 256), lambda b: (b + blk_off, 0)),
                   pl.BlockSpec((1, 2, 256), lambda b: (b, 0, 0))],
        out_shape=[jax.ShapeDtypeStruct((NTOK, 256), F32), jax.ShapeDtypeStruct((nb, 2, 256), F32)],
        scratch_shapes=[pltpu.VMEM((t_len, 256), F32)] * 3,
        compiler_params=_cparams(("arbitrary",), VMEM_LIMIT),
        name="lru",
    )


def _pool_body(t_len, pb_ref, pw_ref, ps_ref, ob_ref):
    nt = t_len // TILE
    single = nt == 1
    n_ext = TILE + 16
    lane = lax.broadcasted_iota(I32, (1, 256), 1)
    rowi = lax.broadcasted_iota(I32, (TILE, 1), 0)

    def ahead(x, k):
        return pltpu.roll(x, n_ext - k, 0)

    def tile(i, carry):
        t0 = pl.multiple_of(i * TILE, TILE)
        xa, ext = _halo_tile(pb_ref, 0, 256, t0, t_len, single)
        p2 = ext + ahead(ext, 1)
        p4 = p2 + ahead(p2, 2)
        p8 = p4 + ahead(p4, 4)
        p16 = p8 + ahead(p8, 8)
        sums = (ahead(p2, 7)[0:TILE], ahead(p4, 6)[0:TILE], ahead(p8, 4)[0:TILE], p16[0:TILE])
        tpos = t0 + rowi
        means = []
        for w, s in zip((2, 4, 8, 16), sums):
            cnt = jnp.minimum(tpos + w // 2, t_len) - jnp.maximum(tpos - w // 2, 0)
            means.append(s / cnt.astype(F32))
        mean = jnp.where(lane < 64, means[0], jnp.where(lane < 128, means[1],
                                                          jnp.where(lane < 192, means[2], means[3])))
        pooled = mean - xa
        ob_ref[pl.ds(t0, TILE), :] = _dot(pooled.astype(BF16), pw_ref[...]) * ps_ref[...]
        return carry

    if single:
        tile(0, 0)
    else:
        lax.fori_loop(0, nt, tile, 0)


def _pool_call(pb, t_len, nb, blk_off, prev, pw_bd, pscale):
    return _seq_call(
        functools.partial(_pool_body, t_len), 3,
        [_seq_spec(t_len, PB_W, blk_off),
         pl.BlockSpec((256, 256), lambda b: (0, 0)),
         pl.BlockSpec((1, 256), lambda b: (0, 0))],
        (pb, pw_bd, pscale), prev,
        grid=(nb,),
        out_specs=pl.BlockSpec((t_len, 256), lambda b: (b + blk_off, 0)),
        out_shape=jax.ShapeDtypeStruct((NTOK, 256), F32),
        compiler_params=_cparams(("arbitrary",), VMEM_LIMIT),
        name="pool",
    )


HGRN_LEVELS = (1, 2, 4, 8, 16, 32, 64, 128)


def _hgrn_dir(rev, q, k, v, lf, st_s, att_s, ones_ref, rowi, r8, coli, lane_head):
    c_rows = TILE
    g = _cumsum_rows(lf, r8, rev)
    vb = v.astype(BF16)
    seg = g
    row4 = lax.broadcasted_iota(I32, (4 * c_rows, 1), 0)
    t4 = row4 & (c_rows - 1)
    h4 = row4 >> 8
    for li, m in enumerate(HGRN_LEVELS):
        up = (rowi & (2 * m - 1)) >= m
        if not rev:
            ref_q = pltpu.roll(seg, m, 0)
            qsel, ksel = up, jnp.logical_not(up)
            seg_next = jnp.where(up, seg, pltpu.roll(seg, c_rows - m, 0))
        else:
            ref_q = pltpu.roll(seg, c_rows - m, 0)
            qsel, ksel = jnp.logical_not(up), up
            seg_next = jnp.where(up, pltpu.roll(seg, m, 0), seg)
        qp = jnp.where(qsel, q * jnp.exp(jnp.minimum(g - ref_q, 0.0)), 0.0)
        kp = jnp.where(ksel, k * jnp.exp(jnp.minimum(seg - g, 0.0)), 0.0)
        qs = jnp.concatenate([jnp.where(lane_head == h, qp, 0.0) for h in range(4)], axis=0).astype(BF16)
        prod = _dot_nt(qs, kp.astype(BF16))
        sh = int(np.log2(2 * m))
        pair = (t4 >> sh) == (coli >> sh)
        contrib = jnp.where(pair, prod, 0.0)
        if li == 0:
            att_s[...] = contrib
        else:
            att_s[...] = att_s[...] + contrib
        seg = seg_next
    o = _dot((q * k).astype(BF16), ones_ref[...]) * v
    for h in range(4):
        oh = _dot(att_s[h * c_rows:(h + 1) * c_rows, :].astype(BF16), vb)
        o = o + jnp.where(lane_head == h, oh, 0.0)
    st = st_s[...]
    o = o + _dot_nt((q * jnp.exp(g)).astype(BF16), st.astype(BF16))
    g_end = g[c_rows - 1:c_rows] if not rev else g[0:1]
    kd = k * jnp.exp(g_end - g)
    upd = _dot_tn(vb, kd.astype(BF16))
    blk = (lax.broadcasted_iota(I32, (256, 1), 0) >> 6) == (lax.broadcasted_iota(I32, (1, 256), 1) >> 6)
    st_s[...] = st * jnp.exp(g_end) + jnp.where(blk, upd, 0.0)
    return o


def _hgrn_body(t_len, pc_ref, lb_ref, nw_ref, s0_ref, ones_ref, oc_ref, sfin_ref, of_s, st_s, att_s):
    nt = t_len // TILE
    rowi = lax.broadcasted_iota(I32, (TILE, 1), 0)
    r8 = rowi & 7
    coli = lax.broadcasted_iota(I32, (1, TILE), 1)
    lane_head = lax.broadcasted_iota(I32, (1, 256), 1) >> 6

    def load(t0, d):
        q = _silu(pc_ref[pl.ds(t0, TILE), 0:256]) * 0.125
        f_raw = pc_ref[pl.ds(t0, TILE), 256 * (1 + d):256 * (2 + d)]
        v = pc_ref[pl.ds(t0, TILE), 768:1024]
        lb = lb_ref[d]
        f_val = lb + (1.0 - lb) * jax.nn.sigmoid(f_raw)
        lf = jnp.log(jnp.maximum(f_val, TINY))
        return q, 1.0 - f_val, v, lf

    def fwd_tile(i, carry):
        t0 = pl.multiple_of(i * TILE, TILE)
        q, k, v, lf = load(t0, 0)
        of_s[pl.ds(t0, TILE), :] = _hgrn_dir(False, q, k, v, lf, st_s, att_s, ones_ref, rowi, r8, coli, lane_head)
        return carry

    def bwd_tile(kk, carry):
        t0 = pl.multiple_of((nt - 1 - kk) * TILE, TILE)
        q, k, v, lf = load(t0, 1)
        o = of_s[pl.ds(t0, TILE), :] + _hgrn_dir(True, q, k, v, lf, st_s, att_s, ones_ref, rowi, r8, coli, lane_head)
        ms = _segsum2(o * o, ones_ref[...]) * (1.0 / 64.0)
        y = o * lax.rsqrt(ms + EPS) * nw_ref[...]
        oc_ref[pl.ds(t0, TILE), :] = y * _silu(pc_ref[pl.ds(t0, TILE), 1024:1280])
        return carry

    st_s[...] = s0_ref[0, 0]
    if nt == 1:
        fwd_tile(0, 0)
    else:
        lax.fori_loop(0, nt, fwd_tile, 0)
    sfin_ref[0, 0] = st_s[...]
    st_s[...] = s0_ref[0, 1]
    if nt == 1:
        bwd_tile(0, 0)
    else:
        lax.fori_loop(0, nt, bwd_tile, 0)
    sfin_ref[0, 1] = st_s[...]


def _hgrn_call(pc, t_len, nb, blk_off, prev, lower, normw, s0t, ones64):
    return _seq_call(
        functools.partial(_hgrn_body, t_len), 5,
        [_seq_spec(t_len, PC_W, blk_off),
         pl.BlockSpec((2, 1, 256), lambda b: (0, 0, 0)),
         pl.BlockSpec((1, 256), lambda b: (0, 0)),
         pl.BlockSpec((1, 2, 256, 256), lambda b: (b, 0, 0, 0)),
         pl.BlockSpec((256, 256), lambda b: (0, 0))],
        (pc, lower, normw, s0t, ones64), prev,
        grid=(nb,),
        out_specs=[pl.BlockSpec((t_len, 256), lambda b: (b + blk_off, 0)),
                   pl.BlockSpec((1, 2, 256, 256), lambda b: (b, 0, 0, 0))],
        out_shape=[jax.ShapeDtypeStruct((NTOK, 256), F32), jax.ShapeDtypeStruct((nb, 2, 256, 256), F32)],
        scratch_shapes=[pltpu.VMEM((t_len, 256), F32), pltpu.VMEM((256, 256), F32),
                        pltpu.VMEM((4 * TILE, TILE), F32)],
        compiler_params=_cparams(("arbitrary",), VMEM_LIMIT),
        name="hgrn",
    )


ATT_SCALE = 96.0 ** -0.5


def _rope512(x, cs128, sn128, lane128):
    cs = jnp.concatenate([cs128] * 4, axis=1)
    sn = jnp.concatenate([sn128] * 4, axis=1)
    partner = jnp.where(lane128 < 80, pltpu.roll(x, 512 - 16, 1), pltpu.roll(x, 16, 1))
    return x * cs + partner * sn


def _mla_body(t_len, n_ctx, use_rope, pd_ref, ckv_c_ref, kr_c_ref, cs_ref, sn_ref, qnorm_ref, wuq_ref, qnw_ref,
              kvnorm_ref, wukv_ref, knw_ref, ones_ref, od_ref, ckvn_ref, k_s, v_s, m_s, l_s, acc_s):
    nt = t_len // TILE
    t_k = n_ctx + t_len
    nkb = t_k // TILE
    lane128 = lax.broadcasted_iota(I32, (1, 512), 1) & 127
    lane_head = lax.broadcasted_iota(I32, (1, 256), 1) >> 6

    def head_norm(x, w_ref):
        ss = _segsum2(x * x, ones_ref[...])
        return x * lax.rsqrt(ss * (1.0 / 96.0) + EPS) * w_ref[...]

    def put_kv(r0, ckv_n, kr128, rope_rows):
        kv = _dot(ckv_n.astype(BF16), wukv_ref[...])
        k_all = kv[:, 0:512] + jnp.concatenate([kr128] * 4, axis=1)
        kn = head_norm(k_all, knw_ref)
        if rope_rows is not None:
            kn = _rope512(kn, cs_ref[pl.ds(rope_rows, TILE), :], sn_ref[pl.ds(rope_rows, TILE), :], lane128)
        k_s[pl.ds(r0, TILE), :] = kn.astype(BF16)
        v_s[pl.ds(r0, TILE), :] = kv[:, 512:768].astype(BF16)

    if n_ctx:
        put_kv(0, ckv_c_ref[0], kr_c_ref[0], None)

    def kv_tile(i, carry):
        t0 = pl.multiple_of(i * TILE, TILE)
        ckv_n = _rms(pd_ref[pl.ds(t0, TILE), 256:384], kvnorm_ref[...])
        ckvn_ref[pl.ds(t0, TILE), :] = ckv_n
        put_kv(pl.multiple_of(n_ctx + t0, TILE), ckv_n, pd_ref[pl.ds(t0, TILE), 512:640],
               t0 if use_rope else None)
        return carry

    if nt == 1:
        kv_tile(0, 0)
    else:
        lax.fori_loop(0, nt, kv_tile, 0)

    def q_tile(i, carry):
        t0 = pl.multiple_of(i * TILE, TILE)
        qn = _rms(pd_ref[pl.ds(t0, TILE), 0:256], qnorm_ref[...])
        q = head_norm(_dot(qn.astype(BF16), wuq_ref[...]), qnw_ref)
        if use_rope:
            q = _rope512(q, cs_ref[pl.ds(t0, TILE), :], sn_ref[pl.ds(t0, TILE), :], lane128)
        qb = q.astype(BF16)
        o = jnp.zeros((TILE, 256), F32)
        for h in range(4):
            qh = qb[:, 128 * h:128 * (h + 1)]
            if nkb == 1:
                s = _dot_nt(qh, k_s[0:TILE, 128 * h:128 * (h + 1)]) * ATT_SCALE
                p = jnp.exp(s - jnp.max(s, axis=1, keepdims=True))
                oh = _dot(p.astype(BF16), v_s[0:TILE, :]) / jnp.sum(p, axis=1, keepdims=True)
            else:
                m_s[...] = jnp.full((TILE, 128), -jnp.inf, F32)
                l_s[...] = jnp.zeros((TILE, 128), F32)
                acc_s[...] = jnp.zeros((TILE, 256), F32)

                def kblock(j, c2):
                    r0 = pl.multiple_of(j * TILE, TILE)
                    s = _dot_nt(qh, k_s[pl.ds(r0, TILE), 128 * h:128 * (h + 1)]) * ATT_SCALE
                    m_prev = m_s[...]
                    m_new = jnp.maximum(m_prev, jnp.max(s, axis=1, keepdims=True))
                    alpha = jnp.exp(m_prev - m_new)
                    p = jnp.exp(s - jnp.concatenate([m_new, m_new], axis=1))
                    l_s[...] = alpha * l_s[...] + jnp.sum(p, axis=1, keepdims=True)
                    acc_s[...] = (jnp.concatenate([alpha, alpha], axis=1) * acc_s[...]
                                  + _dot(p.astype(BF16), v_s[pl.ds(r0, TILE), :]))
                    m_s[...] = m_new
                    return c2

                lax.fori_loop(0, nkb, kblock, 0)
                l = l_s[...]
                oh = acc_s[...] / jnp.concatenate([l, l], axis=1)
            o = jnp.where(lane_head == h, oh, o)
        od_ref[pl.ds(t0, TILE), :] = o
        return carry

    if nt == 1:
        q_tile(0, 0)
    else:
        lax.fori_loop(0, nt, q_tile, 0)


def _mla_call(pd, t_len, nb, blk_off, prev, n_ctx, use_rope, ckv_c, kr_c, cs, sn, wts):
    qnorm, wuq, qnw, kvnorm, wukv, knw, ones128 = wts
    t_k = n_ctx + t_len
    c2 = lambda shp: pl.BlockSpec(shp, lambda b: (0,) * len(shp))
    return _seq_call(
        functools.partial(_mla_body, t_len, n_ctx, use_rope), 12,
        [_seq_spec(t_len, PD_W, blk_off),
         pl.BlockSpec((1, 256, 128), lambda b: (b, 0, 0)),
         pl.BlockSpec((1, 256, 128), lambda b: (b, 0, 0)),
         c2((t_len, 128)), c2((t_len, 128)),
         c2((1, 256)), c2((256, 512)), c2((1, 512)), c2((1, 128)), c2((128, 768)), c2((1, 512)),
         c2((512, 512))],
        (pd, ckv_c, kr_c, cs, sn, qnorm, wuq, qnw, kvnorm, wukv, knw, ones128), prev,
        grid=(nb,),
        out_specs=[pl.BlockSpec((t_len, 256), lambda b: (b + blk_off, 0)),
                   pl.BlockSpec((t_len, 128), lambda b: (b + blk_off, 0))],
        out_shape=[jax.ShapeDtypeStruct((NTOK, 256), F32), jax.ShapeDtypeStruct((NTOK, 128), F32)],
        scratch_shapes=[pltpu.VMEM((t_k, 512), BF16), pltpu.VMEM((t_k, 256), BF16),
                        pltpu.VMEM((TILE, 128), F32), pltpu.VMEM((TILE, 128), F32), pltpu.VMEM((TILE, 256), F32)],
        compiler_params=_cparams(("arbitrary",), VMEM_LIMIT),
        name="mla",
    )


def _out_body(oa_ref, ob_ref, oc_ref, od_ref, x_ref, g1_ref, sc_ref, sh_ref, nw_ref, wout_ref,
              wrh_ref, wrl_ref, wrth_ref, wrtl_ref, x1_ref, h2e_ref, afft_ref):
    m = _dot(oa_ref[...].astype(BF16), wout_ref[0:256, :])
    m = m + _dot(ob_ref[...].astype(BF16), wout_ref[256:512, :])
    m = m + _dot(oc_ref[...].astype(BF16), wout_ref[512:768, :])
    m = m + _dot(od_ref[...].astype(BF16), wout_ref[768:1024, :])
    x1 = x_ref[...] + g1_ref[0] * m
    x1_ref[...] = x1
    h2 = _rms(x1, nw_ref[...]) * (1.0 + sc_ref[0]) + sh_ref[0]
    hh, hl = _split_bf16(h2)
    lg = _dot(hh, wrh_ref[...]) + _dot(hl, wrh_ref[...]) + _dot(hh, wrl_ref[...])
    lane = lax.broadcasted_iota(I32, (1, 128), 1)
    lg = jnp.where(lane < N_EXP, lg, -jnp.inf)
    ex = jnp.exp(lg - jnp.max(lg, axis=1, keepdims=True))
    aff = ex / jnp.sum(ex, axis=1, keepdims=True)
    a_hi = aff.astype(BF16).astype(F32)
    a_lo = aff - a_hi
    ext = a_hi + pltpu.roll(a_lo, N_EXP, 1)
    h2e_ref[:, 0:D] = hh
    h2e_ref[:, D:HEXT] = ext.astype(BF16)
    lt = _dot_nt(wrth_ref[...], hh) + _dot_nt(wrth_ref[...], hl) + _dot_nt(wrtl_ref[...], hh)
    et = jnp.exp(lt - jnp.max(lt, axis=0, keepdims=True))
    afft_ref[...] = et / jnp.sum(et, axis=0, keepdims=True)


def _out_call(oa, ob, oc, od, x, g1, sc2, sh2, norm2, wout_b, wr_h, wr_l, wrt_h, wrt_l):
    tm = 512
    n = NTOK // tm
    per_seg = SEG // tm
    row = lambda i: (i, 0)
    c2 = lambda shp: pl.BlockSpec(shp, lambda i: (0,) * len(shp))
    modspec = pl.BlockSpec((1, 1, D), lambda i: (i // per_seg, 0, 0))
    grp = pl.BlockSpec((tm, 256), row)
    return pl.pallas_call(
        _out_body,
        grid=(n,),
        in_specs=[grp, grp, grp, grp, pl.BlockSpec((tm, D), row), modspec, modspec, modspec,
                  c2((1, D)), c2((D, D)), c2((D, 128)), c2((D, 128)), c2((N_EXP, D)), c2((N_EXP, D))],
        out_specs=[pl.BlockSpec((tm, D), row), pl.BlockSpec((tm, HEXT), row),
                   pl.BlockSpec((N_EXP, tm), lambda i: (0, i))],
        out_shape=[jax.ShapeDtypeStruct((NTOK, D), F32), jax.ShapeDtypeStruct((NTOK, HEXT), BF16),
                   jax.ShapeDtypeStruct((N_EXP, NTOK), F32)],
        compiler_params=_cparams(("arbitrary",), VMEM_LIMIT),
        name="out_proj",
    )(oa, ob, oc, od, x, g1, sc2, sh2, norm2, wout_b, wr_h, wr_l, wrt_h, wrt_l)


def _sel_body(n_grp, cap, aff_ref, tri_ref, segt_ref, slot_ref, cum_ref, pref_s):
    w = SEG // n_grp
    nblk = SEG // 256
    aff = aff_ref[...]
    pref_s[...] = jnp.zeros((N_EXP, SEG), I32)

    def grp_cols(fn):
        return jnp.concatenate([jnp.broadcast_to(fn(g), (N_EXP, w)) for g in range(n_grp)], axis=1)

    def it(i, carry):
        bit = lax.shift_left(jnp.int32(1), 30 - i)
        cand = pref_s[...] | bit
        ge = jnp.where(aff >= pltpu.bitcast(cand, F32), 1.0, 0.0)
        ok = grp_cols(lambda g: jnp.where(
            jnp.sum(ge[:, g * w:(g + 1) * w], axis=1, keepdims=True) >= cap, 1.0, 0.0))
        pref_s[...] = jnp.where(ok > 0.5, cand, pref_s[...])
        return carry

    lax.fori_loop(0, 31, it, 0)
    thr = pref_s[...]

    def grp_cumsum(x):
        outs, off = [], None
        for b in range(nblk):
            loc = _dot(x[:, 256 * b:256 * (b + 1)].astype(BF16), tri_ref[...])
            if (256 * b) % w == 0:
                off = None
            if off is not None:
                loc = loc + off
            off = loc[:, 255:256]
            outs.append(loc)
        return jnp.concatenate(outs, axis=1)

    gt = jnp.where(aff >= pltpu.bitcast(thr + 1, F32), 1.0, 0.0)
    eq = jnp.where(aff >= pltpu.bitcast(thr, F32), 1.0, 0.0) - gt
    room = grp_cols(lambda g: cap - jnp.sum(gt[:, g * w:(g + 1) * w], axis=1, keepdims=True))
    sel = jnp.where((gt > 0.5) | ((eq > 0.5) & (grp_cumsum(eq) <= room)), 1.0, 0.0)
    base = grp_cols(lambda g: jnp.full((N_EXP, 1), float(g * cap), F32))
    slot = base + grp_cumsum(sel) - 1.0
    slot_ref[0] = jnp.where(sel > 0.5, slot, -1.0).astype(I32)
    cum_ref[0] = _dot(sel.astype(BF16), segt_ref[...]).astype(I32)


def _sel_call(afft, n_seg, seg_off, n_grp, cap, tri, segt):
    c2 = lambda shp: pl.BlockSpec(shp, lambda s: (0,) * len(shp))
    return pl.pallas_call(
        functools.partial(_sel_body, n_grp, cap),
        grid=(n_seg,),
        in_specs=[pl.BlockSpec((N_EXP, SEG), lambda s: (0, s + seg_off)), c2((256, 256)), c2((SEG, 128))],
        out_specs=[pl.BlockSpec((1, N_EXP, SEG), lambda s: (s, 0, 0)),
                   pl.BlockSpec((1, N_EXP, 128), lambda s: (s, 0, 0))],
        out_shape=[jax.ShapeDtypeStruct((n_seg, N_EXP, SEG), I32), jax.ShapeDtypeStruct((n_seg, N_EXP, 128), I32)],
        scratch_shapes=[pltpu.VMEM((N_EXP, SEG), I32)],
        compiler_params=_cparams(("arbitrary",), VMEM_LIMIT),
        name="select",
    )(afft, tri, segt)


def _windows(cum_ref, s, e, tk):
    lo = cum_ref[s, e, tk]
    hi = cum_ref[s, e, tk + 1]
    w0 = lax.shift_left(lax.shift_right_logical(lo, 4), 4)
    nw = jnp.where(hi > lo, lax.shift_right_logical(hi - w0 + (WIN - 1), 7), 0)
    return w0, nw


def _onehot(base, slot_row):
    rows = base + lax.broadcasted_iota(I32, (WIN, 1), 0)
    return jnp.where(rows == slot_row, 1.0, 0.0).astype(BF16)


def _gather_body(cum_ref, slot_ref, h2e_ref, g_ref):
    s, e = pl.program_id(0), pl.program_id(1)
    g_ref[...] = jnp.zeros(g_ref.shape, BF16)
    for tk in range(SEG // TK):
        w0, nw = _windows(cum_ref, s, e, tk)
        slot_row = slot_ref[0, 0, :, tk * TK:(tk + 1) * TK]

        def wbody(w, carry, slot_row=slot_row, tk=tk, w0=w0):
            base = pl.multiple_of(w0 + w * WIN, 16)
            got = _dot(_onehot(base, slot_row), h2e_ref[tk * TK:(tk + 1) * TK, :])
            g_ref[0, 0, pl.ds(base, WIN), :] = g_ref[0, 0, pl.ds(base, WIN), :] + got.astype(BF16)
            return carry

        lax.fori_loop(0, nw, wbody, 0)


def _gather_call(cum, slot4, h2e):
    return pl.pallas_call(
        _gather_body,
        grid_spec=pltpu.PrefetchScalarGridSpec(
            num_scalar_prefetch=1, grid=(NSEG, N_EXP),
            in_specs=[pl.BlockSpec((1, 1, 1, SEG), lambda s, e, c: (s, e, 0, 0)),
                      pl.BlockSpec((SEG, HEXT), lambda s, e, c: (s, 0))],
            out_specs=pl.BlockSpec((1, 1, SLOT_PAD, HEXT), lambda s, e, c: (s, e, 0, 0))),
        out_shape=jax.ShapeDtypeStruct((NSEG, N_EXP, SLOT_PAD, HEXT), BF16),
        compiler_params=_cparams(("arbitrary", "arbitrary"), VMEM_LIMIT),
        name="moe_gather",
    )(cum, slot4, h2e)


def _ffn_body(g_ref, wg_ref, wu_ref, wd_ref, y_ref, wgb_s, wub_s, wdb_s):
    e = pl.program_id(0)
    wgb_s[...] = wg_ref[0].astype(BF16)
    wub_s[...] = wu_ref[0].astype(BF16)
    wdb_s[...] = wd_ref[0].astype(BF16)
    lane = lax.broadcasted_iota(I32, (1, 128), 1)
    pick = (lane == e) | (lane == e + N_EXP)
    for s in range(NSEG):
        xs = g_ref[s, 0, :, 0:D]
        ext = g_ref[s, 0, :, D:HEXT].astype(F32)
        gate = jnp.sum(jnp.where(pick, ext, 0.0), axis=1, keepdims=True)
        a = _dot(xs, wgb_s[...])
        u = _dot(xs, wub_s[...])
        y = _dot((_silu(a) * u).astype(BF16), wdb_s[...]) * gate
        y_ref[s, 0, 0:CAP_SEG, :] = y.astype(BF16)
        y_ref[s, 0, CAP_SEG:SLOT_PAD, :] = jnp.zeros((SLOT_PAD - CAP_SEG, D), BF16)


def _ffn_call(gath, wg, wu, wd):
    wspec = pl.BlockSpec((1, D, D), lambda e: (e, 0, 0))
    return pl.pallas_call(
        _ffn_body,
        grid=(N_EXP,),
        in_specs=[pl.BlockSpec((NSEG, 1, CAP_SEG, HEXT), lambda e: (0, e, 0, 0)), wspec, wspec, wspec],
        out_specs=pl.BlockSpec((NSEG, 1, SLOT_PAD, D), lambda e: (0, e, 0, 0)),
        out_shape=jax.ShapeDtypeStruct((NSEG, N_EXP, SLOT_PAD, D), BF16),
        scratch_shapes=[pltpu.VMEM((D, D), BF16)] * 3,
        compiler_params=_cparams(("arbitrary",), VMEM_LIMIT),
        name="moe_ffn",
    )(gath, wg, wu, wd)


def _scatter_body(cum_ref, slot_ref, y_ref, o_ref):
    s, e = pl.program_id(0), pl.program_id(1)

    @pl.when(e == 0)
    def _():
        o_ref[...] = jnp.zeros(o_ref.shape, F32)

    for tk in range(SEG // TK):
        w0, nw = _windows(cum_ref, s, e, tk)
        slot_row = slot_ref[0, 0, :, tk * TK:(tk + 1) * TK]

        def wbody(w, carry, slot_row=slot_row, tk=tk, w0=w0):
            base = pl.multiple_of(w0 + w * WIN, 16)
            add = _dot_tn(_onehot(base, slot_row), y_ref[0, 0, pl.ds(base, WIN), :])
            o_ref[tk * TK:(tk + 1) * TK, :] = o_ref[tk * TK:(tk + 1) * TK, :] + add
            return carry

        lax.fori_loop(0, nw, wbody, 0)


def _scatter_call(cum, slot4, y):
    return pl.pallas_call(
        _scatter_body,
        grid_spec=pltpu.PrefetchScalarGridSpec(
            num_scalar_prefetch=1, grid=(NSEG, N_EXP),
            in_specs=[pl.BlockSpec((1, 1, 1, SEG), lambda s, e, c: (s, e, 0, 0)),
                      pl.BlockSpec((1, 1, SLOT_PAD, D), lambda s, e, c: (s, e, 0, 0))],
            out_specs=pl.BlockSpec((SEG, D), lambda s, e, c: (s, 0))),
        out_shape=jax.ShapeDtypeStruct((NTOK, D), F32),
        compiler_params=_cparams(("arbitrary", "arbitrary"), VMEM_LIMIT),
        name="moe_scatter",
    )(cum, slot4, y)


def _final_body(x1_ref, moe_ref, g2_ref, o_ref):
    o_ref[...] = x1_ref[...] + g2_ref[0] * moe_ref[...]


def _final_call(x1, moe, g2):
    tm = 512
    per_seg = SEG // tm
    tok = pl.BlockSpec((tm, D), lambda i: (i, 0))
    return pl.pallas_call(
        _final_body,
        grid=(NTOK // tm,),
        in_specs=[tok, tok, pl.BlockSpec((1, 1, D), lambda i: (i // per_seg, 0, 0))],
        out_specs=tok,
        out_shape=jax.ShapeDtypeStruct((NTOK, D), F32),
        compiler_params=_cparams(("arbitrary",), VMEM_LIMIT),
        name="final_residual",
    )(x1, moe, g2)


def _block_diag4(w):
    eye = jnp.eye(4, dtype=w.dtype)
    return jnp.einsum('hij,hg->higj', w, eye).reshape(256, 256)


def _np_block_ones(n, blk):
    i = np.arange(n) // blk
    return (i[:, None] == i[None, :]).astype(np.float32)


def _head_cols():
    j = np.arange(128)
    src = np.full(128, -1)
    src[:64] = j[:64]
    src[64:80] = 64 + 2 * (j[64:80] - 64)
    src[80:96] = 64 + 2 * (j[80:96] - 80) + 1
    return src


def _mla_weights(l, mla_q_norm, mla_w_uq, mla_kv_norm, mla_w_ukv, mla_qn, mla_kn):
    src = _head_cols()
    valid = src >= 0
    srcc = np.where(valid, src, 0)
    colq = np.concatenate([h * 96 + srcc for h in range(4)])
    maskq = jnp.asarray(np.tile(valid, 4).astype(np.float32))
    wuq = (mla_w_uq[l][:, colq] * maskq).astype(BF16)
    qnw = (jnp.tile(mla_qn[l][srcc], 4) * maskq).reshape(1, 512)
    knw = (jnp.tile(mla_kn[l][srcc], 4) * maskq).reshape(1, 512)
    jn = np.arange(128)
    nope_valid = jn < 64
    colk = np.concatenate([h * 128 + np.where(nope_valid, jn, 0) for h in range(4)])
    maskk = jnp.asarray(np.tile(nope_valid, 4).astype(np.float32))
    colv = np.concatenate([h * 128 + 64 + np.arange(64) for h in range(4)])
    wukv = jnp.concatenate([mla_w_ukv[l][:, colk] * maskk, mla_w_ukv[l][:, colv]], axis=1).astype(BF16)
    return (mla_q_norm[l].reshape(1, 256), wuq, qnw, mla_kv_norm[l].reshape(1, 128), wukv, knw,
            jnp.asarray(_np_block_ones(512, 128), BF16))


def _krope128(kr):
    z64 = jnp.zeros(kr.shape[:-1] + (64,), kr.dtype)
    z32 = jnp.zeros(kr.shape[:-1] + (32,), kr.dtype)
    return jnp.concatenate([z64, kr[..., 0::2], kr[..., 1::2], z32], axis=-1)


def _rope_tables(n_tokens, grid_w):
    rows = (np.arange(n_tokens) // grid_w).astype(np.float32)
    cols = (np.arange(n_tokens) % grid_w).astype(np.float32)
    n_freq = 8
    inv = jnp.asarray(10000.0, F32) ** (-jnp.arange(n_freq, dtype=F32) / n_freq)
    ang = jnp.concatenate([jnp.asarray(rows)[:, None] * inv, jnp.asarray(cols)[:, None] * inv], axis=-1)
    cos, sin = jnp.cos(ang), jnp.sin(ang)
    one = jnp.ones((n_tokens, 64), F32)
    zero = jnp.zeros((n_tokens, 64), F32)
    cs = jnp.concatenate([one, cos, cos, one[:, :32]], axis=1)
    sn = jnp.concatenate([zero, -sin, sin, zero[:, :32]], axis=1)
    return cs, sn


def _in_weight(w_in_l):
    a, b, c = w_in_l[:, 0:512], w_in_l[:, 512:768], w_in_l[:, 768:2048]
    cq, ckv, kr = w_in_l[:, 2048:2304], w_in_l[:, 2304:2432], w_in_l[:, 2432:2464]
    z96 = jnp.zeros((D, 96), w_in_l.dtype)
    return jnp.concatenate([a, b, c, cq, ckv, kr, z96, _krope128(kr)], axis=1).astype(BF16)


def kernel(x_prompt, x_sample, cache_mla_ckv, cache_mla_krope, state_rglru, state_hgrn, c, c_ctx, norm1_w, norm2_w, w_ada, b_ada, w_in, conv_w, conv_b, lru_wa, lru_ba, lru_wx, lru_bx, lru_lambda, pool_w, pool_scale, hgrn_lower_bounds, hgrn_norm_w, mla_q_norm, mla_w_uq, mla_kv_norm, mla_w_ukv, mla_qk_norm_q, mla_qk_norm_k, w_out, w_router, w_exp_gate, w_exp_up, w_exp_down):
    nbp, t_p = x_prompt.shape[0], x_prompt.shape[1]
    nbs, t_s = x_sample.shape[0], x_sample.shape[1]
    depth = w_in.shape[0]

    x = jnp.concatenate([x_prompt.reshape(nbp * t_p, D), x_sample.reshape(nbs * t_s, D)], axis=0)
    cond8 = jnp.concatenate([c_ctx[None], c, jnp.zeros((5, D), F32)], axis=0)
    mod = _ada_call(cond8, w_ada, b_ada)

    lb_soft = jax.nn.softmax(hgrn_lower_bounds.astype(F32), axis=1)
    lower = jnp.cumsum(lb_soft, axis=1) - lb_soft[:, :1]

    ones64 = jnp.asarray(_np_block_ones(256, 64), BF16)
    tri = jnp.asarray(np.triu(np.ones((256, 256), np.float32)), BF16)
    segt = jnp.asarray((np.arange(SEG)[:, None] < TK * np.arange(128)[None, :]).astype(np.float32), BF16)
    cs_s, sn_s = _rope_tables(t_s, 64)
    cs_p, sn_p = jnp.ones((t_p, 128), F32), jnp.zeros((t_p, 128), F32)
    zero_ctx = jnp.zeros((nbp, 256, 128), F32)

    ckvs, krs, lru_states, hgrn_states = [], [], [], []
    moe, g2_prev = None, None
    for l in range(depth):
        m6 = mod[l, 0:NSEG].reshape(NSEG, 6, 1, D)
        sh1, sc1, g1, sh2, sc2, g2 = (m6[:, k] for k in range(6))
        w_in_b = _in_weight(w_in[l])
        if moe is None:
            pa, pb, pc, pd = _in_call(x, sc1, sh1, norm1_w[l].reshape(1, D), w_in_b)
        else:
            x, pa, pb, pc, pd = _in_call(x, sc1, sh1, norm1_w[l].reshape(1, D), w_in_b, (moe, g2_prev))

        lru_w = (conv_w[l], conv_b[l].reshape(1, 256),
                 jnp.stack([_block_diag4(lru_wa[l, d]) for d in range(2)]).astype(BF16), lru_ba[l].reshape(2, 1, 256),
                 jnp.stack([_block_diag4(lru_wx[l, d]) for d in range(2)]).astype(BF16), lru_bx[l].reshape(2, 1, 256),
                 lru_lambda[l].reshape(2, 1, 256))
        oa, lru_fin = _lru_call(pa, t_p, nbp, 0, None, *lru_w, jnp.zeros((nbp, 2, 256), F32))
        oa, _ = _lru_call(pa, t_s, nbs, 1, oa, *lru_w, state_rglru[:, l].astype(F32))
        pw_bd = _block_diag4(pool_w[l]).astype(BF16)
        ob = _pool_call(pb, t_p, nbp, 0, None, pw_bd, pool_scale[l].reshape(1, 256))
        ob = _pool_call(pb, t_s, nbs, 1, ob, pw_bd, pool_scale[l].reshape(1, 256))
        eye4 = jnp.eye(4, dtype=F32)
        s0t = jnp.einsum('bzhdv,hg->bzhvgd', state_hgrn[:, l].astype(F32), eye4).reshape(nbs, 2, 256, 256)
        hg_w = (lower[:, l].reshape(2, 1, 256), jnp.tile(hgrn_norm_w[l], 4).reshape(1, 256))
        oc, st_p = _hgrn_call(pc, t_p, nbp, 0, None, *hg_w, jnp.zeros((nbp, 2, 256, 256), F32), ones64)
        oc, _ = _hgrn_call(pc, t_s, nbs, 1, oc, *hg_w, s0t, ones64)
        mw = _mla_weights(l, mla_q_norm, mla_w_uq, mla_kv_norm, mla_w_ukv, mla_qk_norm_q, mla_qk_norm_k)
        od, ckvn = _mla_call(pd, t_p, nbp, 0, None, 0, False, zero_ctx, zero_ctx, cs_p, sn_p, mw)
        od, _ = _mla_call(pd, t_s, nbs, 1, od, 256, True, cache_mla_ckv[:, l], _krope128(cache_mla_krope[:, l]),
                          cs_s, sn_s, mw)
        n_p = nbp * t_p

        wr_h, wr_l = _split_bf16(jnp.pad(w_router[l], ((0, 0), (0, 128 - N_EXP))))
        wrt_h, wrt_l = _split_bf16(w_router[l].T)
        x1, h2e, afft = _out_call(oa, ob, oc, od, x, g1, sc2, sh2, norm2_w[l].reshape(1, D),
                                  w_out[l].astype(BF16), wr_h, wr_l, wrt_h, wrt_l)

        slot_p, cum_p = _sel_call(afft, 1, 0, nbp, 2 * t_p // N_EXP, tri, segt)
        slot_s, cum_s = _sel_call(afft, nbs, 1, 1, 2 * t_s // N_EXP, tri, segt)
        slot4 = jnp.concatenate([slot_p, slot_s], axis=0).reshape(NSEG, N_EXP, 1, SEG)
        cum = jnp.concatenate([cum_p, cum_s], axis=0)
        gath = _gather_call(cum, slot4, h2e)
        y = _ffn_call(gath, w_exp_gate[l], w_exp_up[l], w_exp_down[l])
        moe = _scatter_call(cum, slot4, y)
        x, g2_prev = x1, g2

        ckvs.append(ckvn[:n_p].reshape(nbp, t_p, 128))
        krs.append(pd[:n_p, 384:416].reshape(nbp, t_p, 32))
        lru_states.append(lru_fin)
        st6 = st_p.reshape(nbp, 2, 4, 64, 4, 64)
        hgrn_states.append(jnp.einsum('bzhvgd,hg->bzhdv', st6, eye4))

    y_all = _final_call(x, moe, g2_prev)
    n_p = nbp * t_p
    return (y_all[:n_p].reshape(nbp, t_p, D), y_all[n_p:].reshape(nbs, t_s, D),
            jnp.stack(ckvs, axis=1), jnp.stack(krs, axis=1),
            jnp.stack(lru_states, axis=1), jnp.stack(hgrn_states, axis=1))
```

```python
import functools

import numpy as np
import jax
import jax.numpy as jnp
from jax import lax
from jax.experimental import pallas as pl
from jax.experimental.pallas import tpu as pltpu

F32 = jnp.float32
BF16 = jnp.bfloat16
I32 = jnp.int32

D = 1024
NTOK = 12288
SEG = 4096
NSEG = 3
EPS = 1e-6
TINY = 1e-30
LRU_C = 8.0
N_EXP = 16
CAP_SEG = 512
SLOT_PAD = 640
TILE = 256
TK = 512
WIN = 128
HEXT = D + 128
VMEM_LIMIT = 56 * 1024 * 1024

PA_W, PB_W, PC_W, PD_W = 512, 256, 1280, 640
IN_PAD_W = PA_W + PB_W + PC_W + PD_W


def _cparams(sem, vmem=None):
    return pltpu.CompilerParams(dimension_semantics=sem, vmem_limit_bytes=vmem)


def _dot(a, b):
    return jnp.dot(a, b, preferred_element_type=F32)


def _dot_nt(a, b):
    return lax.dot_general(a, b, (((1,), (1,)), ((), ())), preferred_element_type=F32)


def _dot_tn(a, b):
    return lax.dot_general(a, b, (((0,), (0,)), ((), ())), preferred_element_type=F32)


def _rms(x, w):
    ms = jnp.mean(x * x, axis=-1, keepdims=True)
    return x * lax.rsqrt(ms + EPS) * w


def _silu(x):
    return x * jax.nn.sigmoid(x)


def _split_bf16(x):
    hi = x.astype(BF16)
    lo = (x - hi.astype(F32)).astype(BF16)
    return hi, lo


def _segsum2(x, ones_blk):
    hi, lo = _split_bf16(x)
    return _dot(hi, ones_blk) + _dot(lo, ones_blk)


def _cumsum_rows(x, r8, rev):
    n = x.shape[0]
    for s in (1, 2, 4):
        if not rev:
            x = jnp.where(r8 >= s, x + pltpu.roll(x, s, 0), x)
        else:
            x = jnp.where(r8 < 8 - s, x + pltpu.roll(x, n - s, 0), x)
    ng = n // 8
    outs = [None] * ng
    c = None
    for g in (range(ng) if not rev else reversed(range(ng))):
        xg = x[8 * g:8 * g + 8]
        if c is not None:
            xg = xg + c
        c = xg[7:8] if not rev else xg[0:1]
        outs[g] = xg
    return jnp.concatenate(outs, axis=0)


def _ada_body(c_ref, w_ref, b_ref, o_ref):
    s = _silu(c_ref[...])
    o_ref[0] = _dot(s.astype(BF16), w_ref[0].astype(BF16)) + b_ref[0]


def _ada_call(cond8, w_ada, b_ada):
    nj = 4
    wj = 6 * D // nj
    return pl.pallas_call(
        _ada_body,
        grid=(2, nj),
        in_specs=[pl.BlockSpec((8, D), lambda l, j: (0, 0)),
                  pl.BlockSpec((1, D, wj), lambda l, j: (l, 0, j)),
                  pl.BlockSpec((1, 1, wj), lambda l, j: (l, 0, j))],
        out_specs=pl.BlockSpec((1, 8, wj), lambda l, j: (l, 0, j)),
        out_shape=jax.ShapeDtypeStruct((2, 8, 6 * D), F32),
        compiler_params=_cparams(("arbitrary", "arbitrary"), VMEM_LIMIT),
        name="ada",
    )(cond8, w_ada, b_ada.reshape(2, 1, 6 * D))


def _in_body(n_ctx_steps, *refs):
    if n_ctx_steps is None:
        x1_ref, moe_ref, g2_ref, sc_ref, sh_ref, nw_ref, w_ref, x_ref, pa_ref, pb_ref, pc_ref, pd_ref = refs
        x = x1_ref[...] + g2_ref[0] * moe_ref[...]
    else:
        xc_ref, xl_ref, sc_ref, sh_ref, nw_ref, w_ref, x_ref, pa_ref, pb_ref, pc_ref, pd_ref = refs
        x = jnp.where(pl.program_id(0) < n_ctx_steps, xc_ref[...], xl_ref[...])
    x_ref[...] = x
    h = _rms(x, nw_ref[...]) * (1.0 + sc_ref[0]) + sh_ref[0]
    hb = h.astype(BF16)
    o = 0
    for ref, w in ((pa_ref, PA_W), (pb_ref, PB_W), (pc_ref, PC_W), (pd_ref, PD_W)):
        ref[...] = _dot(hb, w_ref[:, o:o + w])
        o += w


def _in_call(src, sc1, sh1, norm1, w_in_b):
    tm = 512
    n = NTOK // tm
    per_seg = SEG // tm
    row = lambda i: (i, 0)
    modspec = pl.BlockSpec((1, 1, D), lambda i: (i // per_seg, 0, 0))
    tokspec = pl.BlockSpec((tm, D), row)
    outs = [jax.ShapeDtypeStruct((NTOK, w), F32) for w in (D, PA_W, PB_W, PC_W, PD_W)]
    out_specs = [pl.BlockSpec((tm, w), row) for w in (D, PA_W, PB_W, PC_W, PD_W)]
    common_specs = [modspec, modspec, pl.BlockSpec((1, D), lambda i: (0, 0)),
                    pl.BlockSpec((D, IN_PAD_W), lambda i: (0, 0))]
    if len(src) == 2:
        n_ctx_steps = src[0].shape[0] // tm
        args = (*src, sc1, sh1, norm1, w_in_b)
        in_specs = [pl.BlockSpec((tm, D), lambda i: (jnp.minimum(i, n_ctx_steps - 1), 0)),
                    pl.BlockSpec((tm, D), lambda i: (jnp.maximum(i - n_ctx_steps, 0), 0))] + common_specs
    else:
        n_ctx_steps = None
        args = (*src, sc1, sh1, norm1, w_in_b)
        in_specs = [tokspec, tokspec, modspec] + common_specs
    return pl.pallas_call(
        functools.partial(_in_body, n_ctx_steps),
        grid=(n,), in_specs=in_specs, out_specs=out_specs, out_shape=outs,
        compiler_params=_cparams(("arbitrary",), VMEM_LIMIT),
        name="in_proj",
    )(*args)


def _halo_tile(ref, c0, c1, t0, t_len, static_single):
    xa = ref[pl.ds(t0, TILE), c0:c1]
    if static_single:
        z = jnp.zeros((8, c1 - c0), F32)
        return xa, jnp.concatenate([z, xa, z], axis=0)
    ps = pl.multiple_of(jnp.maximum(t0 - 8, 0), 8)
    ns = pl.multiple_of(jnp.minimum(t0 + TILE, t_len - 8), 8)
    prev = jnp.where(t0 > 0, ref[pl.ds(ps, 8), c0:c1], 0.0)
    nxt = jnp.where(t0 + TILE < t_len, ref[pl.ds(ns, 8), c0:c1], 0.0)
    return xa, jnp.concatenate([prev, xa, nxt], axis=0)


def _seq_spec(t_len, width, blk_off):
    idx = lambda b: (b + blk_off, 0)
    if t_len > TILE:
        return pl.BlockSpec((t_len, width), idx, pipeline_mode=pl.Buffered(1))
    return pl.BlockSpec((t_len, width), idx)


def _gelu_tanh(x):
    return 0.5 * x * (1.0 + jnp.tanh(0.7978845608028654 * (x + 0.044715 * (x * x * x))))


def _softplus(x):
    return jnp.maximum(x, 0.0) + jnp.log1p(jnp.exp(-jnp.abs(x)))


def _lru_scan(a, u, c, r8, rev):
    n = a.shape[0]
    for s in (1, 2, 4):
        if not rev:
            m = r8 >= s
            a_sh, u_sh = pltpu.roll(a, s, 0), pltpu.roll(u, s, 0)
        else:
            m = r8 < 8 - s
            a_sh, u_sh = pltpu.roll(a, n - s, 0), pltpu.roll(u, n - s, 0)
        u = jnp.where(m, a * u_sh + u, u)
        a = jnp.where(m, a * a_sh, a)
    ng = n // 8
    outs = [None] * ng
    for g in (range(ng) if not rev else reversed(range(ng))):
        hg = u[8 * g:8 * g + 8] + a[8 * g:8 * g + 8] * c
        c = hg[7:8] if not rev else hg[0:1]
        outs[g] = hg
    return jnp.concatenate(outs, axis=0), c


def _lru_body(t_len, pa_ref, cw_ref, cb_ref, wa_ref, ba_ref, wx_ref, bx_ref, lam_ref, h0_ref,
              oa_ref, hfin_ref, hf_s, ab_s, ub_s):
    nt = t_len // TILE
    single = nt == 1
    r8 = lax.broadcasted_iota(I32, (TILE, 1), 0) & 7
    n_ext = TILE + 16

    def gates(xc, xb, d):
        r = jax.nn.sigmoid(_dot(xb, wa_ref[d]) + ba_ref[d])
        i = jax.nn.sigmoid(_dot(xb, wx_ref[d]) + bx_ref[d])
        log_a = -LRU_C * r * _softplus(-lam_ref[d])
        a = jnp.exp(log_a)
        th = jnp.tanh(log_a)
        mult = jnp.sqrt(jnp.maximum(-2.0 * th / (1.0 - th), 0.0))
        return a, mult * (i * xc)

    def fwd_tile(i, c):
        t0 = pl.multiple_of(i * TILE, TILE)
        xa, ext = _halo_tile(pa_ref, 0, 256, t0, t_len, single)
        xc = cb_ref[...] + xa * cw_ref[1:2, :]
        for j in (0, 2, 3):
            xc = xc + pltpu.roll(ext, n_ext - 7 - j, 0)[0:TILE] * cw_ref[j:j + 1, :]
        xb = xc.astype(BF16)
        a_f, u_f = gates(xc, xb, 0)
        h, c = _lru_scan(a_f, u_f, c, r8, False)
        hf_s[pl.ds(t0, TILE), :] = h
        a_b, u_b = gates(xc, xb, 1)
        ab_s[pl.ds(t0, TILE), :] = a_b
        ub_s[pl.ds(t0, TILE), :] = u_b
        return c

    def bwd_tile(k, c):
        t0 = pl.multiple_of((nt - 1 - k) * TILE, TILE)
        h_b, c = _lru_scan(ab_s[pl.ds(t0, TILE), :], ub_s[pl.ds(t0, TILE), :], c, r8, True)
        gate = pa_ref[pl.ds(t0, TILE), 256:512]
        oa_ref[pl.ds(t0, TILE), :] = (hf_s[pl.ds(t0, TILE), :] + h_b) * _gelu_tanh(gate)
        return c

    h0 = h0_ref[0]
    if single:
        c_f = fwd_tile(0, h0[0:1])
        c_b = bwd_tile(0, h0[1:2])
    else:
        c_f = lax.fori_loop(0, nt, fwd_tile, h0[0:1])
        c_b = lax.fori_loop(0, nt, bwd_tile, h0[1:2])
    hfin_ref[0, 0:1, :] = c_f
    hfin_ref[0, 1:2, :] = c_b


def _lru_call(pa, t_len, nb, blk_off, conv_w, conv_b, wa_bd, ba, wx_bd, bx, lam, h0):
    full2 = lambda shp: pl.BlockSpec(shp, lambda b: (0,) * len(shp))
    return pl.pallas_call(
        functools.partial(_lru_body, t_len),
        grid=(nb,),
        in_specs=[_seq_spec(t_len, PA_W, blk_off),
                  full2((4, 256)), full2((1, 256)), full2((2, 256, 256)), full2((2, 1, 256)),
                  full2((2, 256, 256)), full2((2, 1, 256)), full2((2, 1, 256)),
                  pl.BlockSpec((1, 2, 256), lambda b: (b, 0, 0))],
        out_specs=[pl.BlockSpec((t_len, 256), lambda b: (b, 0)),
                   pl.BlockSpec((1, 2, 256), lambda b: (b, 0, 0))],
        out_shape=[jax.ShapeDtypeStruct((nb * t_len, 256), F32), jax.ShapeDtypeStruct((nb, 2, 256), F32)],
        scratch_shapes=[pltpu.VMEM((t_len, 256), F32)] * 3,
        compiler_params=_cparams(("arbitrary",), VMEM_LIMIT),
        name="lru",
    )(pa, conv_w, conv_b, wa_bd, ba, wx_bd, bx, lam, h0)


def _pool_body(t_len, pb_ref, pw_ref, ps_ref, ob_ref):
    nt = t_len // TILE
    single = nt == 1
    n_ext = TILE + 16
    lane = lax.broadcasted_iota(I32, (1, 256), 1)
    rowi = lax.broadcasted_iota(I32, (TILE, 1), 0)

    def ahead(x, k):
        return pltpu.roll(x, n_ext - k, 0)

    def tile(i, carry):
        t0 = pl.multiple_of(i * TILE, TILE)
        xa, ext = _halo_tile(pb_ref, 0, 256, t0, t_len, single)
        p2 = ext + ahead(ext, 1)
        p4 = p2 + ahead(p2, 2)
        p8 = p4 + ahead(p4, 4)
        p16 = p8 + ahead(p8, 8)
        sums = (ahead(p2, 7)[0:TILE], ahead(p4, 6)[0:TILE], ahead(p8, 4)[0:TILE], p16[0:TILE])
        tpos = t0 + rowi
        means = []
        for w, s in zip((2, 4, 8, 16), sums):
            cnt = jnp.minimum(tpos + w // 2, t_len) - jnp.maximum(tpos - w // 2, 0)
            means.append(s / cnt.astype(F32))
        mean = jnp.where(lane < 64, means[0], jnp.where(lane < 128, means[1],
                                                          jnp.where(lane < 192, means[2], means[3])))
        pooled = mean - xa
        ob_ref[pl.ds(t0, TILE), :] = _dot(pooled.astype(BF16), pw_ref[...]) * ps_ref[...]
        return carry

    if single:
        tile(0, 0)
    else:
        lax.fori_loop(0, nt, tile, 0)


def _pool_call(pb, t_len, nb, blk_off, pw_bd, pscale):
    return pl.pallas_call(
        functools.partial(_pool_body, t_len),
        grid=(nb,),
        in_specs=[_seq_spec(t_len, PB_W, blk_off),
                  pl.BlockSpec((256, 256), lambda b: (0, 0)),
                  pl.BlockSpec((1, 256), lambda b: (0, 0))],
        out_specs=pl.BlockSpec((t_len, 256), lambda b: (b, 0)),
        out_shape=jax.ShapeDtypeStruct((nb * t_len, 256), F32),
        compiler_params=_cparams(("arbitrary",), VMEM_LIMIT),
        name="pool",
    )(pb, pw_bd, pscale)


HGRN_LEVELS = (1, 2, 4, 8, 16, 32, 64, 128)


def _hgrn_dir(rev, q, k, v, lf, st_s, att_s, ones_ref, rowi, r8, coli, lane_head):
    c_rows = TILE
    g = _cumsum_rows(lf, r8, rev)
    vb = v.astype(BF16)
    seg = g
    row4 = lax.broadcasted_iota(I32, (4 * c_rows, 1), 0)
    t4 = row4 & (c_rows - 1)
    h4 = row4 >> 8
    for li, m in enumerate(HGRN_LEVELS):
        up = (rowi & (2 * m - 1)) >= m
        if not rev:
            ref_q = pltpu.roll(seg, m, 0)
            qsel, ksel = up, jnp.logical_not(up)
            seg_next = jnp.where(up, seg, pltpu.roll(seg, c_rows - m, 0))
        else:
            ref_q = pltpu.roll(seg, c_rows - m, 0)
            qsel, ksel = jnp.logical_not(up), up
            seg_next = jnp.where(up, pltpu.roll(seg, m, 0), seg)
        qp = jnp.where(qsel, q * jnp.exp(jnp.minimum(g - ref_q, 0.0)), 0.0)
        kp = jnp.where(ksel, k * jnp.exp(jnp.minimum(seg - g, 0.0)), 0.0)
        qs = jnp.concatenate([jnp.where(lane_head == h, qp, 0.0) for h in range(4)], axis=0).astype(BF16)
        prod = _dot_nt(qs, kp.astype(BF16))
        sh = int(np.log2(2 * m))
        pair = (t4 >> sh) == (coli >> sh)
        contrib = jnp.where(pair, prod, 0.0)
        if li == 0:
            att_s[...] = contrib
        else:
            att_s[...] = att_s[...] + contrib
        seg = seg_next
    o = _dot((q * k).astype(BF16), ones_ref[...]) * v
    for h in range(4):
        oh = _dot(att_s[h * c_rows:(h + 1) * c_rows, :].astype(BF16), vb)
        o = o + jnp.where(lane_head == h, oh, 0.0)
    st = st_s[...]
    o = o + _dot_nt((q * jnp.exp(g)).astype(BF16), st.astype(BF16))
    g_end = g[c_rows - 1:c_rows] if not rev else g[0:1]
    kd = k * jnp.exp(g_end - g)
    upd = _dot_tn(vb, kd.astype(BF16))
    blk = (lax.broadcasted_iota(I32, (256, 1), 0) >> 6) == (lax.broadcasted_iota(I32, (1, 256), 1) >> 6)
    st_s[...] = st * jnp.exp(g_end) + jnp.where(blk, upd, 0.0)
    return o


def _hgrn_body(t_len, pc_ref, lb_ref, nw_ref, s0_ref, ones_ref, oc_ref, sfin_ref, of_s, st_s, att_s):
    nt = t_len // TILE
    rowi = lax.broadcasted_iota(I32, (TILE, 1), 0)
    r8 = rowi & 7
    coli = lax.broadcasted_iota(I32, (1, TILE), 1)
    lane_head = lax.broadcasted_iota(I32, (1, 256), 1) >> 6

    def load(t0, d):
        q = _silu(pc_ref[pl.ds(t0, TILE), 0:256]) * 0.125
        f_raw = pc_ref[pl.ds(t0, TILE), 256 * (1 + d):256 * (2 + d)]
        v = pc_ref[pl.ds(t0, TILE), 768:1024]
        lb = lb_ref[d]
        f_val = lb + (1.0 - lb) * jax.nn.sigmoid(f_raw)
        lf = jnp.log(jnp.maximum(f_val, TINY))
        return q, 1.0 - f_val, v, lf

    def fwd_tile(i, carry):
        t0 = pl.multiple_of(i * TILE, TILE)
        q, k, v, lf = load(t0, 0)
        of_s[pl.ds(t0, TILE), :] = _hgrn_dir(False, q, k, v, lf, st_s, att_s, ones_ref, rowi, r8, coli, lane_head)
        return carry

    def bwd_tile(kk, carry):
        t0 = pl.multiple_of((nt - 1 - kk) * TILE, TILE)
        q, k, v, lf = load(t0, 1)
        o = of_s[pl.ds(t0, TILE), :] + _hgrn_dir(True, q, k, v, lf, st_s, att_s, ones_ref, rowi, r8, coli, lane_head)
        ms = _segsum2(o * o, ones_ref[...]) * (1.0 / 64.0)
        y = o * lax.rsqrt(ms + EPS) * nw_ref[...]
        oc_ref[pl.ds(t0, TILE), :] = y * _silu(pc_ref[pl.ds(t0, TILE), 1024:1280])
        return carry

    st_s[...] = s0_ref[0, 0]
    if nt == 1:
        fwd_tile(0, 0)
    else:
        lax.fori_loop(0, nt, fwd_tile, 0)
    sfin_ref[0, 0] = st_s[...]
    st_s[...] = s0_ref[0, 1]
    if nt == 1:
        bwd_tile(0, 0)
    else:
        lax.fori_loop(0, nt, bwd_tile, 0)
    sfin_ref[0, 1] = st_s[...]


def _hgrn_call(pc, t_len, nb, blk_off, lower, normw, s0t, ones64):
    return pl.pallas_call(
        functools.partial(_hgrn_body, t_len),
        grid=(nb,),
        in_specs=[_seq_spec(t_len, PC_W, blk_off),
                  pl.BlockSpec((2, 1, 256), lambda b: (0, 0, 0)),
                  pl.BlockSpec((1, 256), lambda b: (0, 0)),
                  pl.BlockSpec((1, 2, 256, 256), lambda b: (b, 0, 0, 0)),
                  pl.BlockSpec((256, 256), lambda b: (0, 0))],
        out_specs=[pl.BlockSpec((t_len, 256), lambda b: (b, 0)),
                   pl.BlockSpec((1, 2, 256, 256), lambda b: (b, 0, 0, 0))],
        out_shape=[jax.ShapeDtypeStruct((nb * t_len, 256), F32), jax.ShapeDtypeStruct((nb, 2, 256, 256), F32)],
        scratch_shapes=[pltpu.VMEM((t_len, 256), F32), pltpu.VMEM((256, 256), F32),
                        pltpu.VMEM((4 * TILE, TILE), F32)],
        compiler_params=_cparams(("arbitrary",), VMEM_LIMIT),
        name="hgrn",
    )(pc, lower, normw, s0t, ones64)


ATT_SCALE = 96.0 ** -0.5
KEY_BLK = 512


def _rope512(x, cs128, sn128, lane128):
    cs = jnp.concatenate([cs128] * 4, axis=1)
    sn = jnp.concatenate([sn128] * 4, axis=1)
    partner = jnp.where(lane128 < 80, pltpu.roll(x, 512 - 16, 1), pltpu.roll(x, 16, 1))
    return x * cs + partner * sn


def _mla_body(t_len, n_ctx, use_rope, pd_ref, ckv_c_ref, kr_c_ref, cs_ref, sn_ref, qnorm_ref, wuq_ref, qnw_ref,
              kvnorm_ref, wukv_ref, knw_ref, ones_ref, od_ref, ckvn_ref, k_s, v_s, m_s, l_s, acc_s):
    nt = t_len // TILE
    t_k = n_ctx + t_len
    assert (t_k - TILE) % KEY_BLK == 0
    n_kb = (t_k - TILE) // KEY_BLK
    lane128 = lax.broadcasted_iota(I32, (1, 512), 1) & 127
    lane_head = lax.broadcasted_iota(I32, (1, 256), 1) >> 6

    def head_norm(x, w_ref):
        ss = _segsum2(x * x, ones_ref[...])
        return x * lax.rsqrt(ss * (1.0 / 96.0) + EPS) * w_ref[...]

    def put_kv(r0, ckv_n, kr128, rope_rows):
        kv = _dot(ckv_n.astype(BF16), wukv_ref[...])
        k_all = kv[:, 0:512] + jnp.concatenate([kr128] * 4, axis=1)
        kn = head_norm(k_all, knw_ref)
        if rope_rows is not None:
            kn = _rope512(kn, cs_ref[pl.ds(rope_rows, TILE), :], sn_ref[pl.ds(rope_rows, TILE), :], lane128)
        k_s[pl.ds(r0, TILE), :] = kn.astype(BF16)
        v_s[pl.ds(r0, TILE), :] = kv[:, 512:768].astype(BF16)

    if n_ctx:
        put_kv(0, ckv_c_ref[0], kr_c_ref[0], None)

    def kv_tile(i, carry):
        t0 = pl.multiple_of(i * TILE, TILE)
        ckv_n = _rms(pd_ref[pl.ds(t0, TILE), 256:384], kvnorm_ref[...])
        ckvn_ref[pl.ds(t0, TILE), :] = ckv_n
        put_kv(pl.multiple_of(n_ctx + t0, TILE), ckv_n, pd_ref[pl.ds(t0, TILE), 512:640],
               t0 if use_rope else None)
        return carry

    if nt == 1:
        kv_tile(0, 0)
    else:
        lax.fori_loop(0, nt, kv_tile, 0)

    def q_tile(i, carry):
        t0 = pl.multiple_of(i * TILE, TILE)
        qn = _rms(pd_ref[pl.ds(t0, TILE), 0:256], qnorm_ref[...])
        q = head_norm(_dot(qn.astype(BF16), wuq_ref[...]), qnw_ref)
        if use_rope:
            q = _rope512(q, cs_ref[pl.ds(t0, TILE), :], sn_ref[pl.ds(t0, TILE), :], lane128)
        qb = (q * ATT_SCALE).astype(BF16)
        qhs = [qb[:, 128 * h:128 * (h + 1)] for h in range(4)]

        def first_block(h):
            s = _dot_nt(qhs[h], k_s[0:TILE, 128 * h:128 * (h + 1)])
            m = jnp.max(s, axis=1, keepdims=True)
            p = jnp.exp(s - m)
            return m, jnp.sum(p, axis=1, keepdims=True), _dot(p.astype(BF16), v_s[0:TILE, :])

        if n_kb == 0:
            outs = []
            for h in range(4):
                _, l, acc = first_block(h)
                outs.append(acc / l)
        else:
            for h in range(4):
                m, l, acc = first_block(h)
                m_s[h] = jnp.broadcast_to(m, (TILE, 128))
                l_s[h] = jnp.broadcast_to(l, (TILE, 128))
                acc_s[h] = acc

            def kblock(j, c2):
                r0 = pl.multiple_of(TILE + j * KEY_BLK, TILE)
                for h in range(4):
                    s = _dot_nt(qhs[h], k_s[pl.ds(r0, KEY_BLK), 128 * h:128 * (h + 1)])
                    m_prev = m_s[h]
                    m_new = jnp.maximum(m_prev, jnp.max(s, axis=1, keepdims=True))
                    alpha = jnp.exp(m_prev - m_new)
                    p = jnp.exp(s - jnp.concatenate([m_new] * (KEY_BLK // 128), axis=1))
                    l_s[h] = alpha * l_s[h] + jnp.sum(p, axis=1, keepdims=True)
                    acc_s[h] = (jnp.concatenate([alpha, alpha], axis=1) * acc_s[h]
                                + _dot(p.astype(BF16), v_s[pl.ds(r0, KEY_BLK), :]))
                    m_s[h] = m_new
                return c2

            lax.fori_loop(0, n_kb, kblock, 0)
            outs = []
            for h in range(4):
                l = l_s[h]
                outs.append(acc_s[h] / jnp.concatenate([l, l], axis=1))
        o = outs[3]
        for h in range(3):
            o = jnp.where(lane_head == h, outs[h], o)
        od_ref[pl.ds(t0, TILE), :] = o
        return carry

    if nt == 1:
        q_tile(0, 0)
    else:
        lax.fori_loop(0, nt, q_tile, 0)


def _mla_call(pd, t_len, nb, blk_off, n_ctx, use_rope, ckv_c, kr_c, cs, sn, wts):
    qnorm, wuq, qnw, kvnorm, wukv, knw, ones128 = wts
    t_k = n_ctx + t_len
    c2 = lambda shp: pl.BlockSpec(shp, lambda b: (0,) * len(shp))
    return pl.pallas_call(
        functools.partial(_mla_body, t_len, n_ctx, use_rope),
        grid=(nb,),
        in_specs=[_seq_spec(t_len, PD_W, blk_off),
                  pl.BlockSpec((1, 256, 128), lambda b: (b, 0, 0)),
                  pl.BlockSpec((1, 256, 128), lambda b: (b, 0, 0)),
                  c2((t_len, 128)), c2((t_len, 128)),
                  c2((1, 256)), c2((256, 512)), c2((1, 512)), c2((1, 128)), c2((128, 768)), c2((1, 512)),
                  c2((512, 512))],
        out_specs=[pl.BlockSpec((t_len, 256), lambda b: (b, 0)),
                   pl.BlockSpec((t_len, 128), lambda b: (b, 0))],
        out_shape=[jax.ShapeDtypeStruct((nb * t_len, 256), F32), jax.ShapeDtypeStruct((nb * t_len, 128), F32)],
        scratch_shapes=[pltpu.VMEM((t_k, 512), BF16), pltpu.VMEM((t_k, 256), BF16),
                        pltpu.VMEM((4, TILE, 128), F32), pltpu.VMEM((4, TILE, 128), F32),
                        pltpu.VMEM((4, TILE, 256), F32)],
        compiler_params=_cparams(("arbitrary",), VMEM_LIMIT),
        name="mla",
    )(pd, ckv_c, kr_c, cs, sn, qnorm, wuq, qnw, kvnorm, wukv, knw, ones128)


def _out_body(n_ctx_steps, *refs):
    mix_refs = refs[:8]
    (x_ref, g1_ref, sc_ref, sh_ref, nw_ref, wout_ref, wrh_ref, wrl_ref, wrth_ref, wrtl_ref,
     x1_ref, h2e_ref, afft_ref) = refs[8:]
    is_ctx = pl.program_id(0) < n_ctx_steps
    m = None
    for k in range(4):
        ok = jnp.where(is_ctx, mix_refs[2 * k][...], mix_refs[2 * k + 1][...]).astype(BF16)
        mk = _dot(ok, wout_ref[256 * k:256 * (k + 1), :])
        m = mk if m is None else m + mk
    x1 = x_ref[...] + g1_ref[0] * m
    x1_ref[...] = x1
    h2 = _rms(x1, nw_ref[...]) * (1.0 + sc_ref[0]) + sh_ref[0]
    hh, hl = _split_bf16(h2)
    lg = _dot(hh, wrh_ref[...]) + _dot(hl, wrh_ref[...]) + _dot(hh, wrl_ref[...])
    lane = lax.broadcasted_iota(I32, (1, 128), 1)
    lg = jnp.where(lane < N_EXP, lg, -jnp.inf)
    ex = jnp.exp(lg - jnp.max(lg, axis=1, keepdims=True))
    aff = ex / jnp.sum(ex, axis=1, keepdims=True)
    a_hi = aff.astype(BF16).astype(F32)
    a_lo = aff - a_hi
    ext = a_hi + pltpu.roll(a_lo, N_EXP, 1)
    h2e_ref[:, 0:D] = hh
    h2e_ref[:, D:HEXT] = ext.astype(BF16)
    lt = _dot_nt(wrth_ref[...], hh) + _dot_nt(wrth_ref[...], hl) + _dot_nt(wrtl_ref[...], hh)
    et = jnp.exp(lt - jnp.max(lt, axis=0, keepdims=True))
    afft_ref[...] = et / jnp.sum(et, axis=0, keepdims=True)


def _out_call(mix, x, g1, sc2, sh2, norm2, wout_b, wr_h, wr_l, wrt_h, wrt_l):
    tm = 512
    n = NTOK // tm
    per_seg = SEG // tm
    n_ctx_steps = mix[0].shape[0] // tm
    row = lambda i: (i, 0)
    c2 = lambda shp: pl.BlockSpec(shp, lambda i: (0,) * len(shp))
    modspec = pl.BlockSpec((1, 1, D), lambda i: (i // per_seg, 0, 0))
    ctx_spec = pl.BlockSpec((tm, 256), lambda i: (jnp.minimum(i, n_ctx_steps - 1), 0))
    lat_spec = pl.BlockSpec((tm, 256), lambda i: (jnp.maximum(i - n_ctx_steps, 0), 0))
    return pl.pallas_call(
        functools.partial(_out_body, n_ctx_steps),
        grid=(n,),
        in_specs=[ctx_spec, lat_spec] * 4 + [pl.BlockSpec((tm, D), row), modspec, modspec, modspec,
                  c2((1, D)), c2((D, D)), c2((D, 128)), c2((D, 128)), c2((N_EXP, D)), c2((N_EXP, D))],
        out_specs=[pl.BlockSpec((tm, D), row), pl.BlockSpec((tm, HEXT), row),
                   pl.BlockSpec((N_EXP, tm), lambda i: (0, i))],
        out_shape=[jax.ShapeDtypeStruct((NTOK, D), F32), jax.ShapeDtypeStruct((NTOK, HEXT), BF16),
                   jax.ShapeDtypeStruct((N_EXP, NTOK), F32)],
        compiler_params=_cparams(("arbitrary",), VMEM_LIMIT),
        name="out_proj",
    )(*mix, x, g1, sc2, sh2, norm2, wout_b, wr_h, wr_l, wrt_h, wrt_l)


def _sel_body(n_grp, cap, aff_ref, tri_ref, segt_ref, slot_ref, cum_ref, pref_s):
    w = SEG // n_grp
    nblk = SEG // 256
    aff = aff_ref[...]
    pref_s[...] = jnp.zeros((N_EXP, SEG), I32)

    def grp_cols(fn):
        return jnp.concatenate([jnp.broadcast_to(fn(g), (N_EXP, w)) for g in range(n_grp)], axis=1)

    def it(i, carry):
        bit = lax.shift_left(jnp.int32(1), 30 - i)
        cand = pref_s[...] | bit
        ge = jnp.where(aff >= pltpu.bitcast(cand, F32), 1.0, 0.0)
        ok = grp_cols(lambda g: jnp.where(
            jnp.sum(ge[:, g * w:(g + 1) * w], axis=1, keepdims=True) >= cap, 1.0, 0.0))
        pref_s[...] = jnp.where(ok > 0.5, cand, pref_s[...])
        return carry

    lax.fori_loop(0, 31, it, 0)
    thr = pref_s[...]

    def grp_cumsum(x):
        outs, off = [], None
        for b in range(nblk):
            loc = _dot(x[:, 256 * b:256 * (b + 1)].astype(BF16), tri_ref[...])
            if (256 * b) % w == 0:
                off = None
            if off is not None:
                loc = loc + off
            off = loc[:, 255:256]
            outs.append(loc)
        return jnp.concatenate(outs, axis=1)

    gt = jnp.where(aff >= pltpu.bitcast(thr + 1, F32), 1.0, 0.0)
    eq = jnp.where(aff >= pltpu.bitcast(thr, F32), 1.0, 0.0) - gt
    room = grp_cols(lambda g: cap - jnp.sum(gt[:, g * w:(g + 1) * w], axis=1, keepdims=True))
    sel = jnp.where((gt > 0.5) | ((eq > 0.5) & (grp_cumsum(eq) <= room)), 1.0, 0.0)
    base = grp_cols(lambda g: jnp.full((N_EXP, 1), float(g * cap), F32))
    slot = base + grp_cumsum(sel) - 1.0
    slot_ref[0] = jnp.where(sel > 0.5, slot, -1.0).astype(I32)
    cum_ref[0] = _dot(sel.astype(BF16), segt_ref[...]).astype(I32)


def _sel_call(afft, n_seg, seg_off, n_grp, cap, tri, segt):
    c2 = lambda shp: pl.BlockSpec(shp, lambda s: (0,) * len(shp))
    return pl.pallas_call(
        functools.partial(_sel_body, n_grp, cap),
        grid=(n_seg,),
        in_specs=[pl.BlockSpec((N_EXP, SEG), lambda s: (0, s + seg_off)), c2((256, 256)), c2((SEG, 128))],
        out_specs=[pl.BlockSpec((1, N_EXP, SEG), lambda s: (s, 0, 0)),
                   pl.BlockSpec((1, N_EXP, 128), lambda s: (s, 0, 0))],
        out_shape=[jax.ShapeDtypeStruct((n_seg, N_EXP, SEG), I32), jax.ShapeDtypeStruct((n_seg, N_EXP, 128), I32)],
        scratch_shapes=[pltpu.VMEM((N_EXP, SEG), I32)],
        compiler_params=_cparams(("arbitrary",), VMEM_LIMIT),
        name="select",
    )(afft, tri, segt)


def _windows(cum_ref, s, e, tk):
    lo = cum_ref[s, e, tk]
    hi = cum_ref[s, e, tk + 1]
    w0 = lax.shift_left(lax.shift_right_logical(lo, 4), 4)
    nw = jnp.where(hi > lo, lax.shift_right_logical(hi - w0 + (WIN - 1), 7), 0)
    return w0, nw


def _onehot(base, slot_row):
    rows = base + lax.broadcasted_iota(I32, (WIN, 1), 0)
    return jnp.where(rows == slot_row, 1.0, 0.0).astype(BF16)


GATHER_EXPERTS = 8


def _gather_body(cum_ref, slot_ref, h2e_ref, g_ref):
    s, half, tk = pl.program_id(0), pl.program_id(1), pl.program_id(2)

    @pl.when(tk == 0)
    def _():
        g_ref[...] = jnp.zeros(g_ref.shape, BF16)

    wins = [_windows(cum_ref, s, half * GATHER_EXPERTS + j, tk) for j in range(GATHER_EXPERTS)]
    bases = [pl.multiple_of(w0, 16) for w0, _ in wins]
    sel = jnp.concatenate([_onehot(bases[j], slot_ref[0, j:j + 1, :]) for j in range(GATHER_EXPERTS)], axis=0)
    got = _dot(sel, h2e_ref[...]).astype(BF16)
    for j in range(GATHER_EXPERTS):
        g_ref[0, j, pl.ds(bases[j], WIN), :] = (g_ref[0, j, pl.ds(bases[j], WIN), :]
                                                  + got[j * WIN:(j + 1) * WIN, :])
    for j in range(GATHER_EXPERTS):
        def wbody(w, carry, j=j):
            base = pl.multiple_of(wins[j][0] + w * WIN, 16)
            more = _dot(_onehot(base, slot_ref[0, j:j + 1, :]), h2e_ref[...])
            g_ref[0, j, pl.ds(base, WIN), :] = g_ref[0, j, pl.ds(base, WIN), :] + more.astype(BF16)
            return carry

        lax.fori_loop(1, wins[j][1], wbody, 0)


def _gather_call(cum, slot, h2e):
    n_half = N_EXP // GATHER_EXPERTS
    return pl.pallas_call(
        _gather_body,
        grid_spec=pltpu.PrefetchScalarGridSpec(
            num_scalar_prefetch=1, grid=(NSEG, n_half, SEG // TK),
            in_specs=[pl.BlockSpec((1, GATHER_EXPERTS, TK), lambda s, h, t, c: (s, h, t)),
                      pl.BlockSpec((TK, HEXT), lambda s, h, t, c: (s * (SEG // TK) + t, 0))],
            out_specs=pl.BlockSpec((1, GATHER_EXPERTS, SLOT_PAD, HEXT), lambda s, h, t, c: (s, h, 0, 0))),
        out_shape=jax.ShapeDtypeStruct((NSEG, N_EXP, SLOT_PAD, HEXT), BF16),
        compiler_params=_cparams(("arbitrary", "arbitrary", "arbitrary"), VMEM_LIMIT),
        name="moe_gather",
    )(cum, slot, h2e)


def _ffn_body(g_ref, wg_ref, wu_ref, wd_ref, y_ref, wgb_s, wub_s, wdb_s):
    e = pl.program_id(0)
    wgb_s[...] = wg_ref[0].astype(BF16)
    wub_s[...] = wu_ref[0].astype(BF16)
    wdb_s[...] = wd_ref[0].astype(BF16)
    lane = lax.broadcasted_iota(I32, (1, 128), 1)
    pick = (lane == e) | (lane == e + N_EXP)
    for s in range(NSEG):
        xs = g_ref[s, 0, :, 0:D]
        ext = g_ref[s, 0, :, D:HEXT].astype(F32)
        gate = jnp.sum(jnp.where(pick, ext, 0.0), axis=1, keepdims=True)
        a = _dot(xs, wgb_s[...])
        u = _dot(xs, wub_s[...])
        y = _dot((_silu(a) * u).astype(BF16), wdb_s[...]) * gate
        y_ref[s, 0, 0:CAP_SEG, :] = y.astype(BF16)
        y_ref[s, 0, CAP_SEG:SLOT_PAD, :] = jnp.zeros((SLOT_PAD - CAP_SEG, D), BF16)


def _ffn_call(gath, wg, wu, wd):
    wspec = pl.BlockSpec((1, D, D), lambda e: (e, 0, 0))
    return pl.pallas_call(
        _ffn_body,
        grid=(N_EXP,),
        in_specs=[pl.BlockSpec((NSEG, 1, CAP_SEG, HEXT), lambda e: (0, e, 0, 0)), wspec, wspec, wspec],
        out_specs=pl.BlockSpec((NSEG, 1, SLOT_PAD, D), lambda e: (0, e, 0, 0)),
        out_shape=jax.ShapeDtypeStruct((NSEG, N_EXP, SLOT_PAD, D), BF16),
        scratch_shapes=[pltpu.VMEM((D, D), BF16)] * 3,
        compiler_params=_cparams(("arbitrary",), VMEM_LIMIT),
        name="moe_ffn",
    )(gath, wg, wu, wd)


def _scatter_body(cum_ref, slot_ref, y_ref, o_ref):
    s, tk = pl.program_id(0), pl.program_id(1)
    wins = [_windows(cum_ref, s, e, tk) for e in range(N_EXP)]
    acc = None
    for e in range(N_EXP):
        base = pl.multiple_of(wins[e][0], 16)
        term = _dot_tn(_onehot(base, slot_ref[0, e:e + 1, :]), y_ref[0, e, pl.ds(base, WIN), :])
        acc = term if acc is None else acc + term
    o_ref[...] = acc
    for e in range(N_EXP):
        def wbody(w, carry, e=e):
            base = pl.multiple_of(wins[e][0] + w * WIN, 16)
            o_ref[...] = o_ref[...] + _dot_tn(_onehot(base, slot_ref[0, e:e + 1, :]),
                                              y_ref[0, e, pl.ds(base, WIN), :])
            return carry

        lax.fori_loop(1, wins[e][1], wbody, 0)


def _scatter_call(cum, slot, y):
    return pl.pallas_call(
        _scatter_body,
        grid_spec=pltpu.PrefetchScalarGridSpec(
            num_scalar_prefetch=1, grid=(NSEG, SEG // TK),
            in_specs=[pl.BlockSpec((1, N_EXP, TK), lambda s, t, c: (s, 0, t)),
                      pl.BlockSpec((1, N_EXP, SLOT_PAD, D), lambda s, t, c: (s, 0, 0, 0),
                                   pipeline_mode=pl.Buffered(1))],
            out_specs=pl.BlockSpec((TK, D), lambda s, t, c: (s * (SEG // TK) + t, 0))),
        out_shape=jax.ShapeDtypeStruct((NTOK, D), F32),
        compiler_params=_cparams(("arbitrary", "arbitrary"), VMEM_LIMIT),
        name="moe_scatter",
    )(cum, slot, y)


def _final_body(n_ctx_steps, x1_ref, moe_ref, g2_ref, oc_ref, ol_ref):
    y = x1_ref[...] + g2_ref[0] * moe_ref[...]
    is_ctx = pl.program_id(0) < n_ctx_steps

    @pl.when(is_ctx)
    def _():
        oc_ref[...] = y

    @pl.when(jnp.logical_not(is_ctx))
    def _():
        ol_ref[...] = y


def _final_call(x1, moe, g2, n_ctx_rows):
    tm = 512
    per_seg = SEG // tm
    n_ctx_steps = n_ctx_rows // tm
    tok = pl.BlockSpec((tm, D), lambda i: (i, 0))
    return pl.pallas_call(
        functools.partial(_final_body, n_ctx_steps),
        grid=(NTOK // tm,),
        in_specs=[tok, tok, pl.BlockSpec((1, 1, D), lambda i: (i // per_seg, 0, 0))],
        out_specs=[pl.BlockSpec((tm, D), lambda i: (jnp.minimum(i, n_ctx_steps - 1), 0)),
                   pl.BlockSpec((tm, D), lambda i: (jnp.maximum(i - n_ctx_steps, 0), 0))],
        out_shape=[jax.ShapeDtypeStruct((n_ctx_rows, D), F32), jax.ShapeDtypeStruct((NTOK - n_ctx_rows, D), F32)],
        compiler_params=_cparams(("arbitrary",), VMEM_LIMIT),
        name="final_residual",
    )(x1, moe, g2)


def _block_diag4(w):
    eye = jnp.eye(4, dtype=w.dtype)
    return jnp.einsum('hij,hg->higj', w, eye).reshape(256, 256)


def _np_block_ones(n, blk):
    i = np.arange(n) // blk
    return (i[:, None] == i[None, :]).astype(np.float32)


def _head_cols():
    j = np.arange(128)
    src = np.full(128, -1)
    src[:64] = j[:64]
    src[64:80] = 64 + 2 * (j[64:80] - 64)
    src[80:96] = 64 + 2 * (j[80:96] - 80) + 1
    return src


def _mla_weights(l, mla_q_norm, mla_w_uq, mla_kv_norm, mla_w_ukv, mla_qn, mla_kn):
    src = _head_cols()
    valid = src >= 0
    srcc = np.where(valid, src, 0)
    colq = np.concatenate([h * 96 + srcc for h in range(4)])
    maskq = jnp.asarray(np.tile(valid, 4).astype(np.float32))
    wuq = (mla_w_uq[l][:, colq] * maskq).astype(BF16)
    qnw = (jnp.tile(mla_qn[l][srcc], 4) * maskq).reshape(1, 512)
    knw = (jnp.tile(mla_kn[l][srcc], 4) * maskq).reshape(1, 512)
    jn = np.arange(128)
    nope_valid = jn < 64
    colk = np.concatenate([h * 128 + np.where(nope_valid, jn, 0) for h in range(4)])
    maskk = jnp.asarray(np.tile(nope_valid, 4).astype(np.float32))
    colv = np.concatenate([h * 128 + 64 + np.arange(64) for h in range(4)])
    wukv = jnp.concatenate([mla_w_ukv[l][:, colk] * maskk, mla_w_ukv[l][:, colv]], axis=1).astype(BF16)
    return (mla_q_norm[l].reshape(1, 256), wuq, qnw, mla_kv_norm[l].reshape(1, 128), wukv, knw,
            jnp.asarray(_np_block_ones(512, 128), BF16))


def _krope128(kr):
    z64 = jnp.zeros(kr.shape[:-1] + (64,), kr.dtype)
    z32 = jnp.zeros(kr.shape[:-1] + (32,), kr.dtype)
    return jnp.concatenate([z64, kr[..., 0::2], kr[..., 1::2], z32], axis=-1)


def _rope_tables(n_tokens, grid_w):
    rows = (np.arange(n_tokens) // grid_w).astype(np.float32)
    cols = (np.arange(n_tokens) % grid_w).astype(np.float32)
    n_freq = 8
    inv = jnp.asarray(10000.0, F32) ** (-jnp.arange(n_freq, dtype=F32) / n_freq)
    ang = jnp.concatenate([jnp.asarray(rows)[:, None] * inv, jnp.asarray(cols)[:, None] * inv], axis=-1)
    cos, sin = jnp.cos(ang), jnp.sin(ang)
    one = jnp.ones((n_tokens, 64), F32)
    zero = jnp.zeros((n_tokens, 64), F32)
    cs = jnp.concatenate([one, cos, cos, one[:, :32]], axis=1)
    sn = jnp.concatenate([zero, -sin, sin, zero[:, :32]], axis=1)
    return cs, sn


def _in_weight(w_in_l):
    a, b, c = w_in_l[:, 0:512], w_in_l[:, 512:768], w_in_l[:, 768:2048]
    cq, ckv, kr = w_in_l[:, 2048:2304], w_in_l[:, 2304:2432], w_in_l[:, 2432:2464]
    z96 = jnp.zeros((D, 96), w_in_l.dtype)
    return jnp.concatenate([a, b, c, cq, ckv, kr, z96, _krope128(kr)], axis=1).astype(BF16)


def kernel(x_prompt, x_sample, cache_mla_ckv, cache_mla_krope, state_rglru, state_hgrn, c, c_ctx, norm1_w, norm2_w, w_ada, b_ada, w_in, conv_w, conv_b, lru_wa, lru_ba, lru_wx, lru_bx, lru_lambda, pool_w, pool_scale, hgrn_lower_bounds, hgrn_norm_w, mla_q_norm, mla_w_uq, mla_kv_norm, mla_w_ukv, mla_qk_norm_q, mla_qk_norm_k, w_out, w_router, w_exp_gate, w_exp_up, w_exp_down):
    nbp, t_p = x_prompt.shape[0], x_prompt.shape[1]
    nbs, t_s = x_sample.shape[0], x_sample.shape[1]
    depth = w_in.shape[0]

    n_p = nbp * t_p
    cond8 = jnp.concatenate([c_ctx[None], c, jnp.zeros((5, D), F32)], axis=0)
    mod = _ada_call(cond8, w_ada, b_ada)

    lb_soft = jax.nn.softmax(hgrn_lower_bounds.astype(F32), axis=1)
    lower = jnp.cumsum(lb_soft, axis=1) - lb_soft[:, :1]

    ones64 = jnp.asarray(_np_block_ones(256, 64), BF16)
    tri = jnp.asarray(np.triu(np.ones((256, 256), np.float32)), BF16)
    segt = jnp.asarray((np.arange(SEG)[:, None] < TK * np.arange(128)[None, :]).astype(np.float32), BF16)
    cs_s, sn_s = _rope_tables(t_s, 64)
    cs_p, sn_p = jnp.ones((t_p, 128), F32), jnp.zeros((t_p, 128), F32)
    zero_ctx = jnp.zeros((nbp, 256, 128), F32)

    ckvs, krs, lru_states, hgrn_states = [], [], [], []
    src = (x_prompt.reshape(n_p, D), x_sample.reshape(nbs * t_s, D))
    for l in range(depth):
        m6 = mod[l, 0:NSEG].reshape(NSEG, 6, 1, D)
        sh1, sc1, g1, sh2, sc2, g2 = (m6[:, k] for k in range(6))
        x, pa, pb, pc, pd = _in_call(src, sc1, sh1, norm1_w[l].reshape(1, D), _in_weight(w_in[l]))

        lru_w = (conv_w[l], conv_b[l].reshape(1, 256),
                 jnp.stack([_block_diag4(lru_wa[l, d]) for d in range(2)]).astype(BF16), lru_ba[l].reshape(2, 1, 256),
                 jnp.stack([_block_diag4(lru_wx[l, d]) for d in range(2)]).astype(BF16), lru_bx[l].reshape(2, 1, 256),
                 lru_lambda[l].reshape(2, 1, 256))
        oa_c, lru_fin = _lru_call(pa, t_p, nbp, 0, *lru_w, jnp.zeros((nbp, 2, 256), F32))
        oa_l, _ = _lru_call(pa, t_s, nbs, 1, *lru_w, state_rglru[:, l].astype(F32))
        pw_bd = _block_diag4(pool_w[l]).astype(BF16)
        ob_c = _pool_call(pb, t_p, nbp, 0, pw_bd, pool_scale[l].reshape(1, 256))
        ob_l = _pool_call(pb, t_s, nbs, 1, pw_bd, pool_scale[l].reshape(1, 256))
        eye4 = jnp.eye(4, dtype=F32)
        s0t = jnp.einsum('bzhdv,hg->bzhvgd', state_hgrn[:, l].astype(F32), eye4).reshape(nbs, 2, 256, 256)
        hg_w = (lower[:, l].reshape(2, 1, 256), jnp.tile(hgrn_norm_w[l], 4).reshape(1, 256))
        oc_c, st_p = _hgrn_call(pc, t_p, nbp, 0, *hg_w, jnp.zeros((nbp, 2, 256, 256), F32), ones64)
        oc_l, _ = _hgrn_call(pc, t_s, nbs, 1, *hg_w, s0t, ones64)
        mw = _mla_weights(l, mla_q_norm, mla_w_uq, mla_kv_norm, mla_w_ukv, mla_qk_norm_q, mla_qk_norm_k)
        od_c, ckvn = _mla_call(pd, t_p, nbp, 0, 0, False, zero_ctx, zero_ctx, cs_p, sn_p, mw)
        od_l, _ = _mla_call(pd, t_s, nbs, 1, 256, True, cache_mla_ckv[:, l], _krope128(cache_mla_krope[:, l]),
                            cs_s, sn_s, mw)

        wr_h, wr_l = _split_bf16(jnp.pad(w_router[l], ((0, 0), (0, 128 - N_EXP))))
        wrt_h, wrt_l = _split_bf16(w_router[l].T)
        x1, h2e, afft = _out_call((oa_c, oa_l, ob_c, ob_l, oc_c, oc_l, od_c, od_l), x, g1, sc2, sh2,
                                  norm2_w[l].reshape(1, D), w_out[l].astype(BF16), wr_h, wr_l, wrt_h, wrt_l)

        slot_p, cum_p = _sel_call(afft, 1, 0, nbp, 2 * t_p // N_EXP, tri, segt)
        slot_s, cum_s = _sel_call(afft, nbs, 1, 1, 2 * t_s // N_EXP, tri, segt)
        slot = jnp.concatenate([slot_p, slot_s], axis=0)
        cum = jnp.concatenate([cum_p, cum_s], axis=0)
        gath = _gather_call(cum, slot, h2e)
        y = _ffn_call(gath, w_exp_gate[l], w_exp_up[l], w_exp_down[l])
        moe = _scatter_call(cum, slot, y)
        src = (x1, moe, g2)

        ckvs.append(ckvn.reshape(nbp, t_p, 128))
        krs.append(pd[:n_p, 384:416].reshape(nbp, t_p, 32))
        lru_states.append(lru_fin)
        st6 = st_p.reshape(nbp, 2, 4, 64, 4, 64)
        hgrn_states.append(jnp.einsum('bzhvgd,hg->bzhdv', st6, eye4))

    y_c, y_l = _final_call(*src, n_p)
    return (y_c.reshape(nbp, t_p, D), y_l.reshape(nbs, t_s, D),
            jnp.stack(ckvs, axis=1), jnp.stack(krs, axis=1),
            jnp.stack(lru_states, axis=1), jnp.stack(hgrn_states, axis=1))
```

```python
import functools

import numpy as np
import jax
import jax.numpy as jnp
from jax import lax
from jax.experimental import pallas as pl
from jax.experimental.pallas import tpu as pltpu

F32 = jnp.float32
BF16 = jnp.bfloat16
I32 = jnp.int32

D = 1024
NTOK = 12288
SEG = 4096
NSEG = 3
EPS = 1e-6
TINY = 1e-30
LRU_C = 8.0
N_EXP = 16
CAP_SEG = 512
SLOT_PAD = 640
TILE = 256
TK = 256
WIN = 64
WIN_SHIFT = 6
HEXT = D + 128
VMEM_LIMIT = 56 * 1024 * 1024

PA_W, PB_W, PC_W, PD_W = 512, 256, 1280, 640
IN_PAD_W = PA_W + PB_W + PC_W + PD_W


def _cparams(sem, vmem=None):
    return pltpu.CompilerParams(dimension_semantics=sem, vmem_limit_bytes=vmem)


def _dot(a, b):
    return jnp.dot(a, b, preferred_element_type=F32)


def _dot_nt(a, b):
    return lax.dot_general(a, b, (((1,), (1,)), ((), ())), preferred_element_type=F32)


def _dot_tn(a, b):
    return lax.dot_general(a, b, (((0,), (0,)), ((), ())), preferred_element_type=F32)


def _rms(x, w):
    ms = jnp.mean(x * x, axis=-1, keepdims=True)
    return x * lax.rsqrt(ms + EPS) * w


def _silu(x):
    return x * jax.nn.sigmoid(x)


def _split_bf16(x):
    hi = x.astype(BF16)
    lo = (x - hi.astype(F32)).astype(BF16)
    return hi, lo


def _segsum2(x, ones_blk):
    hi, lo = _split_bf16(x)
    return _dot(hi, ones_blk) + _dot(lo, ones_blk)


def _cumsum_rows(x, r8, rev):
    n = x.shape[0]
    for s in (1, 2, 4):
        if not rev:
            x = jnp.where(r8 >= s, x + pltpu.roll(x, s, 0), x)
        else:
            x = jnp.where(r8 < 8 - s, x + pltpu.roll(x, n - s, 0), x)
    ng = n // 8
    outs = [None] * ng
    c = None
    for g in (range(ng) if not rev else reversed(range(ng))):
        xg = x[8 * g:8 * g + 8]
        if c is not None:
            xg = xg + c
        c = xg[7:8] if not rev else xg[0:1]
        outs[g] = xg
    return jnp.concatenate(outs, axis=0)


def _ada_body(c_ref, w_ref, b_ref, o_ref):
    s = _silu(c_ref[...])
    o_ref[0] = _dot(s.astype(BF16), w_ref[0].astype(BF16)) + b_ref[0]


def _ada_call(cond8, w_ada, b_ada):
    nj = 4
    wj = 6 * D // nj
    return pl.pallas_call(
        _ada_body,
        grid=(2, nj),
        in_specs=[pl.BlockSpec((8, D), lambda l, j: (0, 0)),
                  pl.BlockSpec((1, D, wj), lambda l, j: (l, 0, j)),
                  pl.BlockSpec((1, 1, wj), lambda l, j: (l, 0, j))],
        out_specs=pl.BlockSpec((1, 8, wj), lambda l, j: (l, 0, j)),
        out_shape=jax.ShapeDtypeStruct((2, 8, 6 * D), F32),
        compiler_params=_cparams(("arbitrary", "arbitrary"), VMEM_LIMIT),
        name="ada",
    )(cond8, w_ada, b_ada.reshape(2, 1, 6 * D))


def _in_body(n_ctx_steps, *refs):
    if n_ctx_steps is None:
        x1_ref, moe_ref, g2_ref, sc_ref, sh_ref, nw_ref, w_ref, x_ref, pa_ref, pb_ref, pc_ref, pd_ref = refs
        x = x1_ref[...] + g2_ref[0] * moe_ref[...]
    else:
        xc_ref, xl_ref, sc_ref, sh_ref, nw_ref, w_ref, x_ref, pa_ref, pb_ref, pc_ref, pd_ref = refs
        x = jnp.where(pl.program_id(0) < n_ctx_steps, xc_ref[...], xl_ref[...])
    x_ref[...] = x
    h = _rms(x, nw_ref[...]) * (1.0 + sc_ref[0]) + sh_ref[0]
    hb = h.astype(BF16)
    o = 0
    for ref, w in ((pa_ref, PA_W), (pb_ref, PB_W), (pc_ref, PC_W), (pd_ref, PD_W)):
        ref[...] = _dot(hb, w_ref[:, o:o + w])
        o += w


def _in_call(src, sc1, sh1, norm1, w_in_b):
    tm = 512
    n = NTOK // tm
    per_seg = SEG // tm
    row = lambda i: (i, 0)
    modspec = pl.BlockSpec((1, 1, D), lambda i: (i // per_seg, 0, 0))
    tokspec = pl.BlockSpec((tm, D), row)
    outs = [jax.ShapeDtypeStruct((NTOK, w), F32) for w in (D, PA_W, PB_W, PC_W, PD_W)]
    out_specs = [pl.BlockSpec((tm, w), row) for w in (D, PA_W, PB_W, PC_W, PD_W)]
    common_specs = [modspec, modspec, pl.BlockSpec((1, D), lambda i: (0, 0)),
                    pl.BlockSpec((D, IN_PAD_W), lambda i: (0, 0))]
    if len(src) == 2:
        n_ctx_steps = src[0].shape[0] // tm
        args = (*src, sc1, sh1, norm1, w_in_b)
        in_specs = [pl.BlockSpec((tm, D), lambda i: (jnp.minimum(i, n_ctx_steps - 1), 0)),
                    pl.BlockSpec((tm, D), lambda i: (jnp.maximum(i - n_ctx_steps, 0), 0))] + common_specs
    else:
        n_ctx_steps = None
        args = (*src, sc1, sh1, norm1, w_in_b)
        in_specs = [tokspec, tokspec, modspec] + common_specs
    return pl.pallas_call(
        functools.partial(_in_body, n_ctx_steps),
        grid=(n,), in_specs=in_specs, out_specs=out_specs, out_shape=outs,
        compiler_params=_cparams(("arbitrary",), VMEM_LIMIT),
        name="in_proj",
    )(*args)


def _halo_tile(ref, c0, c1, t0, t_len, static_single):
    xa = ref[pl.ds(t0, TILE), c0:c1]
    if static_single:
        z = jnp.zeros((8, c1 - c0), F32)
        return xa, jnp.concatenate([z, xa, z], axis=0)
    ps = pl.multiple_of(jnp.maximum(t0 - 8, 0), 8)
    ns = pl.multiple_of(jnp.minimum(t0 + TILE, t_len - 8), 8)
    prev = jnp.where(t0 > 0, ref[pl.ds(ps, 8), c0:c1], 0.0)
    nxt = jnp.where(t0 + TILE < t_len, ref[pl.ds(ns, 8), c0:c1], 0.0)
    return xa, jnp.concatenate([prev, xa, nxt], axis=0)


def _seq_spec(t_len, width, blk_off):
    idx = lambda b: (b + blk_off, 0)
    if t_len > TILE:
        return pl.BlockSpec((t_len, width), idx, pipeline_mode=pl.Buffered(1))
    return pl.BlockSpec((t_len, width), idx)


def _gelu_tanh(x):
    return 0.5 * x * (1.0 + jnp.tanh(0.7978845608028654 * (x + 0.044715 * (x * x * x))))


def _softplus(x):
    return jnp.maximum(x, 0.0) + jnp.log1p(jnp.exp(-jnp.abs(x)))


def _lru_scan(a, u, c, r8, rev):
    n = a.shape[0]
    for s in (1, 2, 4):
        if not rev:
            m = r8 >= s
            a_sh, u_sh = pltpu.roll(a, s, 0), pltpu.roll(u, s, 0)
        else:
            m = r8 < 8 - s
            a_sh, u_sh = pltpu.roll(a, n - s, 0), pltpu.roll(u, n - s, 0)
        u = jnp.where(m, a * u_sh + u, u)
        a = jnp.where(m, a * a_sh, a)
    ng = n // 8
    outs = [None] * ng
    for g in (range(ng) if not rev else reversed(range(ng))):
        hg = u[8 * g:8 * g + 8] + a[8 * g:8 * g + 8] * c
        c = hg[7:8] if not rev else hg[0:1]
        outs[g] = hg
    return jnp.concatenate(outs, axis=0), c


def _lru_body(t_len, pa_ref, cw_ref, cb_ref, wa_ref, ba_ref, wx_ref, bx_ref, lam_ref, h0_ref,
              oa_ref, hfin_ref, hf_s, ab_s, ub_s):
    nt = t_len // TILE
    single = nt == 1
    r8 = lax.broadcasted_iota(I32, (TILE, 1), 0) & 7
    n_ext = TILE + 16

    def gates(xc, xb, d):
        r = jax.nn.sigmoid(_dot(xb, wa_ref[d]) + ba_ref[d])
        i = jax.nn.sigmoid(_dot(xb, wx_ref[d]) + bx_ref[d])
        log_a = -LRU_C * r * _softplus(-lam_ref[d])
        a = jnp.exp(log_a)
        th = jnp.tanh(log_a)
        mult = jnp.sqrt(jnp.maximum(-2.0 * th / (1.0 - th), 0.0))
        return a, mult * (i * xc)

    def fwd_tile(i, c):
        t0 = pl.multiple_of(i * TILE, TILE)
        xa, ext = _halo_tile(pa_ref, 0, 256, t0, t_len, single)
        xc = cb_ref[...] + xa * cw_ref[1:2, :]
        for j in (0, 2, 3):
            xc = xc + pltpu.roll(ext, n_ext - 7 - j, 0)[0:TILE] * cw_ref[j:j + 1, :]
        xb = xc.astype(BF16)
        a_f, u_f = gates(xc, xb, 0)
        h, c = _lru_scan(a_f, u_f, c, r8, False)
        hf_s[pl.ds(t0, TILE), :] = h
        a_b, u_b = gates(xc, xb, 1)
        ab_s[pl.ds(t0, TILE), :] = a_b
        ub_s[pl.ds(t0, TILE), :] = u_b
        return c

    def bwd_tile(k, c):
        t0 = pl.multiple_of((nt - 1 - k) * TILE, TILE)
        h_b, c = _lru_scan(ab_s[pl.ds(t0, TILE), :], ub_s[pl.ds(t0, TILE), :], c, r8, True)
        gate = pa_ref[pl.ds(t0, TILE), 256:512]
        oa_ref[pl.ds(t0, TILE), :] = (hf_s[pl.ds(t0, TILE), :] + h_b) * _gelu_tanh(gate)
        return c

    h0 = h0_ref[0]
    if single:
        c_f = fwd_tile(0, h0[0:1])
        c_b = bwd_tile(0, h0[1:2])
    else:
        c_f = lax.fori_loop(0, nt, fwd_tile, h0[0:1])
        c_b = lax.fori_loop(0, nt, bwd_tile, h0[1:2])
    hfin_ref[0, 0:1, :] = c_f
    hfin_ref[0, 1:2, :] = c_b


def _lru_call(pa, t_len, nb, blk_off, conv_w, conv_b, wa_bd, ba, wx_bd, bx, lam, h0):
    full2 = lambda shp: pl.BlockSpec(shp, lambda b: (0,) * len(shp))
    return pl.pallas_call(
        functools.partial(_lru_body, t_len),
        grid=(nb,),
        in_specs=[_seq_spec(t_len, PA_W, blk_off),
                  full2((4, 256)), full2((1, 256)), full2((2, 256, 256)), full2((2, 1, 256)),
                  full2((2, 256, 256)), full2((2, 1, 256)), full2((2, 1, 256)),
                  pl.BlockSpec((1, 2, 256), lambda b: (b, 0, 0))],
        out_specs=[pl.BlockSpec((t_len, 256), lambda b: (b, 0)),
                   pl.BlockSpec((1, 2, 256), lambda b: (b, 0, 0))],
        out_shape=[jax.ShapeDtypeStruct((nb * t_len, 256), F32), jax.ShapeDtypeStruct((nb, 2, 256), F32)],
        scratch_shapes=[pltpu.VMEM((t_len, 256), F32)] * 3,
        compiler_params=_cparams(("arbitrary",), VMEM_LIMIT),
        name="lru",
    )(pa, conv_w, conv_b, wa_bd, ba, wx_bd, bx, lam, h0)


def _pool_body(t_len, pb_ref, pw_ref, ps_ref, ob_ref):
    nt = t_len // TILE
    single = nt == 1
    n_ext = TILE + 16
    lane = lax.broadcasted_iota(I32, (1, 256), 1)
    rowi = lax.broadcasted_iota(I32, (TILE, 1), 0)

    def ahead(x, k):
        return pltpu.roll(x, n_ext - k, 0)

    def tile(i, carry):
        t0 = pl.multiple_of(i * TILE, TILE)
        xa, ext = _halo_tile(pb_ref, 0, 256, t0, t_len, single)
        p2 = ext + ahead(ext, 1)
        p4 = p2 + ahead(p2, 2)
        p8 = p4 + ahead(p4, 4)
        p16 = p8 + ahead(p8, 8)
        sums = (ahead(p2, 7)[0:TILE], ahead(p4, 6)[0:TILE], ahead(p8, 4)[0:TILE], p16[0:TILE])
        tpos = t0 + rowi
        means = []
        for w, s in zip((2, 4, 8, 16), sums):
            cnt = jnp.minimum(tpos + w // 2, t_len) - jnp.maximum(tpos - w // 2, 0)
            means.append(s / cnt.astype(F32))
        mean = jnp.where(lane < 64, means[0], jnp.where(lane < 128, means[1],
                                                          jnp.where(lane < 192, means[2], means[3])))
        pooled = mean - xa
        ob_ref[pl.ds(t0, TILE), :] = _dot(pooled.astype(BF16), pw_ref[...]) * ps_ref[...]
        return carry

    if single:
        tile(0, 0)
    else:
        lax.fori_loop(0, nt, tile, 0)


def _pool_call(pb, t_len, nb, blk_off, pw_bd, pscale):
    return pl.pallas_call(
        functools.partial(_pool_body, t_len),
        grid=(nb,),
        in_specs=[_seq_spec(t_len, PB_W, blk_off),
                  pl.BlockSpec((256, 256), lambda b: (0, 0)),
                  pl.BlockSpec((1, 256), lambda b: (0, 0))],
        out_specs=pl.BlockSpec((t_len, 256), lambda b: (b, 0)),
        out_shape=jax.ShapeDtypeStruct((nb * t_len, 256), F32),
        compiler_params=_cparams(("arbitrary",), VMEM_LIMIT),
        name="pool",
    )(pb, pw_bd, pscale)


HGRN_LEVELS = (1, 2, 4, 8, 16, 32, 64, 128)


def _hgrn_dir(rev, q, k, v, lf, st_s, att_s, ones_ref, pm_ref, rowi, r8, lane_head):
    c_rows = TILE
    g = _cumsum_rows(lf, r8, rev)
    vb = v.astype(BF16)
    seg = g
    head_on = [jnp.where(lane_head == h, 1.0, 0.0).astype(BF16) for h in range(4)]
    for li, m in enumerate(HGRN_LEVELS):
        up = (rowi & (2 * m - 1)) >= m
        if not rev:
            ref_q = pltpu.roll(seg, m, 0)
            qsel, ksel = up, jnp.logical_not(up)
            seg_next = jnp.where(up, seg, pltpu.roll(seg, c_rows - m, 0))
        else:
            ref_q = pltpu.roll(seg, c_rows - m, 0)
            qsel, ksel = jnp.logical_not(up), up
            seg_next = jnp.where(up, pltpu.roll(seg, m, 0), seg)
        qp = jnp.where(qsel, q * jnp.exp(g - ref_q), 0.0).astype(BF16)
        kp = jnp.where(ksel, k * jnp.exp(seg - g), 0.0).astype(BF16)
        qs = jnp.concatenate([qp * head_on[h] for h in range(4)], axis=0)
        prod = _dot_nt(qs, kp)
        last = 2 * m == c_rows
        for h in range(4):
            blk = prod[h * c_rows:(h + 1) * c_rows, :]
            if not last:
                blk = blk * pm_ref[li]
            if li == 0:
                att_s[h * c_rows:(h + 1) * c_rows, :] = blk
            else:
                att_s[h * c_rows:(h + 1) * c_rows, :] = att_s[h * c_rows:(h + 1) * c_rows, :] + blk
        seg = seg_next
    o = _dot((q * k).astype(BF16), ones_ref[...]) * v
    for h in range(4):
        oh = _dot(att_s[h * c_rows:(h + 1) * c_rows, :].astype(BF16), vb)
        o = o + jnp.where(lane_head == h, oh, 0.0)
    st = st_s[...]
    o = o + _dot_nt((q * jnp.exp(g)).astype(BF16), st.astype(BF16))
    g_end = g[c_rows - 1:c_rows] if not rev else g[0:1]
    kd = k * jnp.exp(g_end - g)
    upd = _dot_tn(vb, kd.astype(BF16))
    blk = (lax.broadcasted_iota(I32, (256, 1), 0) >> 6) == (lax.broadcasted_iota(I32, (1, 256), 1) >> 6)
    st_s[...] = st * jnp.exp(g_end) + jnp.where(blk, upd, 0.0)
    return o


def _hgrn_body(t_len, pc_ref, lb_ref, nw_ref, s0_ref, ones_ref, pm_ref, oc_ref, sfin_ref, of_s, st_s, att_s):
    nt = t_len // TILE
    rowi = lax.broadcasted_iota(I32, (TILE, 1), 0)
    r8 = rowi & 7
    lane_head = lax.broadcasted_iota(I32, (1, 256), 1) >> 6

    def load(t0, d):
        q = _silu(pc_ref[pl.ds(t0, TILE), 0:256]) * 0.125
        f_raw = pc_ref[pl.ds(t0, TILE), 256 * (1 + d):256 * (2 + d)]
        v = pc_ref[pl.ds(t0, TILE), 768:1024]
        lb = lb_ref[d]
        f_val = lb + (1.0 - lb) * jax.nn.sigmoid(f_raw)
        lf = jnp.log(jnp.maximum(f_val, TINY))
        return q, 1.0 - f_val, v, lf

    def fwd_tile(i, carry):
        t0 = pl.multiple_of(i * TILE, TILE)
        q, k, v, lf = load(t0, 0)
        of_s[pl.ds(t0, TILE), :] = _hgrn_dir(False, q, k, v, lf, st_s, att_s, ones_ref, pm_ref, rowi, r8, lane_head)
        return carry

    def bwd_tile(kk, carry):
        t0 = pl.multiple_of((nt - 1 - kk) * TILE, TILE)
        q, k, v, lf = load(t0, 1)
        o = of_s[pl.ds(t0, TILE), :] + _hgrn_dir(True, q, k, v, lf, st_s, att_s, ones_ref, pm_ref, rowi, r8, lane_head)
        ms = _segsum2(o * o, ones_ref[...]) * (1.0 / 64.0)
        y = o * lax.rsqrt(ms + EPS) * nw_ref[...]
        oc_ref[pl.ds(t0, TILE), :] = y * _silu(pc_ref[pl.ds(t0, TILE), 1024:1280])
        return carry

    st_s[...] = s0_ref[0, 0]
    if nt == 1:
        fwd_tile(0, 0)
    else:
        lax.fori_loop(0, nt, fwd_tile, 0)
    sfin_ref[0, 0] = st_s[...]
    st_s[...] = s0_ref[0, 1]
    if nt == 1:
        bwd_tile(0, 0)
    else:
        lax.fori_loop(0, nt, bwd_tile, 0)
    sfin_ref[0, 1] = st_s[...]


def _pair_masks():
    i = np.arange(TILE)
    return np.stack([((i[:, None] // (2 * m)) == (i[None, :] // (2 * m))).astype(np.float32)
                     for m in HGRN_LEVELS[:-1]])


def _hgrn_call(pc, t_len, nb, blk_off, lower, normw, s0t, ones64):
    n_lv = len(HGRN_LEVELS) - 1
    return pl.pallas_call(
        functools.partial(_hgrn_body, t_len),
        grid=(nb,),
        in_specs=[_seq_spec(t_len, PC_W, blk_off),
                  pl.BlockSpec((2, 1, 256), lambda b: (0, 0, 0)),
                  pl.BlockSpec((1, 256), lambda b: (0, 0)),
                  pl.BlockSpec((1, 2, 256, 256), lambda b: (b, 0, 0, 0)),
                  pl.BlockSpec((256, 256), lambda b: (0, 0)),
                  pl.BlockSpec((n_lv, TILE, TILE), lambda b: (0, 0, 0))],
        out_specs=[pl.BlockSpec((t_len, 256), lambda b: (b, 0)),
                   pl.BlockSpec((1, 2, 256, 256), lambda b: (b, 0, 0, 0))],
        out_shape=[jax.ShapeDtypeStruct((nb * t_len, 256), F32), jax.ShapeDtypeStruct((nb, 2, 256, 256), F32)],
        scratch_shapes=[pltpu.VMEM((t_len, 256), F32), pltpu.VMEM((256, 256), F32),
                        pltpu.VMEM((4 * TILE, TILE), F32)],
        compiler_params=_cparams(("arbitrary",), VMEM_LIMIT),
        name="hgrn",
    )(pc, lower, normw, s0t, ones64, jnp.asarray(_pair_masks()))


ATT_SCALE = 96.0 ** -0.5
KEY_BLK = 512


def _rope512(x, cs128, sn128, lane128):
    cs = jnp.concatenate([cs128] * 4, axis=1)
    sn = jnp.concatenate([sn128] * 4, axis=1)
    partner = jnp.where(lane128 < 80, pltpu.roll(x, 512 - 16, 1), pltpu.roll(x, 16, 1))
    return x * cs + partner * sn


def _mla_body(t_len, n_ctx, use_rope, pd_ref, ckv_c_ref, kr_c_ref, cs_ref, sn_ref, qnorm_ref, wuq_ref, qnw_ref,
              kvnorm_ref, wukv_ref, knw_ref, ones_ref, od_ref, ckvn_ref, k_s, v_s, m_s, l_s, acc_s):
    nt = t_len // TILE
    t_k = n_ctx + t_len
    assert (t_k - TILE) % KEY_BLK == 0
    n_kb = (t_k - TILE) // KEY_BLK
    lane128 = lax.broadcasted_iota(I32, (1, 512), 1) & 127
    lane_head = lax.broadcasted_iota(I32, (1, 256), 1) >> 6

    def head_norm(x, w_ref):
        ss = _segsum2(x * x, ones_ref[...])
        return x * lax.rsqrt(ss * (1.0 / 96.0) + EPS) * w_ref[...]

    def put_kv(r0, ckv_n, kr128, rope_rows):
        kv = _dot(ckv_n.astype(BF16), wukv_ref[...])
        k_all = kv[:, 0:512] + jnp.concatenate([kr128] * 4, axis=1)
        kn = head_norm(k_all, knw_ref)
        if rope_rows is not None:
            kn = _rope512(kn, cs_ref[pl.ds(rope_rows, TILE), :], sn_ref[pl.ds(rope_rows, TILE), :], lane128)
        k_s[pl.ds(r0, TILE), :] = kn.astype(BF16)
        v_s[pl.ds(r0, TILE), :] = kv[:, 512:768].astype(BF16)

    if n_ctx:
        put_kv(0, ckv_c_ref[0], kr_c_ref[0], None)

    def kv_tile(i, carry):
        t0 = pl.multiple_of(i * TILE, TILE)
        ckv_n = _rms(pd_ref[pl.ds(t0, TILE), 256:384], kvnorm_ref[...])
        ckvn_ref[pl.ds(t0, TILE), :] = ckv_n
        put_kv(pl.multiple_of(n_ctx + t0, TILE), ckv_n, pd_ref[pl.ds(t0, TILE), 512:640],
               t0 if use_rope else None)
        return carry

    if nt == 1:
        kv_tile(0, 0)
    else:
        lax.fori_loop(0, nt, kv_tile, 0)

    def q_tile(i, carry):
        t0 = pl.multiple_of(i * TILE, TILE)
        qn = _rms(pd_ref[pl.ds(t0, TILE), 0:256], qnorm_ref[...])
        q = head_norm(_dot(qn.astype(BF16), wuq_ref[...]), qnw_ref)
        if use_rope:
            q = _rope512(q, cs_ref[pl.ds(t0, TILE), :], sn_ref[pl.ds(t0, TILE), :], lane128)
        qb = (q * ATT_SCALE).astype(BF16)
        qhs = [qb[:, 128 * h:128 * (h + 1)] for h in range(4)]

        def first_block(h):
            s = _dot_nt(qhs[h], k_s[0:TILE, 128 * h:128 * (h + 1)])
            m = jnp.max(s, axis=1, keepdims=True)
            p = jnp.exp(s - m)
            return m, jnp.sum(p, axis=1, keepdims=True), _dot(p.astype(BF16), v_s[0:TILE, :])

        if n_kb == 0:
            outs = []
            for h in range(4):
                _, l, acc = first_block(h)
                outs.append(acc / l)
        else:
            for h in range(4):
                m, l, acc = first_block(h)
                m_s[h] = jnp.broadcast_to(m, (TILE, 128))
                l_s[h] = jnp.broadcast_to(l, (TILE, 128))
                acc_s[h] = acc

            def kblock(j, c2):
                r0 = pl.multiple_of(TILE + j * KEY_BLK, TILE)
                for h in range(4):
                    s = _dot_nt(qhs[h], k_s[pl.ds(r0, KEY_BLK), 128 * h:128 * (h + 1)])
                    m_prev = m_s[h]
                    m_new = jnp.maximum(m_prev, jnp.max(s, axis=1, keepdims=True))
                    alpha = jnp.exp(m_prev - m_new)
                    p = jnp.exp(s - jnp.concatenate([m_new] * (KEY_BLK // 128), axis=1))
                    l_s[h] = alpha * l_s[h] + jnp.sum(p, axis=1, keepdims=True)
                    acc_s[h] = (jnp.concatenate([alpha, alpha], axis=1) * acc_s[h]
                                + _dot(p.astype(BF16), v_s[pl.ds(r0, KEY_BLK), :]))
                    m_s[h] = m_new
                return c2

            lax.fori_loop(0, n_kb, kblock, 0)
            outs = []
            for h in range(4):
                l = l_s[h]
                outs.append(acc_s[h] / jnp.concatenate([l, l], axis=1))
        o = outs[3]
        for h in range(3):
            o = jnp.where(lane_head == h, outs[h], o)
        od_ref[pl.ds(t0, TILE), :] = o
        return carry

    if nt == 1:
        q_tile(0, 0)
    else:
        lax.fori_loop(0, nt, q_tile, 0)


def _mla_call(pd, t_len, nb, blk_off, n_ctx, use_rope, ckv_c, kr_c, cs, sn, wts):
    qnorm, wuq, qnw, kvnorm, wukv, knw, ones128 = wts
    t_k = n_ctx + t_len
    c2 = lambda shp: pl.BlockSpec(shp, lambda b: (0,) * len(shp))
    return pl.pallas_call(
        functools.partial(_mla_body, t_len, n_ctx, use_rope),
        grid=(nb,),
        in_specs=[_seq_spec(t_len, PD_W, blk_off),
                  pl.BlockSpec((1, 256, 128), lambda b: (b, 0, 0)),
                  pl.BlockSpec((1, 256, 128), lambda b: (b, 0, 0)),
                  c2((t_len, 128)), c2((t_len, 128)),
                  c2((1, 256)), c2((256, 512)), c2((1, 512)), c2((1, 128)), c2((128, 768)), c2((1, 512)),
                  c2((512, 512))],
        out_specs=[pl.BlockSpec((t_len, 256), lambda b: (b, 0)),
                   pl.BlockSpec((t_len, 128), lambda b: (b, 0))],
        out_shape=[jax.ShapeDtypeStruct((nb * t_len, 256), F32), jax.ShapeDtypeStruct((nb * t_len, 128), F32)],
        scratch_shapes=[pltpu.VMEM((t_k, 512), BF16), pltpu.VMEM((t_k, 256), BF16),
                        pltpu.VMEM((4, TILE, 128), F32), pltpu.VMEM((4, TILE, 128), F32),
                        pltpu.VMEM((4, TILE, 256), F32)],
        compiler_params=_cparams(("arbitrary",), VMEM_LIMIT),
        name="mla",
    )(pd, ckv_c, kr_c, cs, sn, qnorm, wuq, qnw, kvnorm, wukv, knw, ones128)


def _out_body(n_ctx_steps, *refs):
    mix_refs = refs[:8]
    (x_ref, g1_ref, sc_ref, sh_ref, nw_ref, wout_ref, wrh_ref, wrl_ref,
     x1_ref, h2e_ref, afft_ref) = refs[8:]
    is_ctx = pl.program_id(0) < n_ctx_steps
    m = None
    for k in range(4):
        ok = jnp.where(is_ctx, mix_refs[2 * k][...], mix_refs[2 * k + 1][...]).astype(BF16)
        mk = _dot(ok, wout_ref[256 * k:256 * (k + 1), :])
        m = mk if m is None else m + mk
    x1 = x_ref[...] + g1_ref[0] * m
    x1_ref[...] = x1
    h2 = _rms(x1, nw_ref[...]) * (1.0 + sc_ref[0]) + sh_ref[0]
    hh, hl = _split_bf16(h2)
    lg = _dot(hh, wrh_ref[...]) + _dot(hl, wrh_ref[...]) + _dot(hh, wrl_ref[...])
    lane = lax.broadcasted_iota(I32, (1, 128), 1)
    lg = jnp.where(lane < N_EXP, lg, -jnp.inf)
    ex = jnp.exp(lg - jnp.max(lg, axis=1, keepdims=True))
    aff = ex / jnp.sum(ex, axis=1, keepdims=True)
    a_hi = aff.astype(BF16).astype(F32)
    a_lo = aff - a_hi
    ext = a_hi + pltpu.roll(a_lo, N_EXP, 1)
    h2e_ref[:, 0:D] = hh
    h2e_ref[:, D:HEXT] = ext.astype(BF16)
    afft_ref[...] = jnp.transpose(aff)[0:N_EXP, :]


def _out_call(mix, x, g1, sc2, sh2, norm2, wout_b, wr_h, wr_l):
    tm = 512
    n = NTOK // tm
    per_seg = SEG // tm
    n_ctx_steps = mix[0].shape[0] // tm
    row = lambda i: (i, 0)
    c2 = lambda shp: pl.BlockSpec(shp, lambda i: (0,) * len(shp))
    modspec = pl.BlockSpec((1, 1, D), lambda i: (i // per_seg, 0, 0))
    ctx_spec = pl.BlockSpec((tm, 256), lambda i: (jnp.minimum(i, n_ctx_steps - 1), 0))
    lat_spec = pl.BlockSpec((tm, 256), lambda i: (jnp.maximum(i - n_ctx_steps, 0), 0))
    return pl.pallas_call(
        functools.partial(_out_body, n_ctx_steps),
        grid=(n,),
        in_specs=[ctx_spec, lat_spec] * 4 + [pl.BlockSpec((tm, D), row), modspec, modspec, modspec,
                  c2((1, D)), c2((D, D)), c2((D, 128)), c2((D, 128))],
        out_specs=[pl.BlockSpec((tm, D), row), pl.BlockSpec((tm, HEXT), row),
                   pl.BlockSpec((N_EXP, tm), lambda i: (0, i))],
        out_shape=[jax.ShapeDtypeStruct((NTOK, D), F32), jax.ShapeDtypeStruct((NTOK, HEXT), BF16),
                   jax.ShapeDtypeStruct((N_EXP, NTOK), F32)],
        compiler_params=_cparams(("arbitrary",), VMEM_LIMIT),
        name="out_proj",
    )(*mix, x, g1, sc2, sh2, norm2, wout_b, wr_h, wr_l)


def _sel_body(n_grp, cap, aff_ref, tri_ref, segt_ref, slot_ref, cum_ref, pref_s):
    w = SEG // n_grp
    nblk = SEG // 256
    aff = aff_ref[...]
    pref_s[...] = jnp.zeros((N_EXP, SEG), I32)

    def grp_cols(fn):
        return jnp.concatenate([jnp.broadcast_to(fn(g), (N_EXP, w)) for g in range(n_grp)], axis=1)

    def it(i, carry):
        bit = lax.shift_left(jnp.int32(1), 30 - i)
        cand = pref_s[...] | bit
        ge = jnp.where(aff >= pltpu.bitcast(cand, F32), 1.0, 0.0)
        ok = grp_cols(lambda g: jnp.where(
            jnp.sum(ge[:, g * w:(g + 1) * w], axis=1, keepdims=True) >= cap, 1.0, 0.0))
        pref_s[...] = jnp.where(ok > 0.5, cand, pref_s[...])
        return carry

    lax.fori_loop(0, 31, it, 0)
    thr = pref_s[...]

    def grp_cumsum(x):
        outs, off = [], None
        for b in range(nblk):
            loc = _dot(x[:, 256 * b:256 * (b + 1)].astype(BF16), tri_ref[...])
            if (256 * b) % w == 0:
                off = None
            if off is not None:
                loc = loc + off
            off = loc[:, 255:256]
            outs.append(loc)
        return jnp.concatenate(outs, axis=1)

    gt = jnp.where(aff >= pltpu.bitcast(thr + 1, F32), 1.0, 0.0)
    eq = jnp.where(aff >= pltpu.bitcast(thr, F32), 1.0, 0.0) - gt
    room = grp_cols(lambda g: cap - jnp.sum(gt[:, g * w:(g + 1) * w], axis=1, keepdims=True))
    sel = jnp.where((gt > 0.5) | ((eq > 0.5) & (grp_cumsum(eq) <= room)), 1.0, 0.0)
    base = grp_cols(lambda g: jnp.full((N_EXP, 1), float(g * cap), F32))
    slot = base + grp_cumsum(sel) - 1.0
    slot_ref[0] = jnp.where(sel > 0.5, slot, -1.0).astype(I32)
    cum_ref[0] = _dot(sel.astype(BF16), segt_ref[...]).astype(I32)


def _sel_call(afft, n_seg, seg_off, n_grp, cap, tri, segt):
    c2 = lambda shp: pl.BlockSpec(shp, lambda s: (0,) * len(shp))
    return pl.pallas_call(
        functools.partial(_sel_body, n_grp, cap),
        grid=(n_seg,),
        in_specs=[pl.BlockSpec((N_EXP, SEG), lambda s: (0, s + seg_off)), c2((256, 256)), c2((SEG, 128))],
        out_specs=[pl.BlockSpec((1, N_EXP, SEG), lambda s: (s, 0, 0)),
                   pl.BlockSpec((1, N_EXP, 128), lambda s: (s, 0, 0))],
        out_shape=[jax.ShapeDtypeStruct((n_seg, N_EXP, SEG), I32), jax.ShapeDtypeStruct((n_seg, N_EXP, 128), I32)],
        scratch_shapes=[pltpu.VMEM((N_EXP, SEG), I32)],
        compiler_params=_cparams(("arbitrary",), VMEM_LIMIT),
        name="select",
    )(afft, tri, segt)


def _windows(cum_ref, s, e, tk):
    lo = cum_ref[s, e, tk]
    hi = cum_ref[s, e, tk + 1]
    w0 = lax.shift_left(lax.shift_right_logical(lo, 4), 4)
    nw = jnp.where(hi > lo, lax.shift_right_logical(hi - w0 + (WIN - 1), WIN_SHIFT), 0)
    return w0, nw


def _onehot(base, slot_row):
    rows = base + lax.broadcasted_iota(I32, (WIN, 1), 0)
    return jnp.where(rows == slot_row, 1.0, 0.0).astype(BF16)


GATHER_EXPERTS = 8


def _gather_body(cum_ref, slot_ref, h2e_ref, g_ref):
    s, half, tk = pl.program_id(0), pl.program_id(1), pl.program_id(2)

    @pl.when(tk == 0)
    def _():
        g_ref[...] = jnp.zeros(g_ref.shape, BF16)

    wins = [_windows(cum_ref, s, half * GATHER_EXPERTS + j, tk) for j in range(GATHER_EXPERTS)]
    bases = [pl.multiple_of(w0, 16) for w0, _ in wins]
    sel = jnp.concatenate([_onehot(bases[j], slot_ref[0, j:j + 1, :]) for j in range(GATHER_EXPERTS)], axis=0)
    got = _dot(sel, h2e_ref[...]).astype(BF16)
    for j in range(GATHER_EXPERTS):
        g_ref[0, j, pl.ds(bases[j], WIN), :] = (g_ref[0, j, pl.ds(bases[j], WIN), :]
                                                  + got[j * WIN:(j + 1) * WIN, :])
    for j in range(GATHER_EXPERTS):
        def wbody(w, carry, j=j):
            base = pl.multiple_of(wins[j][0] + w * WIN, 16)
            more = _dot(_onehot(base, slot_ref[0, j:j + 1, :]), h2e_ref[...])
            g_ref[0, j, pl.ds(base, WIN), :] = g_ref[0, j, pl.ds(base, WIN), :] + more.astype(BF16)
            return carry

        lax.fori_loop(1, wins[j][1], wbody, 0)


def _gather_call(cum, slot, h2e):
    n_half = N_EXP // GATHER_EXPERTS
    return pl.pallas_call(
        _gather_body,
        grid_spec=pltpu.PrefetchScalarGridSpec(
            num_scalar_prefetch=1, grid=(NSEG, n_half, SEG // TK),
            in_specs=[pl.BlockSpec((1, GATHER_EXPERTS, TK), lambda s, h, t, c: (s, h, t)),
                      pl.BlockSpec((TK, HEXT), lambda s, h, t, c: (s * (SEG // TK) + t, 0))],
            out_specs=pl.BlockSpec((1, GATHER_EXPERTS, SLOT_PAD, HEXT), lambda s, h, t, c: (s, h, 0, 0))),
        out_shape=jax.ShapeDtypeStruct((NSEG, N_EXP, SLOT_PAD, HEXT), BF16),
        compiler_params=_cparams(("arbitrary", "arbitrary", "arbitrary"), VMEM_LIMIT),
        name="moe_gather",
    )(cum, slot, h2e)


def _ffn_body(g_ref, wg_ref, wu_ref, wd_ref, y_ref, wgb_s, wub_s, wdb_s):
    e = pl.program_id(0)
    wgb_s[...] = wg_ref[0, 0].astype(BF16)
    wub_s[...] = wu_ref[0, 0].astype(BF16)
    wdb_s[...] = wd_ref[0, 0].astype(BF16)
    lane = lax.broadcasted_iota(I32, (1, 128), 1)
    pick = (lane == e) | (lane == e + N_EXP)
    for s in range(NSEG):
        xs = g_ref[s, 0, :, 0:D]
        ext = g_ref[s, 0, :, D:HEXT].astype(F32)
        gate = jnp.sum(jnp.where(pick, ext, 0.0), axis=1, keepdims=True)
        a = _dot(xs, wgb_s[...])
        u = _dot(xs, wub_s[...])
        y = _dot((_silu(a) * u).astype(BF16), wdb_s[...]) * gate
        y_ref[s, 0, 0:CAP_SEG, :] = y.astype(BF16)
        y_ref[s, 0, CAP_SEG:SLOT_PAD, :] = jnp.zeros((SLOT_PAD - CAP_SEG, D), BF16)


def _ffn_call(gath, layer, wg, wu, wd):
    wspec = pl.BlockSpec((1, 1, D, D), lambda e: (layer, e, 0, 0))
    return pl.pallas_call(
        _ffn_body,
        grid=(N_EXP,),
        in_specs=[pl.BlockSpec((NSEG, 1, CAP_SEG, HEXT), lambda e: (0, e, 0, 0)), wspec, wspec, wspec],
        out_specs=pl.BlockSpec((NSEG, 1, SLOT_PAD, D), lambda e: (0, e, 0, 0)),
        out_shape=jax.ShapeDtypeStruct((NSEG, N_EXP, SLOT_PAD, D), BF16),
        scratch_shapes=[pltpu.VMEM((D, D), BF16)] * 3,
        compiler_params=_cparams(("arbitrary",), VMEM_LIMIT),
        name="moe_ffn",
    )(gath, wg, wu, wd)


def _scatter_body(cum_ref, slot_ref, y_ref, o_ref):
    s, tk = pl.program_id(0), pl.program_id(1)
    wins = [_windows(cum_ref, s, e, tk) for e in range(N_EXP)]
    bases = [pl.multiple_of(w0, 16) for w0, _ in wins]
    sel = jnp.concatenate([_onehot(bases[e], slot_ref[0, e:e + 1, :]) for e in range(N_EXP)], axis=0)
    rows = jnp.concatenate([y_ref[0, e, pl.ds(bases[e], WIN), :] for e in range(N_EXP)], axis=0)
    o_ref[...] = _dot_tn(sel, rows)
    for e in range(N_EXP):
        def wbody(w, carry, e=e):
            base = pl.multiple_of(wins[e][0] + w * WIN, 16)
            o_ref[...] = o_ref[...] + _dot_tn(_onehot(base, slot_ref[0, e:e + 1, :]),
                                              y_ref[0, e, pl.ds(base, WIN), :])
            return carry

        lax.fori_loop(1, wins[e][1], wbody, 0)


def _scatter_call(cum, slot, y):
    return pl.pallas_call(
        _scatter_body,
        grid_spec=pltpu.PrefetchScalarGridSpec(
            num_scalar_prefetch=1, grid=(NSEG, SEG // TK),
            in_specs=[pl.BlockSpec((1, N_EXP, TK), lambda s, t, c: (s, 0, t)),
                      pl.BlockSpec((1, N_EXP, SLOT_PAD, D), lambda s, t, c: (s, 0, 0, 0),
                                   pipeline_mode=pl.Buffered(1))],
            out_specs=pl.BlockSpec((TK, D), lambda s, t, c: (s * (SEG // TK) + t, 0))),
        out_shape=jax.ShapeDtypeStruct((NTOK, D), F32),
        compiler_params=_cparams(("arbitrary", "arbitrary"), VMEM_LIMIT),
        name="moe_scatter",
    )(cum, slot, y)


def _final_body(n_ctx_steps, x1_ref, moe_ref, g2_ref, oc_ref, ol_ref):
    y = x1_ref[...] + g2_ref[0] * moe_ref[...]
    is_ctx = pl.program_id(0) < n_ctx_steps

    @pl.when(is_ctx)
    def _():
        oc_ref[...] = y

    @pl.when(jnp.logical_not(is_ctx))
    def _():
        ol_ref[...] = y


def _final_call(x1, moe, g2, n_ctx_rows):
    tm = 512
    per_seg = SEG // tm
    n_ctx_steps = n_ctx_rows // tm
    tok = pl.BlockSpec((tm, D), lambda i: (i, 0))
    return pl.pallas_call(
        functools.partial(_final_body, n_ctx_steps),
        grid=(NTOK // tm,),
        in_specs=[tok, tok, pl.BlockSpec((1, 1, D), lambda i: (i // per_seg, 0, 0))],
        out_specs=[pl.BlockSpec((tm, D), lambda i: (jnp.minimum(i, n_ctx_steps - 1), 0)),
                   pl.BlockSpec((tm, D), lambda i: (jnp.maximum(i - n_ctx_steps, 0), 0))],
        out_shape=[jax.ShapeDtypeStruct((n_ctx_rows, D), F32), jax.ShapeDtypeStruct((NTOK - n_ctx_rows, D), F32)],
        compiler_params=_cparams(("arbitrary",), VMEM_LIMIT),
        name="final_residual",
    )(x1, moe, g2)


def _block_diag4(w):
    eye = jnp.eye(4, dtype=w.dtype)
    return jnp.einsum('hij,hg->higj', w, eye).reshape(256, 256)


def _np_block_ones(n, blk):
    i = np.arange(n) // blk
    return (i[:, None] == i[None, :]).astype(np.float32)


def _head_cols():
    j = np.arange(128)
    src = np.full(128, -1)
    src[:64] = j[:64]
    src[64:80] = 64 + 2 * (j[64:80] - 64)
    src[80:96] = 64 + 2 * (j[80:96] - 80) + 1
    return src


def _mla_weights(l, mla_q_norm, mla_w_uq, mla_kv_norm, mla_w_ukv, mla_qn, mla_kn):
    src = _head_cols()
    valid = src >= 0
    srcc = np.where(valid, src, 0)
    colq = np.concatenate([h * 96 + srcc for h in range(4)])
    maskq = jnp.asarray(np.tile(valid, 4).astype(np.float32))
    wuq = (mla_w_uq[l][:, colq] * maskq).astype(BF16)
    qnw = (jnp.tile(mla_qn[l][srcc], 4) * maskq).reshape(1, 512)
    knw = (jnp.tile(mla_kn[l][srcc], 4) * maskq).reshape(1, 512)
    jn = np.arange(128)
    nope_valid = jn < 64
    colk = np.concatenate([h * 128 + np.where(nope_valid, jn, 0) for h in range(4)])
    maskk = jnp.asarray(np.tile(nope_valid, 4).astype(np.float32))
    colv = np.concatenate([h * 128 + 64 + np.arange(64) for h in range(4)])
    wukv = jnp.concatenate([mla_w_ukv[l][:, colk] * maskk, mla_w_ukv[l][:, colv]], axis=1).astype(BF16)
    return (mla_q_norm[l].reshape(1, 256), wuq, qnw, mla_kv_norm[l].reshape(1, 128), wukv, knw,
            jnp.asarray(_np_block_ones(512, 128), BF16))


def _krope128(kr):
    z64 = jnp.zeros(kr.shape[:-1] + (64,), kr.dtype)
    z32 = jnp.zeros(kr.shape[:-1] + (32,), kr.dtype)
    return jnp.concatenate([z64, kr[..., 0::2], kr[..., 1::2], z32], axis=-1)


def _rope_tables(n_tokens, grid_w):
    rows = (np.arange(n_tokens) // grid_w).astype(np.float32)
    cols = (np.arange(n_tokens) % grid_w).astype(np.float32)
    n_freq = 8
    inv = jnp.asarray(10000.0, F32) ** (-jnp.arange(n_freq, dtype=F32) / n_freq)
    ang = jnp.concatenate([jnp.asarray(rows)[:, None] * inv, jnp.asarray(cols)[:, None] * inv], axis=-1)
    cos, sin = jnp.cos(ang), jnp.sin(ang)
    one = jnp.ones((n_tokens, 64), F32)
    zero = jnp.zeros((n_tokens, 64), F32)
    cs = jnp.concatenate([one, cos, cos, one[:, :32]], axis=1)
    sn = jnp.concatenate([zero, -sin, sin, zero[:, :32]], axis=1)
    return cs, sn


def _in_weight(w_in_l):
    a, b, c = w_in_l[:, 0:512], w_in_l[:, 512:768], w_in_l[:, 768:2048]
    cq, ckv, kr = w_in_l[:, 2048:2304], w_in_l[:, 2304:2432], w_in_l[:, 2432:2464]
    z96 = jnp.zeros((D, 96), w_in_l.dtype)
    return jnp.concatenate([a, b, c, cq, ckv, kr, z96, _krope128(kr)], axis=1).astype(BF16)


def kernel(x_prompt, x_sample, cache_mla_ckv, cache_mla_krope, state_rglru, state_hgrn, c, c_ctx, norm1_w, norm2_w, w_ada, b_ada, w_in, conv_w, conv_b, lru_wa, lru_ba, lru_wx, lru_bx, lru_lambda, pool_w, pool_scale, hgrn_lower_bounds, hgrn_norm_w, mla_q_norm, mla_w_uq, mla_kv_norm, mla_w_ukv, mla_qk_norm_q, mla_qk_norm_k, w_out, w_router, w_exp_gate, w_exp_up, w_exp_down):
    nbp, t_p = x_prompt.shape[0], x_prompt.shape[1]
    nbs, t_s = x_sample.shape[0], x_sample.shape[1]
    depth = w_in.shape[0]

    n_p = nbp * t_p
    cond8 = jnp.concatenate([c_ctx[None], c, jnp.zeros((5, D), F32)], axis=0)
    mod = _ada_call(cond8, w_ada, b_ada)

    lb_soft = jax.nn.softmax(hgrn_lower_bounds.astype(F32), axis=1)
    lower = jnp.cumsum(lb_soft, axis=1) - lb_soft[:, :1]

    ones64 = jnp.asarray(_np_block_ones(256, 64), BF16)
    tri = jnp.asarray(np.triu(np.ones((256, 256), np.float32)), BF16)
    segt = jnp.asarray((np.arange(SEG)[:, None] < TK * np.arange(128)[None, :]).astype(np.float32), BF16)
    cs_s, sn_s = _rope_tables(t_s, 64)
    cs_p, sn_p = jnp.ones((t_p, 128), F32), jnp.zeros((t_p, 128), F32)
    zero_ctx = jnp.zeros((nbp, 256, 128), F32)

    ckvs, krs, lru_states, hgrn_states = [], [], [], []
    src = (x_prompt.reshape(n_p, D), x_sample.reshape(nbs * t_s, D))
    for l in range(depth):
        m6 = mod[l, 0:NSEG].reshape(NSEG, 6, 1, D)
        sh1, sc1, g1, sh2, sc2, g2 = (m6[:, k] for k in range(6))
        x, pa, pb, pc, pd = _in_call(src, sc1, sh1, norm1_w[l].reshape(1, D), _in_weight(w_in[l]))

        lru_w = (conv_w[l], conv_b[l].reshape(1, 256),
                 jnp.stack([_block_diag4(lru_wa[l, d]) for d in range(2)]).astype(BF16), lru_ba[l].reshape(2, 1, 256),
                 jnp.stack([_block_diag4(lru_wx[l, d]) for d in range(2)]).astype(BF16), lru_bx[l].reshape(2, 1, 256),
                 lru_lambda[l].reshape(2, 1, 256))
        oa_c, lru_fin = _lru_call(pa, t_p, nbp, 0, *lru_w, jnp.zeros((nbp, 2, 256), F32))
        oa_l, _ = _lru_call(pa, t_s, nbs, 1, *lru_w, state_rglru[:, l].astype(F32))
        pw_bd = _block_diag4(pool_w[l]).astype(BF16)
        ob_c = _pool_call(pb, t_p, nbp, 0, pw_bd, pool_scale[l].reshape(1, 256))
        ob_l = _pool_call(pb, t_s, nbs, 1, pw_bd, pool_scale[l].reshape(1, 256))
        eye4 = jnp.eye(4, dtype=F32)
        s0t = jnp.einsum('bzhdv,hg->bzhvgd', state_hgrn[:, l].astype(F32), eye4).reshape(nbs, 2, 256, 256)
        hg_w = (lower[:, l].reshape(2, 1, 256), jnp.tile(hgrn_norm_w[l], 4).reshape(1, 256))
        oc_c, st_p = _hgrn_call(pc, t_p, nbp, 0, *hg_w, jnp.zeros((nbp, 2, 256, 256), F32), ones64)
        oc_l, _ = _hgrn_call(pc, t_s, nbs, 1, *hg_w, s0t, ones64)
        mw = _mla_weights(l, mla_q_norm, mla_w_uq, mla_kv_norm, mla_w_ukv, mla_qk_norm_q, mla_qk_norm_k)
        od_c, ckvn = _mla_call(pd, t_p, nbp, 0, 0, False, zero_ctx, zero_ctx, cs_p, sn_p, mw)
        od_l, _ = _mla_call(pd, t_s, nbs, 1, 256, True, cache_mla_ckv[:, l], _krope128(cache_mla_krope[:, l]),
                            cs_s, sn_s, mw)

        wr_h, wr_l = _split_bf16(jnp.pad(w_router[l], ((0, 0), (0, 128 - N_EXP))))
        x1, h2e, afft = _out_call((oa_c, oa_l, ob_c, ob_l, oc_c, oc_l, od_c, od_l), x, g1, sc2, sh2,
                                  norm2_w[l].reshape(1, D), w_out[l].astype(BF16), wr_h, wr_l)

        slot_p, cum_p = _sel_call(afft, 1, 0, nbp, 2 * t_p // N_EXP, tri, segt)
        slot_s, cum_s = _sel_call(afft, nbs, 1, 1, 2 * t_s // N_EXP, tri, segt)
        slot = jnp.concatenate([slot_p, slot_s], axis=0)
        cum = jnp.concatenate([cum_p, cum_s], axis=0)
        gath = _gather_call(cum, slot, h2e)
        y = _ffn_call(gath, l, w_exp_gate, w_exp_up, w_exp_down)
        moe = _scatter_call(cum, slot, y)
        src = (x1, moe, g2)

        ckvs.append(ckvn.reshape(nbp, t_p, 128))
        krs.append(pd[:n_p, 384:416].reshape(nbp, t_p, 32))
        lru_states.append(lru_fin)
        st6 = st_p.reshape(nbp, 2, 4, 64, 4, 64)
        hgrn_states.append(jnp.einsum('bzhvgd,hg->bzhdv', st6, eye4))

    y_c, y_l = _final_call(*src, n_p)
    return (y_c.reshape(nbp, t_p, D), y_l.reshape(nbs, t_s, D),
            jnp.stack(ckvs, axis=1), jnp.stack(krs, axis=1),
            jnp.stack(lru_states, axis=1), jnp.stack(hgrn_states, axis=1))
```

```python
import functools

import numpy as np
import jax
import jax.numpy as jnp
from jax import lax
from jax.experimental import pallas as pl
from jax.experimental.pallas import tpu as pltpu

F32 = jnp.float32
BF16 = jnp.bfloat16
I32 = jnp.int32

D = 1024
NTOK = 12288
SEG = 4096
NSEG = 3
EPS = 1e-6
TINY = 1e-30
LRU_C = 8.0
N_EXP = 16
CAP_SEG = 512
SLOT_PAD = 576
TILE = 256
TK = 256
WIN = 64
WIN_SHIFT = 6
HEXT = D + 128
VMEM_LIMIT = 56 * 1024 * 1024

PA_W, PB_W, PC_W, PD_W = 512, 256, 1280, 640
IN_PAD_W = PA_W + PB_W + PC_W + PD_W


def _cparams(sem, vmem=None):
    return pltpu.CompilerParams(dimension_semantics=sem, vmem_limit_bytes=vmem)


def _dot(a, b):
    return jnp.dot(a, b, preferred_element_type=F32)


def _dot_nt(a, b):
    return lax.dot_general(a, b, (((1,), (1,)), ((), ())), preferred_element_type=F32)


def _dot_tn(a, b):
    return lax.dot_general(a, b, (((0,), (0,)), ((), ())), preferred_element_type=F32)


def _rms(x, w):
    ms = jnp.mean(x * x, axis=-1, keepdims=True)
    return x * lax.rsqrt(ms + EPS) * w


def _silu(x):
    return x * jax.nn.sigmoid(x)


def _split_bf16(x):
    hi = x.astype(BF16)
    lo = (x - hi.astype(F32)).astype(BF16)
    return hi, lo


def _segsum2(x, ones_blk):
    hi, lo = _split_bf16(x)
    return _dot(hi, ones_blk) + _dot(lo, ones_blk)


def _cumsum_rows(x, r8, rev):
    n = x.shape[0]
    for s in (1, 2, 4):
        if not rev:
            x = jnp.where(r8 >= s, x + pltpu.roll(x, s, 0), x)
        else:
            x = jnp.where(r8 < 8 - s, x + pltpu.roll(x, n - s, 0), x)
    ng = n // 8
    outs = [None] * ng
    c = None
    for g in (range(ng) if not rev else reversed(range(ng))):
        xg = x[8 * g:8 * g + 8]
        if c is not None:
            xg = xg + c
        c = xg[7:8] if not rev else xg[0:1]
        outs[g] = xg
    return jnp.concatenate(outs, axis=0)


def _ada_body(c_ref, w_ref, b_ref, o_ref):
    s = _silu(c_ref[...])
    o_ref[0] = _dot(s.astype(BF16), w_ref[0].astype(BF16)) + b_ref[0]


def _ada_call(cond8, w_ada, b_ada):
    nj = 4
    wj = 6 * D // nj
    return pl.pallas_call(
        _ada_body,
        grid=(2, nj),
        in_specs=[pl.BlockSpec((8, D), lambda l, j: (0, 0)),
                  pl.BlockSpec((1, D, wj), lambda l, j: (l, 0, j)),
                  pl.BlockSpec((1, 1, wj), lambda l, j: (l, 0, j))],
        out_specs=pl.BlockSpec((1, 8, wj), lambda l, j: (l, 0, j)),
        out_shape=jax.ShapeDtypeStruct((2, 8, 6 * D), F32),
        compiler_params=_cparams(("arbitrary", "arbitrary"), VMEM_LIMIT),
        name="ada",
    )(cond8, w_ada, b_ada.reshape(2, 1, 6 * D))


def _in_body(n_ctx_steps, *refs):
    if n_ctx_steps is None:
        x_ref, sc_ref, sh_ref, nw_ref, w_ref, pa_ref, pb_ref, pc_ref, pd_ref = refs
        x = x_ref[...]
    else:
        xc_ref, xl_ref, sc_ref, sh_ref, nw_ref, w_ref, x_ref, pa_ref, pb_ref, pc_ref, pd_ref = refs
        x = jnp.where(pl.program_id(0) < n_ctx_steps, xc_ref[...], xl_ref[...])
        x_ref[...] = x
    h = _rms(x, nw_ref[...]) * (1.0 + sc_ref[0]) + sh_ref[0]
    hb = h.astype(BF16)
    o = 0
    for ref, w in ((pa_ref, PA_W), (pb_ref, PB_W), (pc_ref, PC_W), (pd_ref, PD_W)):
        ref[...] = _dot(hb, w_ref[:, o:o + w])
        o += w


def _in_call(src, sc1, sh1, norm1, w_in_b):
    tm = 512
    n = NTOK // tm
    per_seg = SEG // tm
    row = lambda i: (i, 0)
    modspec = pl.BlockSpec((1, 1, D), lambda i: (i // per_seg, 0, 0))
    tokspec = pl.BlockSpec((tm, D), row)
    widths = (PA_W, PB_W, PC_W, PD_W)
    common_specs = [modspec, modspec, pl.BlockSpec((1, D), lambda i: (0, 0)),
                    pl.BlockSpec((D, IN_PAD_W), lambda i: (0, 0))]
    args = (*src, sc1, sh1, norm1, w_in_b)
    if len(src) == 2:
        n_ctx_steps = src[0].shape[0] // tm
        widths = (D,) + widths
        in_specs = [pl.BlockSpec((tm, D), lambda i: (jnp.minimum(i, n_ctx_steps - 1), 0)),
                    pl.BlockSpec((tm, D), lambda i: (jnp.maximum(i - n_ctx_steps, 0), 0))] + common_specs
    else:
        n_ctx_steps = None
        in_specs = [tokspec] + common_specs
    outs = [jax.ShapeDtypeStruct((NTOK, w), F32) for w in widths]
    out_specs = [pl.BlockSpec((tm, w), row) for w in widths]
    return pl.pallas_call(
        functools.partial(_in_body, n_ctx_steps),
        grid=(n,), in_specs=in_specs, out_specs=out_specs, out_shape=outs,
        compiler_params=_cparams(("arbitrary",), VMEM_LIMIT),
        name="in_proj",
    )(*args)


def _halo_tile(ref, c0, c1, t0, t_len, static_single):
    xa = ref[pl.ds(t0, TILE), c0:c1]
    if static_single:
        z = jnp.zeros((8, c1 - c0), F32)
        return xa, jnp.concatenate([z, xa, z], axis=0)
    ps = pl.multiple_of(jnp.maximum(t0 - 8, 0), 8)
    ns = pl.multiple_of(jnp.minimum(t0 + TILE, t_len - 8), 8)
    prev = jnp.where(t0 > 0, ref[pl.ds(ps, 8), c0:c1], 0.0)
    nxt = jnp.where(t0 + TILE < t_len, ref[pl.ds(ns, 8), c0:c1], 0.0)
    return xa, jnp.concatenate([prev, xa, nxt], axis=0)


def _seq_spec(t_len, width, blk_off):
    idx = lambda b: (b + blk_off, 0)
    if t_len > TILE:
        return pl.BlockSpec((t_len, width), idx, pipeline_mode=pl.Buffered(1))
    return pl.BlockSpec((t_len, width), idx)


def _gelu_tanh(x):
    return 0.5 * x * (1.0 + jnp.tanh(0.7978845608028654 * (x + 0.044715 * (x * x * x))))


def _softplus(x):
    return jnp.maximum(x, 0.0) + jnp.log1p(jnp.exp(-jnp.abs(x)))


def _lru_scan(a, u, c, r8, rev):
    n = a.shape[0]
    for s in (1, 2, 4):
        if not rev:
            m = r8 >= s
            a_sh, u_sh = pltpu.roll(a, s, 0), pltpu.roll(u, s, 0)
        else:
            m = r8 < 8 - s
            a_sh, u_sh = pltpu.roll(a, n - s, 0), pltpu.roll(u, n - s, 0)
        u = jnp.where(m, a * u_sh + u, u)
        a = jnp.where(m, a * a_sh, a)
    ng = n // 8
    outs = [None] * ng
    for g in (range(ng) if not rev else reversed(range(ng))):
        hg = u[8 * g:8 * g + 8] + a[8 * g:8 * g + 8] * c
        c = hg[7:8] if not rev else hg[0:1]
        outs[g] = hg
    return jnp.concatenate(outs, axis=0), c


def _lru_body(t_len, pa_ref, cw_ref, cb_ref, wa_ref, ba_ref, wx_ref, bx_ref, lam_ref, h0_ref,
              oa_ref, hfin_ref, hf_s, ab_s, ub_s):
    nt = t_len // TILE
    single = nt == 1
    r8 = lax.broadcasted_iota(I32, (TILE, 1), 0) & 7
    n_ext = TILE + 16

    def gates(xc, xb, d):
        r = jax.nn.sigmoid(_dot(xb, wa_ref[d]) + ba_ref[d])
        i = jax.nn.sigmoid(_dot(xb, wx_ref[d]) + bx_ref[d])
        log_a = -LRU_C * r * _softplus(-lam_ref[d])
        a = jnp.exp(log_a)
        th = jnp.tanh(log_a)
        mult = jnp.sqrt(jnp.maximum(-2.0 * th / (1.0 - th), 0.0))
        return a, mult * (i * xc)

    def fwd_tile(i, c):
        t0 = pl.multiple_of(i * TILE, TILE)
        xa, ext = _halo_tile(pa_ref, 0, 256, t0, t_len, single)
        xc = cb_ref[...] + xa * cw_ref[1:2, :]
        for j in (0, 2, 3):
            xc = xc + pltpu.roll(ext, n_ext - 7 - j, 0)[0:TILE] * cw_ref[j:j + 1, :]
        xb = xc.astype(BF16)
        a_f, u_f = gates(xc, xb, 0)
        h, c = _lru_scan(a_f, u_f, c, r8, False)
        hf_s[pl.ds(t0, TILE), :] = h
        a_b, u_b = gates(xc, xb, 1)
        ab_s[pl.ds(t0, TILE), :] = a_b
        ub_s[pl.ds(t0, TILE), :] = u_b
        return c

    def bwd_tile(k, c):
        t0 = pl.multiple_of((nt - 1 - k) * TILE, TILE)
        h_b, c = _lru_scan(ab_s[pl.ds(t0, TILE), :], ub_s[pl.ds(t0, TILE), :], c, r8, True)
        gate = pa_ref[pl.ds(t0, TILE), 256:512]
        oa_ref[pl.ds(t0, TILE), :] = (hf_s[pl.ds(t0, TILE), :] + h_b) * _gelu_tanh(gate)
        return c

    h0 = h0_ref[0]
    if single:
        c_f = fwd_tile(0, h0[0:1])
        c_b = bwd_tile(0, h0[1:2])
    else:
        c_f = lax.fori_loop(0, nt, fwd_tile, h0[0:1])
        c_b = lax.fori_loop(0, nt, bwd_tile, h0[1:2])
    hfin_ref[0, 0:1, :] = c_f
    hfin_ref[0, 1:2, :] = c_b


def _lru_call(pa, t_len, nb, blk_off, conv_w, conv_b, wa_bd, ba, wx_bd, bx, lam, h0):
    full2 = lambda shp: pl.BlockSpec(shp, lambda b: (0,) * len(shp))
    return pl.pallas_call(
        functools.partial(_lru_body, t_len),
        grid=(nb,),
        in_specs=[_seq_spec(t_len, PA_W, blk_off),
                  full2((4, 256)), full2((1, 256)), full2((2, 256, 256)), full2((2, 1, 256)),
                  full2((2, 256, 256)), full2((2, 1, 256)), full2((2, 1, 256)),
                  pl.BlockSpec((1, 2, 256), lambda b: (b, 0, 0))],
        out_specs=[pl.BlockSpec((t_len, 256), lambda b: (b, 0)),
                   pl.BlockSpec((1, 2, 256), lambda b: (b, 0, 0))],
        out_shape=[jax.ShapeDtypeStruct((nb * t_len, 256), F32), jax.ShapeDtypeStruct((nb, 2, 256), F32)],
        scratch_shapes=[pltpu.VMEM((t_len, 256), F32)] * 3,
        compiler_params=_cparams(("arbitrary",), VMEM_LIMIT),
        name="lru",
    )(pa, conv_w, conv_b, wa_bd, ba, wx_bd, bx, lam, h0)


def _pool_body(t_len, pb_ref, pw_ref, ps_ref, ob_ref):
    nt = t_len // TILE
    single = nt == 1
    n_ext = TILE + 16
    lane = lax.broadcasted_iota(I32, (1, 256), 1)
    rowi = lax.broadcasted_iota(I32, (TILE, 1), 0)

    def ahead(x, k):
        return pltpu.roll(x, n_ext - k, 0)

    def tile(i, carry):
        t0 = pl.multiple_of(i * TILE, TILE)
        xa, ext = _halo_tile(pb_ref, 0, 256, t0, t_len, single)
        p2 = ext + ahead(ext, 1)
        p4 = p2 + ahead(p2, 2)
        p8 = p4 + ahead(p4, 4)
        p16 = p8 + ahead(p8, 8)
        sums = (ahead(p2, 7)[0:TILE], ahead(p4, 6)[0:TILE], ahead(p8, 4)[0:TILE], p16[0:TILE])
        tpos = t0 + rowi
        means = []
        for w, s in zip((2, 4, 8, 16), sums):
            cnt = jnp.minimum(tpos + w // 2, t_len) - jnp.maximum(tpos - w // 2, 0)
            means.append(s / cnt.astype(F32))
        mean = jnp.where(lane < 64, means[0], jnp.where(lane < 128, means[1],
                                                          jnp.where(lane < 192, means[2], means[3])))
        pooled = mean - xa
        ob_ref[pl.ds(t0, TILE), :] = _dot(pooled.astype(BF16), pw_ref[...]) * ps_ref[...]
        return carry

    if single:
        tile(0, 0)
    else:
        lax.fori_loop(0, nt, tile, 0)


def _pool_call(pb, t_len, nb, blk_off, pw_bd, pscale):
    return pl.pallas_call(
        functools.partial(_pool_body, t_len),
        grid=(nb,),
        in_specs=[_seq_spec(t_len, PB_W, blk_off),
                  pl.BlockSpec((256, 256), lambda b: (0, 0)),
                  pl.BlockSpec((1, 256), lambda b: (0, 0))],
        out_specs=pl.BlockSpec((t_len, 256), lambda b: (b, 0)),
        out_shape=jax.ShapeDtypeStruct((nb * t_len, 256), F32),
        compiler_params=_cparams(("arbitrary",), VMEM_LIMIT),
        name="pool",
    )(pb, pw_bd, pscale)


HGRN_LEVELS = (1, 2, 4, 8, 16, 32, 64, 128)


def _hgrn_dir(rev, q, k, v, lf, st_s, att_s, ones_ref, pm_ref, rowi, r8, lane_head):
    c_rows = TILE
    half = TILE // 2
    g = _cumsum_rows(lf, r8, rev)
    vb = v.astype(BF16)
    seg = g
    head_on = [jnp.where(lane_head == h, 1.0, 0.0).astype(BF16) for h in range(4)]
    for li, m in enumerate(HGRN_LEVELS):
        up = (rowi & (2 * m - 1)) >= m
        if not rev:
            ref_q = pltpu.roll(seg, m, 0)
            qsel, ksel = up, jnp.logical_not(up)
            seg_next = jnp.where(up, seg, pltpu.roll(seg, c_rows - m, 0))
        else:
            ref_q = pltpu.roll(seg, c_rows - m, 0)
            qsel, ksel = jnp.logical_not(up), up
            seg_next = jnp.where(up, pltpu.roll(seg, m, 0), seg)
        qp = jnp.where(qsel, q * jnp.exp(g - ref_q), 0.0).astype(BF16)
        kp = jnp.where(ksel, k * jnp.exp(seg - g), 0.0).astype(BF16)

        def scores(qh, kh):
            qs = jnp.concatenate([qp[half * qh:half * (qh + 1)] * head_on[h] for h in range(4)], axis=0)
            return _dot_nt(qs, kp[half * kh:half * (kh + 1)])

        if 2 * m == c_rows:
            cross_q, cross_k = (1, 0) if not rev else (0, 1)
            cross = scores(cross_q, cross_k)
        else:
            for b in range(2):
                prod = scores(b, b)
                for h in range(4):
                    rows = slice(h * half, (h + 1) * half)
                    att_s[li, b, rows, :] = prod[rows] * pm_ref[li]
        seg = seg_next
    vm = [[vb[half * b:half * (b + 1)] * head_on[h] for h in range(4)] for b in range(2)]
    o_halves = []
    for b in range(2):
        ob = None
        for h in range(4):
            rows = slice(h * half, (h + 1) * half)
            att = att_s[0, b, rows, :]
            for li in range(1, len(HGRN_LEVELS) - 1):
                att = att + att_s[li, b, rows, :]
            term = _dot(att.astype(BF16), vm[b][h])
            if b == cross_q:
                term = term + _dot(cross[rows].astype(BF16), vm[cross_k][h])
            ob = term if ob is None else ob + term
        o_halves.append(ob)
    o = _dot((q * k).astype(BF16), ones_ref[...]) * v + jnp.concatenate(o_halves, axis=0)
    st = st_s[...]
    o = o + _dot_nt((q * jnp.exp(g)).astype(BF16), st.astype(BF16))
    g_end = g[c_rows - 1:c_rows] if not rev else g[0:1]
    kd = k * jnp.exp(g_end - g)
    upd = _dot_tn(vb, kd.astype(BF16))
    blk = (lax.broadcasted_iota(I32, (256, 1), 0) >> 6) == (lax.broadcasted_iota(I32, (1, 256), 1) >> 6)
    st_s[...] = st * jnp.exp(g_end) + jnp.where(blk, upd, 0.0)
    return o


def _hgrn_body(t_len, pc_ref, lb_ref, nw_ref, s0_ref, ones_ref, pm_ref, oc_ref, sfin_ref, of_s, st_s, att_s):
    nt = t_len // TILE
    rowi = lax.broadcasted_iota(I32, (TILE, 1), 0)
    r8 = rowi & 7
    lane_head = lax.broadcasted_iota(I32, (1, 256), 1) >> 6

    def load(t0, d):
        q = _silu(pc_ref[pl.ds(t0, TILE), 0:256]) * 0.125
        f_raw = pc_ref[pl.ds(t0, TILE), 256 * (1 + d):256 * (2 + d)]
        v = pc_ref[pl.ds(t0, TILE), 768:1024]
        lb = lb_ref[d]
        f_val = lb + (1.0 - lb) * jax.nn.sigmoid(f_raw)
        lf = jnp.log(jnp.maximum(f_val, TINY))
        return q, 1.0 - f_val, v, lf

    def fwd_tile(i, carry):
        t0 = pl.multiple_of(i * TILE, TILE)
        q, k, v, lf = load(t0, 0)
        of_s[pl.ds(t0, TILE), :] = _hgrn_dir(False, q, k, v, lf, st_s, att_s, ones_ref, pm_ref, rowi, r8, lane_head)
        return carry

    def bwd_tile(kk, carry):
        t0 = pl.multiple_of((nt - 1 - kk) * TILE, TILE)
        q, k, v, lf = load(t0, 1)
        o = of_s[pl.ds(t0, TILE), :] + _hgrn_dir(True, q, k, v, lf, st_s, att_s, ones_ref, pm_ref, rowi, r8, lane_head)
        ms = _segsum2(o * o, ones_ref[...]) * (1.0 / 64.0)
        y = o * lax.rsqrt(ms + EPS) * nw_ref[...]
        oc_ref[pl.ds(t0, TILE), :] = y * _silu(pc_ref[pl.ds(t0, TILE), 1024:1280])
        return carry

    st_s[...] = s0_ref[0, 0]
    if nt == 1:
        fwd_tile(0, 0)
    else:
        lax.fori_loop(0, nt, fwd_tile, 0)
    sfin_ref[0, 0] = st_s[...]
    st_s[...] = s0_ref[0, 1]
    if nt == 1:
        bwd_tile(0, 0)
    else:
        lax.fori_loop(0, nt, bwd_tile, 0)
    sfin_ref[0, 1] = st_s[...]


def _pair_masks():
    i = np.arange(TILE // 2)
    return np.stack([((i[:, None] // (2 * m)) == (i[None, :] // (2 * m))).astype(np.float32)
                     for m in HGRN_LEVELS[:-1]])


def _hgrn_call(pc, t_len, nb, blk_off, lower, normw, s0t, ones64):
    n_lv = len(HGRN_LEVELS) - 1
    return pl.pallas_call(
        functools.partial(_hgrn_body, t_len),
        grid=(nb,),
        in_specs=[_seq_spec(t_len, PC_W, blk_off),
                  pl.BlockSpec((2, 1, 256), lambda b: (0, 0, 0)),
                  pl.BlockSpec((1, 256), lambda b: (0, 0)),
                  pl.BlockSpec((1, 2, 256, 256), lambda b: (b, 0, 0, 0)),
                  pl.BlockSpec((256, 256), lambda b: (0, 0)),
                  pl.BlockSpec((n_lv, TILE // 2, TILE // 2), lambda b: (0, 0, 0))],
        out_specs=[pl.BlockSpec((t_len, 256), lambda b: (b, 0)),
                   pl.BlockSpec((1, 2, 256, 256), lambda b: (b, 0, 0, 0))],
        out_shape=[jax.ShapeDtypeStruct((nb * t_len, 256), F32), jax.ShapeDtypeStruct((nb, 2, 256, 256), F32)],
        scratch_shapes=[pltpu.VMEM((t_len, 256), F32), pltpu.VMEM((256, 256), F32),
                        pltpu.VMEM((len(HGRN_LEVELS) - 1, 2, 2 * TILE, TILE // 2), F32)],
        compiler_params=_cparams(("arbitrary",), VMEM_LIMIT),
        name="hgrn",
    )(pc, lower, normw, s0t, ones64, jnp.asarray(_pair_masks()))


ATT_SCALE = 96.0 ** -0.5
KEY_BLK = 512


def _rope512(x, cs128, sn128, lane128):
    cs = jnp.concatenate([cs128] * 4, axis=1)
    sn = jnp.concatenate([sn128] * 4, axis=1)
    partner = jnp.where(lane128 < 80, pltpu.roll(x, 512 - 16, 1), pltpu.roll(x, 16, 1))
    return x * cs + partner * sn


def _mla_body(t_len, n_ctx, use_rope, pd_ref, ckv_c_ref, kr_c_ref, cs_ref, sn_ref, qnorm_ref, wuq_ref, qnw_ref,
              kvnorm_ref, wukv_ref, knw_ref, ones_ref, od_ref, ckvn_ref, k_s, v_s, m_s, l_s, acc_s):
    nt = t_len // TILE
    t_k = n_ctx + t_len
    assert (t_k - TILE) % KEY_BLK == 0
    n_kb = (t_k - TILE) // KEY_BLK
    lane128 = lax.broadcasted_iota(I32, (1, 512), 1) & 127
    lane_head = lax.broadcasted_iota(I32, (1, 256), 1) >> 6

    def head_norm(x, w_ref):
        ss = _segsum2(x * x, ones_ref[...])
        return x * lax.rsqrt(ss * (1.0 / 96.0) + EPS) * w_ref[...]

    def put_kv(r0, ckv_n, kr128, rope_rows):
        kv = _dot(ckv_n.astype(BF16), wukv_ref[...])
        k_all = kv[:, 0:512] + jnp.concatenate([kr128] * 4, axis=1)
        kn = head_norm(k_all, knw_ref)
        if rope_rows is not None:
            kn = _rope512(kn, cs_ref[pl.ds(rope_rows, TILE), :], sn_ref[pl.ds(rope_rows, TILE), :], lane128)
        k_s[pl.ds(r0, TILE), :] = kn.astype(BF16)
        v_s[pl.ds(r0, TILE), :] = kv[:, 512:768].astype(BF16)

    if n_ctx:
        put_kv(0, ckv_c_ref[0], kr_c_ref[0], None)

    def kv_tile(i, carry):
        t0 = pl.multiple_of(i * TILE, TILE)
        ckv_n = _rms(pd_ref[pl.ds(t0, TILE), 256:384], kvnorm_ref[...])
        ckvn_ref[pl.ds(t0, TILE), :] = ckv_n
        put_kv(pl.multiple_of(n_ctx + t0, TILE), ckv_n, pd_ref[pl.ds(t0, TILE), 512:640],
               t0 if use_rope else None)
        return carry

    if nt == 1:
        kv_tile(0, 0)
    else:
        lax.fori_loop(0, nt, kv_tile, 0)

    def q_tile(i, carry):
        t0 = pl.multiple_of(i * TILE, TILE)
        qn = _rms(pd_ref[pl.ds(t0, TILE), 0:256], qnorm_ref[...])
        q = head_norm(_dot(qn.astype(BF16), wuq_ref[...]), qnw_ref)
        if use_rope:
            q = _rope512(q, cs_ref[pl.ds(t0, TILE), :], sn_ref[pl.ds(t0, TILE), :], lane128)
        qb = (q * ATT_SCALE).astype(BF16)
        qhs = [qb[:, 128 * h:128 * (h + 1)] for h in range(4)]

        def first_block(h):
            s = _dot_nt(qhs[h], k_s[0:TILE, 128 * h:128 * (h + 1)])
            m = jnp.max(s, axis=1, keepdims=True)
            p = jnp.exp(s - m)
            return m, jnp.sum(p, axis=1, keepdims=True), _dot(p.astype(BF16), v_s[0:TILE, :])

        if n_kb == 0:
            outs = []
            for h in range(4):
                _, l, acc = first_block(h)
                outs.append(acc / l)
        else:
            for h in range(4):
                m, l, acc = first_block(h)
                m_s[h] = jnp.broadcast_to(m, (TILE, 128))
                l_s[h] = jnp.broadcast_to(l, (TILE, 128))
                acc_s[h] = acc

            def kblock(j, c2):
                r0 = pl.multiple_of(TILE + j * KEY_BLK, TILE)
                for h in range(4):
                    s = _dot_nt(qhs[h], k_s[pl.ds(r0, KEY_BLK), 128 * h:128 * (h + 1)])
                    m_prev = m_s[h]
                    m_new = jnp.maximum(m_prev, jnp.max(s, axis=1, keepdims=True))
                    alpha = jnp.exp(m_prev - m_new)
                    p = jnp.exp(s - jnp.concatenate([m_new] * (KEY_BLK // 128), axis=1))
                    l_s[h] = alpha * l_s[h] + jnp.sum(p, axis=1, keepdims=True)
                    acc_s[h] = (jnp.concatenate([alpha, alpha], axis=1) * acc_s[h]
                                + _dot(p.astype(BF16), v_s[pl.ds(r0, KEY_BLK), :]))
                    m_s[h] = m_new
                return c2

            lax.fori_loop(0, n_kb, kblock, 0)
            outs = []
            for h in range(4):
                l = l_s[h]
                outs.append(acc_s[h] / jnp.concatenate([l, l], axis=1))
        o = outs[3]
        for h in range(3):
            o = jnp.where(lane_head == h, outs[h], o)
        od_ref[pl.ds(t0, TILE), :] = o
        return carry

    if nt == 1:
        q_tile(0, 0)
    else:
        lax.fori_loop(0, nt, q_tile, 0)


def _mla_call(pd, t_len, nb, blk_off, n_ctx, use_rope, ckv_c, kr_c, cs, sn, wts):
    qnorm, wuq, qnw, kvnorm, wukv, knw, ones128 = wts
    t_k = n_ctx + t_len
    c2 = lambda shp: pl.BlockSpec(shp, lambda b: (0,) * len(shp))
    return pl.pallas_call(
        functools.partial(_mla_body, t_len, n_ctx, use_rope),
        grid=(nb,),
        in_specs=[_seq_spec(t_len, PD_W, blk_off),
                  pl.BlockSpec((1, 256, 128), lambda b: (b, 0, 0)),
                  pl.BlockSpec((1, 256, 128), lambda b: (b, 0, 0)),
                  c2((t_len, 128)), c2((t_len, 128)),
                  c2((1, 256)), c2((256, 512)), c2((1, 512)), c2((1, 128)), c2((128, 768)), c2((1, 512)),
                  c2((512, 512))],
        out_specs=[pl.BlockSpec((t_len, 256), lambda b: (b, 0)),
                   pl.BlockSpec((t_len, 128), lambda b: (b, 0))],
        out_shape=[jax.ShapeDtypeStruct((nb * t_len, 256), F32), jax.ShapeDtypeStruct((nb * t_len, 128), F32)],
        scratch_shapes=[pltpu.VMEM((t_k, 512), BF16), pltpu.VMEM((t_k, 256), BF16),
                        pltpu.VMEM((4, TILE, 128), F32), pltpu.VMEM((4, TILE, 128), F32),
                        pltpu.VMEM((4, TILE, 256), F32)],
        compiler_params=_cparams(("arbitrary",), VMEM_LIMIT),
        name="mla",
    )(pd, ckv_c, kr_c, cs, sn, qnorm, wuq, qnw, kvnorm, wukv, knw, ones128)


def _out_body(n_ctx_steps, *refs):
    mix_refs = refs[:8]
    (x_ref, g1_ref, sc_ref, sh_ref, nw_ref, wout_ref, wrh_ref, wrl_ref,
     x1_ref, h2e_ref, aff_ref) = refs[8:]
    is_ctx = pl.program_id(0) < n_ctx_steps
    m = None
    for k in range(4):
        ok = jnp.where(is_ctx, mix_refs[2 * k][...], mix_refs[2 * k + 1][...]).astype(BF16)
        mk = _dot(ok, wout_ref[256 * k:256 * (k + 1), :])
        m = mk if m is None else m + mk
    x1 = x_ref[...] + g1_ref[0] * m
    x1_ref[...] = x1
    h2 = _rms(x1, nw_ref[...]) * (1.0 + sc_ref[0]) + sh_ref[0]
    hh, hl = _split_bf16(h2)
    lg = _dot(hh, wrh_ref[...]) + _dot(hl, wrh_ref[...]) + _dot(hh, wrl_ref[...])
    lane = lax.broadcasted_iota(I32, (1, 128), 1)
    lg = jnp.where(lane < N_EXP, lg, -jnp.inf)
    ex = jnp.exp(lg - jnp.max(lg, axis=1, keepdims=True))
    aff = ex / jnp.sum(ex, axis=1, keepdims=True)
    a_hi = aff.astype(BF16).astype(F32)
    a_lo = aff - a_hi
    ext = a_hi + pltpu.roll(a_lo, N_EXP, 1)
    h2e_ref[:, 0:D] = hh
    h2e_ref[:, D:HEXT] = ext.astype(BF16)
    aff_ref[...] = aff


def _out_call(mix, x, g1, sc2, sh2, norm2, wout_b, wr_h, wr_l):
    tm = 512
    n = NTOK // tm
    per_seg = SEG // tm
    n_ctx_steps = mix[0].shape[0] // tm
    row = lambda i: (i, 0)
    c2 = lambda shp: pl.BlockSpec(shp, lambda i: (0,) * len(shp))
    modspec = pl.BlockSpec((1, 1, D), lambda i: (i // per_seg, 0, 0))
    ctx_spec = pl.BlockSpec((tm, 256), lambda i: (jnp.minimum(i, n_ctx_steps - 1), 0))
    lat_spec = pl.BlockSpec((tm, 256), lambda i: (jnp.maximum(i - n_ctx_steps, 0), 0))
    return pl.pallas_call(
        functools.partial(_out_body, n_ctx_steps),
        grid=(n,),
        in_specs=[ctx_spec, lat_spec] * 4 + [pl.BlockSpec((tm, D), row), modspec, modspec, modspec,
                  c2((1, D)), c2((D, D)), c2((D, 128)), c2((D, 128))],
        out_specs=[pl.BlockSpec((tm, D), row), pl.BlockSpec((tm, HEXT), row), pl.BlockSpec((tm, 128), row)],
        out_shape=[jax.ShapeDtypeStruct((NTOK, D), F32), jax.ShapeDtypeStruct((NTOK, HEXT), BF16),
                   jax.ShapeDtypeStruct((NTOK, 128), F32)],
        compiler_params=_cparams(("arbitrary",), VMEM_LIMIT),
        name="out_proj",
    )(*mix, x, g1, sc2, sh2, norm2, wout_b, wr_h, wr_l)


def _sel_body(n_grp, cap, aff_ref, tri_ref, segt_ref, slot_ref, cum_ref, pref_s):
    w = SEG // n_grp
    nblk = SEG // 256
    aff = jnp.transpose(aff_ref[...])[0:N_EXP, :]
    pref_s[...] = jnp.zeros((N_EXP, SEG), I32)

    def grp_cols(fn):
        return jnp.concatenate([jnp.broadcast_to(fn(g), (N_EXP, w)) for g in range(n_grp)], axis=1)

    def it(i, carry):
        bit = lax.shift_left(jnp.int32(1), 30 - i)
        cand = pref_s[...] | bit
        ge = jnp.where(aff >= pltpu.bitcast(cand, F32), 1.0, 0.0)
        ok = grp_cols(lambda g: jnp.where(
            jnp.sum(ge[:, g * w:(g + 1) * w], axis=1, keepdims=True) >= cap, 1.0, 0.0))
        pref_s[...] = jnp.where(ok > 0.5, cand, pref_s[...])
        return carry

    lax.fori_loop(0, 31, it, 0)
    thr = pref_s[...]

    def grp_cumsum(x):
        outs, off = [], None
        for b in range(nblk):
            loc = _dot(x[:, 256 * b:256 * (b + 1)].astype(BF16), tri_ref[...])
            if (256 * b) % w == 0:
                off = None
            if off is not None:
                loc = loc + off
            off = loc[:, 255:256]
            outs.append(loc)
        return jnp.concatenate(outs, axis=1)

    gt = jnp.where(aff >= pltpu.bitcast(thr + 1, F32), 1.0, 0.0)
    eq = jnp.where(aff >= pltpu.bitcast(thr, F32), 1.0, 0.0) - gt
    room = grp_cols(lambda g: cap - jnp.sum(gt[:, g * w:(g + 1) * w], axis=1, keepdims=True))
    sel = jnp.where((gt > 0.5) | ((eq > 0.5) & (grp_cumsum(eq) <= room)), 1.0, 0.0)
    base = grp_cols(lambda g: jnp.full((N_EXP, 1), float(g * cap), F32))
    slot = base + grp_cumsum(sel) - 1.0
    slot_ref[0] = jnp.where(sel > 0.5, slot, -1.0).astype(I32)
    cum_ref[0] = _dot(sel.astype(BF16), segt_ref[...]).astype(I32)


def _sel_call(afft, n_seg, seg_off, n_grp, cap, tri, segt):
    c2 = lambda shp: pl.BlockSpec(shp, lambda s: (0,) * len(shp))
    return pl.pallas_call(
        functools.partial(_sel_body, n_grp, cap),
        grid=(n_seg,),
        in_specs=[pl.BlockSpec((SEG, 128), lambda s: (s + seg_off, 0)), c2((256, 256)), c2((SEG, 128))],
        out_specs=[pl.BlockSpec((1, N_EXP, SEG), lambda s: (s, 0, 0)),
                   pl.BlockSpec((1, N_EXP, 128), lambda s: (s, 0, 0))],
        out_shape=[jax.ShapeDtypeStruct((n_seg, N_EXP, SEG), I32), jax.ShapeDtypeStruct((n_seg, N_EXP, 128), I32)],
        scratch_shapes=[pltpu.VMEM((N_EXP, SEG), I32)],
        compiler_params=_cparams(("arbitrary",), VMEM_LIMIT),
        name="select",
    )(afft, tri, segt)


def _windows(cum_ref, s, e, tk):
    lo = cum_ref[s, e, tk]
    hi = cum_ref[s, e, tk + 1]
    w0 = lax.shift_left(lax.shift_right_logical(lo, 4), 4)
    nw = jnp.where(hi > lo, lax.shift_right_logical(hi - w0 + (WIN - 1), WIN_SHIFT), 0)
    return w0, nw


def _onehot(base, slot_row):
    rows = base + lax.broadcasted_iota(I32, (WIN, 1), 0)
    return jnp.where(rows == slot_row, 1.0, 0.0).astype(BF16)


GATHER_EXPERTS = 16


def _gather_body(cum_ref, slot_ref, h2e_ref, g_ref):
    s, half, tk = pl.program_id(0), pl.program_id(1), pl.program_id(2)

    @pl.when(tk == 0)
    def _():
        g_ref[...] = jnp.zeros(g_ref.shape, BF16)

    wins = [_windows(cum_ref, s, half * GATHER_EXPERTS + j, tk) for j in range(GATHER_EXPERTS)]
    bases = [pl.multiple_of(w0, 16) for w0, _ in wins]
    sel = jnp.concatenate([_onehot(bases[j], slot_ref[0, j:j + 1, :]) for j in range(GATHER_EXPERTS)], axis=0)
    got = _dot(sel, h2e_ref[...]).astype(BF16)
    for j in range(GATHER_EXPERTS):
        g_ref[0, j, pl.ds(bases[j], WIN), :] = (g_ref[0, j, pl.ds(bases[j], WIN), :]
                                                  + got[j * WIN:(j + 1) * WIN, :])
    for j in range(GATHER_EXPERTS):
        def wbody(w, carry, j=j):
            base = pl.multiple_of(wins[j][0] + w * WIN, 16)
            more = _dot(_onehot(base, slot_ref[0, j:j + 1, :]), h2e_ref[...])
            g_ref[0, j, pl.ds(base, WIN), :] = g_ref[0, j, pl.ds(base, WIN), :] + more.astype(BF16)
            return carry

        lax.fori_loop(1, wins[j][1], wbody, 0)


def _gather_call(cum, slot, h2e):
    n_half = N_EXP // GATHER_EXPERTS
    return pl.pallas_call(
        _gather_body,
        grid_spec=pltpu.PrefetchScalarGridSpec(
            num_scalar_prefetch=1, grid=(NSEG, n_half, SEG // TK),
            in_specs=[pl.BlockSpec((1, GATHER_EXPERTS, TK), lambda s, h, t, c: (s, h, t)),
                      pl.BlockSpec((TK, HEXT), lambda s, h, t, c: (s * (SEG // TK) + t, 0))],
            out_specs=pl.BlockSpec((1, GATHER_EXPERTS, SLOT_PAD, HEXT), lambda s, h, t, c: (s, h, 0, 0),
                                   pipeline_mode=pl.Buffered(1))),
        out_shape=jax.ShapeDtypeStruct((NSEG, N_EXP, SLOT_PAD, HEXT), BF16),
        compiler_params=_cparams(("arbitrary", "arbitrary", "arbitrary"), VMEM_LIMIT),
        name="moe_gather",
    )(cum, slot, h2e)


def _ffn_body(g_ref, wg_ref, wu_ref, wd_ref, y_ref, wgb_s, wub_s, wdb_s):
    e = pl.program_id(0)
    wgb_s[...] = wg_ref[0, 0].astype(BF16)
    wub_s[...] = wu_ref[0, 0].astype(BF16)
    wdb_s[...] = wd_ref[0, 0].astype(BF16)
    lane = lax.broadcasted_iota(I32, (1, 128), 1)
    pick = (lane == e) | (lane == e + N_EXP)
    for s in range(NSEG):
        xs = g_ref[s, 0, :, 0:D]
        ext = g_ref[s, 0, :, D:HEXT].astype(F32)
        gate = jnp.sum(jnp.where(pick, ext, 0.0), axis=1, keepdims=True)
        a = _dot(xs, wgb_s[...])
        u = _dot(xs, wub_s[...])
        y = _dot((_silu(a) * u).astype(BF16), wdb_s[...]) * gate
        y_ref[s, 0, 0:CAP_SEG, :] = y.astype(BF16)
        y_ref[s, 0, CAP_SEG:SLOT_PAD, :] = jnp.zeros((SLOT_PAD - CAP_SEG, D), BF16)


def _ffn_call(gath, layer, wg, wu, wd):
    wspec = pl.BlockSpec((1, 1, D, D), lambda e: (layer, e, 0, 0))
    return pl.pallas_call(
        _ffn_body,
        grid=(N_EXP,),
        in_specs=[pl.BlockSpec((NSEG, 1, CAP_SEG, HEXT), lambda e: (0, e, 0, 0)), wspec, wspec, wspec],
        out_specs=pl.BlockSpec((NSEG, 1, SLOT_PAD, D), lambda e: (0, e, 0, 0)),
        out_shape=jax.ShapeDtypeStruct((NSEG, N_EXP, SLOT_PAD, D), BF16),
        scratch_shapes=[pltpu.VMEM((D, D), BF16)] * 3,
        compiler_params=_cparams(("arbitrary",), VMEM_LIMIT),
        name="moe_ffn",
    )(gath, wg, wu, wd)


def _scatter_body(split, cum_ref, slot_ref, y_ref, x1_ref, g2_ref, *o_refs):
    s, tk = pl.program_id(0), pl.program_id(1)
    wins = [_windows(cum_ref, s, e, tk) for e in range(N_EXP)]
    bases = [pl.multiple_of(w0, 16) for w0, _ in wins]
    g2 = g2_ref[0]

    def run(o_ref):
        sel = jnp.concatenate([_onehot(bases[e], slot_ref[0, e:e + 1, :]) for e in range(N_EXP)], axis=0)
        rows = jnp.concatenate([y_ref[0, e, pl.ds(bases[e], WIN), :] for e in range(N_EXP)], axis=0)
        o_ref[...] = x1_ref[...] + g2 * _dot_tn(sel, rows)
        for e in range(N_EXP):
            def wbody(w, carry, e=e):
                base = pl.multiple_of(wins[e][0] + w * WIN, 16)
                o_ref[...] = o_ref[...] + g2 * _dot_tn(_onehot(base, slot_ref[0, e:e + 1, :]),
                                                       y_ref[0, e, pl.ds(base, WIN), :])
                return carry

            lax.fori_loop(1, wins[e][1], wbody, 0)

    if not split:
        run(o_refs[0])
    else:
        pl.when(s == 0)(lambda: run(o_refs[0]))
        pl.when(s > 0)(lambda: run(o_refs[1]))


def _scatter_call(cum, slot, y, x1, g2, split):
    n_t = SEG // TK
    tok_idx = lambda s, t, c: (s * n_t + t, 0)
    if split:
        out_specs = [pl.BlockSpec((TK, D), lambda s, t, c: (jnp.where(s == 0, t, n_t - 1), 0)),
                     pl.BlockSpec((TK, D), lambda s, t, c: (jnp.maximum((s - 1) * n_t + t, 0), 0))]
        out_shape = [jax.ShapeDtypeStruct((SEG, D), F32), jax.ShapeDtypeStruct((NTOK - SEG, D), F32)]
    else:
        out_specs = pl.BlockSpec((TK, D), tok_idx)
        out_shape = jax.ShapeDtypeStruct((NTOK, D), F32)
    return pl.pallas_call(
        functools.partial(_scatter_body, split),
        grid_spec=pltpu.PrefetchScalarGridSpec(
            num_scalar_prefetch=1, grid=(NSEG, n_t),
            in_specs=[pl.BlockSpec((1, N_EXP, TK), lambda s, t, c: (s, 0, t)),
                      pl.BlockSpec((1, N_EXP, SLOT_PAD, D), lambda s, t, c: (s, 0, 0, 0),
                                   pipeline_mode=pl.Buffered(1)),
                      pl.BlockSpec((TK, D), tok_idx),
                      pl.BlockSpec((1, 1, D), lambda s, t, c: (s, 0, 0))],
            out_specs=out_specs),
        out_shape=out_shape,
        compiler_params=_cparams(("arbitrary", "arbitrary"), VMEM_LIMIT),
        name="moe_scatter",
    )(cum, slot, y, x1, g2)


def _block_diag4(w):
    eye = jnp.eye(4, dtype=w.dtype)
    return jnp.einsum('hij,hg->higj', w, eye).reshape(256, 256)


def _np_block_ones(n, blk):
    i = np.arange(n) // blk
    return (i[:, None] == i[None, :]).astype(np.float32)


def _head_cols():
    j = np.arange(128)
    src = np.full(128, -1)
    src[:64] = j[:64]
    src[64:80] = 64 + 2 * (j[64:80] - 64)
    src[80:96] = 64 + 2 * (j[80:96] - 80) + 1
    return src


def _mla_weights(l, mla_q_norm, mla_w_uq, mla_kv_norm, mla_w_ukv, mla_qn, mla_kn):
    src = _head_cols()
    valid = src >= 0
    srcc = np.where(valid, src, 0)
    colq = np.concatenate([h * 96 + srcc for h in range(4)])
    maskq = jnp.asarray(np.tile(valid, 4).astype(np.float32))
    wuq = (mla_w_uq[l][:, colq] * maskq).astype(BF16)
    qnw = (jnp.tile(mla_qn[l][srcc], 4) * maskq).reshape(1, 512)
    knw = (jnp.tile(mla_kn[l][srcc], 4) * maskq).reshape(1, 512)
    jn = np.arange(128)
    nope_valid = jn < 64
    colk = np.concatenate([h * 128 + np.where(nope_valid, jn, 0) for h in range(4)])
    maskk = jnp.asarray(np.tile(nope_valid, 4).astype(np.float32))
    colv = np.concatenate([h * 128 + 64 + np.arange(64) for h in range(4)])
    wukv = jnp.concatenate([mla_w_ukv[l][:, colk] * maskk, mla_w_ukv[l][:, colv]], axis=1).astype(BF16)
    return (mla_q_norm[l].reshape(1, 256), wuq, qnw, mla_kv_norm[l].reshape(1, 128), wukv, knw,
            jnp.asarray(_np_block_ones(512, 128), BF16))


def _krope128(kr):
    z64 = jnp.zeros(kr.shape[:-1] + (64,), kr.dtype)
    z32 = jnp.zeros(kr.shape[:-1] + (32,), kr.dtype)
    return jnp.concatenate([z64, kr[..., 0::2], kr[..., 1::2], z32], axis=-1)


def _rope_tables(n_tokens, grid_w):
    rows = (np.arange(n_tokens) // grid_w).astype(np.float32)
    cols = (np.arange(n_tokens) % grid_w).astype(np.float32)
    n_freq = 8
    inv = jnp.asarray(10000.0, F32) ** (-jnp.arange(n_freq, dtype=F32) / n_freq)
    ang = jnp.concatenate([jnp.asarray(rows)[:, None] * inv, jnp.asarray(cols)[:, None] * inv], axis=-1)
    cos, sin = jnp.cos(ang), jnp.sin(ang)
    one = jnp.ones((n_tokens, 64), F32)
    zero = jnp.zeros((n_tokens, 64), F32)
    cs = jnp.concatenate([one, cos, cos, one[:, :32]], axis=1)
    sn = jnp.concatenate([zero, -sin, sin, zero[:, :32]], axis=1)
    return cs, sn


def _in_weight(w_in_l):
    a, b, c = w_in_l[:, 0:512], w_in_l[:, 512:768], w_in_l[:, 768:2048]
    cq, ckv, kr = w_in_l[:, 2048:2304], w_in_l[:, 2304:2432], w_in_l[:, 2432:2464]
    z96 = jnp.zeros((D, 96), w_in_l.dtype)
    return jnp.concatenate([a, b, c, cq, ckv, kr, z96, _krope128(kr)], axis=1).astype(BF16)


def kernel(x_prompt, x_sample, cache_mla_ckv, cache_mla_krope, state_rglru, state_hgrn, c, c_ctx, norm1_w, norm2_w, w_ada, b_ada, w_in, conv_w, conv_b, lru_wa, lru_ba, lru_wx, lru_bx, lru_lambda, pool_w, pool_scale, hgrn_lower_bounds, hgrn_norm_w, mla_q_norm, mla_w_uq, mla_kv_norm, mla_w_ukv, mla_qk_norm_q, mla_qk_norm_k, w_out, w_router, w_exp_gate, w_exp_up, w_exp_down):
    nbp, t_p = x_prompt.shape[0], x_prompt.shape[1]
    nbs, t_s = x_sample.shape[0], x_sample.shape[1]
    depth = w_in.shape[0]

    n_p = nbp * t_p
    cond8 = jnp.concatenate([c_ctx[None], c, jnp.zeros((5, D), F32)], axis=0)
    mod = _ada_call(cond8, w_ada, b_ada)

    lb_soft = jax.nn.softmax(hgrn_lower_bounds.astype(F32), axis=1)
    lower = jnp.cumsum(lb_soft, axis=1) - lb_soft[:, :1]

    ones64 = jnp.asarray(_np_block_ones(256, 64), BF16)
    tri = jnp.asarray(np.triu(np.ones((256, 256), np.float32)), BF16)
    segt = jnp.asarray((np.arange(SEG)[:, None] < TK * np.arange(128)[None, :]).astype(np.float32), BF16)
    cs_s, sn_s = _rope_tables(t_s, 64)
    cs_p, sn_p = jnp.ones((t_p, 128), F32), jnp.zeros((t_p, 128), F32)
    zero_ctx = jnp.zeros((nbp, 256, 128), F32)

    ckvs, krs, lru_states, hgrn_states = [], [], [], []
    src = (x_prompt.reshape(n_p, D), x_sample.reshape(nbs * t_s, D))
    for l in range(depth):
        m6 = mod[l, 0:NSEG].reshape(NSEG, 6, 1, D)
        sh1, sc1, g1, sh2, sc2, g2 = (m6[:, k] for k in range(6))
        res = _in_call(src, sc1, sh1, norm1_w[l].reshape(1, D), _in_weight(w_in[l]))
        x = res[0] if len(src) == 2 else src[0]
        pa, pb, pc, pd = res[-4:]

        lru_w = (conv_w[l], conv_b[l].reshape(1, 256),
                 jnp.stack([_block_diag4(lru_wa[l, d]) for d in range(2)]).astype(BF16), lru_ba[l].reshape(2, 1, 256),
                 jnp.stack([_block_diag4(lru_wx[l, d]) for d in range(2)]).astype(BF16), lru_bx[l].reshape(2, 1, 256),
                 lru_lambda[l].reshape(2, 1, 256))
        oa_c, lru_fin = _lru_call(pa, t_p, nbp, 0, *lru_w, jnp.zeros((nbp, 2, 256), F32))
        oa_l, _ = _lru_call(pa, t_s, nbs, 1, *lru_w, state_rglru[:, l].astype(F32))
        pw_bd = _block_diag4(pool_w[l]).astype(BF16)
        ob_c = _pool_call(pb, t_p, nbp, 0, pw_bd, pool_scale[l].reshape(1, 256))
        ob_l = _pool_call(pb, t_s, nbs, 1, pw_bd, pool_scale[l].reshape(1, 256))
        eye4 = jnp.eye(4, dtype=F32)
        s0t = jnp.einsum('bzhdv,hg->bzhvgd', state_hgrn[:, l].astype(F32), eye4).reshape(nbs, 2, 256, 256)
        hg_w = (lower[:, l].reshape(2, 1, 256), jnp.tile(hgrn_norm_w[l], 4).reshape(1, 256))
        oc_c, st_p = _hgrn_call(pc, t_p, nbp, 0, *hg_w, jnp.zeros((nbp, 2, 256, 256), F32), ones64)
        oc_l, _ = _hgrn_call(pc, t_s, nbs, 1, *hg_w, s0t, ones64)
        mw = _mla_weights(l, mla_q_norm, mla_w_uq, mla_kv_norm, mla_w_ukv, mla_qk_norm_q, mla_qk_norm_k)
        od_c, ckvn = _mla_call(pd, t_p, nbp, 0, 0, False, zero_ctx, zero_ctx, cs_p, sn_p, mw)
        od_l, _ = _mla_call(pd, t_s, nbs, 1, 256, True, cache_mla_ckv[:, l], _krope128(cache_mla_krope[:, l]),
                            cs_s, sn_s, mw)

        wr_h, wr_l = _split_bf16(jnp.pad(w_router[l], ((0, 0), (0, 128 - N_EXP))))
        x1, h2e, aff = _out_call((oa_c, oa_l, ob_c, ob_l, oc_c, oc_l, od_c, od_l), x, g1, sc2, sh2,
                                 norm2_w[l].reshape(1, D), w_out[l].astype(BF16), wr_h, wr_l)

        slot_p, cum_p = _sel_call(aff, 1, 0, nbp, 2 * t_p // N_EXP, tri, segt)
        slot_s, cum_s = _sel_call(aff, nbs, 1, 1, 2 * t_s // N_EXP, tri, segt)
        slot = jnp.concatenate([slot_p, slot_s], axis=0)
        cum = jnp.concatenate([cum_p, cum_s], axis=0)
        gath = _gather_call(cum, slot, h2e)
        y = _ffn_call(gath, l, w_exp_gate, w_exp_up, w_exp_down)
        src = _scatter_call(cum, slot, y, x1, g2, split=(l == depth - 1))
        src = tuple(src) if l == depth - 1 else (src,)

        ckvs.append(ckvn.reshape(nbp, t_p, 128))
        krs.append(pd[:n_p, 384:416].reshape(nbp, t_p, 32))
        lru_states.append(lru_fin)
        st6 = st_p.reshape(nbp, 2, 4, 64, 4, 64)
        hgrn_states.append(jnp.einsum('bzhvgd,hg->bzhdv', st6, eye4))

    y_c, y_l = src
    return (y_c.reshape(nbp, t_p, D), y_l.reshape(nbs, t_s, D),
            jnp.stack(ckvs, axis=1), jnp.stack(krs, axis=1),
            jnp.stack(lru_states, axis=1), jnp.stack(hgrn_states, axis=1))
```

```python
import functools

import numpy as np
import jax
import jax.numpy as jnp
from jax import lax
from jax.experimental import pallas as pl
from jax.experimental.pallas import tpu as pltpu

F32 = jnp.float32
BF16 = jnp.bfloat16
I32 = jnp.int32

D = 1024
NTOK = 12288
SEG = 4096
NSEG = 3
EPS = 1e-6
TINY = 1e-30
LRU_C = 8.0
N_EXP = 16
CAP_SEG = 512
SLOT_PAD = 576
TILE = 256
TK = 256
WIN = 64
WIN_SHIFT = 6
HEXT = D + 128
VMEM_LIMIT = 56 * 1024 * 1024

PA_W, PB_W, PC_W, PD_W = 512, 256, 1280, 640
IN_PAD_W = PA_W + PB_W + PC_W + PD_W


def _cparams(sem, vmem=None):
    return pltpu.CompilerParams(dimension_semantics=sem, vmem_limit_bytes=vmem)


def _dot(a, b):
    return jnp.dot(a, b, preferred_element_type=F32)


def _dot_nt(a, b):
    return lax.dot_general(a, b, (((1,), (1,)), ((), ())), preferred_element_type=F32)


def _dot_tn(a, b):
    return lax.dot_general(a, b, (((0,), (0,)), ((), ())), preferred_element_type=F32)


def _rms(x, w):
    ms = jnp.mean(x * x, axis=-1, keepdims=True)
    return x * lax.rsqrt(ms + EPS) * w


def _silu(x):
    return x * jax.nn.sigmoid(x)


def _split_bf16(x):
    hi = x.astype(BF16)
    lo = (x - hi.astype(F32)).astype(BF16)
    return hi, lo


def _segsum2(x, ones_blk):
    hi, lo = _split_bf16(x)
    return _dot(hi, ones_blk) + _dot(lo, ones_blk)


def _cumsum_rows(x, r8, rev):
    n = x.shape[0]
    for s in (1, 2, 4):
        if not rev:
            x = jnp.where(r8 >= s, x + pltpu.roll(x, s, 0), x)
        else:
            x = jnp.where(r8 < 8 - s, x + pltpu.roll(x, n - s, 0), x)
    ng = n // 8
    outs = [None] * ng
    c = None
    for g in (range(ng) if not rev else reversed(range(ng))):
        xg = x[8 * g:8 * g + 8]
        if c is not None:
            xg = xg + c
        c = xg[7:8] if not rev else xg[0:1]
        outs[g] = xg
    return jnp.concatenate(outs, axis=0)


def _ada_body(c_ref, w_ref, b_ref, o_ref):
    s = _silu(c_ref[...])
    o_ref[0] = _dot(s.astype(BF16), w_ref[0].astype(BF16)) + b_ref[0]


def _ada_call(cond8, w_ada, b_ada):
    nj = 4
    wj = 6 * D // nj
    return pl.pallas_call(
        _ada_body,
        grid=(2, nj),
        in_specs=[pl.BlockSpec((8, D), lambda l, j: (0, 0)),
                  pl.BlockSpec((1, D, wj), lambda l, j: (l, 0, j)),
                  pl.BlockSpec((1, 1, wj), lambda l, j: (l, 0, j))],
        out_specs=pl.BlockSpec((1, 8, wj), lambda l, j: (l, 0, j)),
        out_shape=jax.ShapeDtypeStruct((2, 8, 6 * D), F32),
        compiler_params=_cparams(("arbitrary", "arbitrary"), VMEM_LIMIT),
        name="ada",
    )(cond8, w_ada, b_ada.reshape(2, 1, 6 * D))


def _in_body(n_ctx_steps, *refs):
    if n_ctx_steps is None:
        x_ref, sc_ref, sh_ref, nw_ref, w_ref, pa_ref, pb_ref, pc_ref, pd_ref = refs
        x = x_ref[...]
    else:
        xc_ref, xl_ref, sc_ref, sh_ref, nw_ref, w_ref, x_ref, pa_ref, pb_ref, pc_ref, pd_ref = refs
        x = jnp.where(pl.program_id(0) < n_ctx_steps, xc_ref[...], xl_ref[...])
        x_ref[...] = x
    h = _rms(x, nw_ref[...]) * (1.0 + sc_ref[0]) + sh_ref[0]
    hb = h.astype(BF16)
    o = 0
    for ref, w in ((pa_ref, PA_W), (pb_ref, PB_W), (pc_ref, PC_W), (pd_ref, PD_W)):
        ref[...] = _dot(hb, w_ref[:, o:o + w])
        o += w


def _in_call(src, sc1, sh1, norm1, w_in_b):
    tm = 512
    n = NTOK // tm
    per_seg = SEG // tm
    row = lambda i: (i, 0)
    modspec = pl.BlockSpec((1, 1, D), lambda i: (i // per_seg, 0, 0))
    tokspec = pl.BlockSpec((tm, D), row)
    widths = (PA_W, PB_W, PC_W, PD_W)
    common_specs = [modspec, modspec, pl.BlockSpec((1, D), lambda i: (0, 0)),
                    pl.BlockSpec((D, IN_PAD_W), lambda i: (0, 0))]
    args = (*src, sc1, sh1, norm1, w_in_b)
    if len(src) == 2:
        n_ctx_steps = src[0].shape[0] // tm
        widths = (D,) + widths
        in_specs = [pl.BlockSpec((tm, D), lambda i: (jnp.minimum(i, n_ctx_steps - 1), 0)),
                    pl.BlockSpec((tm, D), lambda i: (jnp.maximum(i - n_ctx_steps, 0), 0))] + common_specs
    else:
        n_ctx_steps = None
        in_specs = [tokspec] + common_specs
    outs = [jax.ShapeDtypeStruct((NTOK, w), F32) for w in widths]
    out_specs = [pl.BlockSpec((tm, w), row) for w in widths]
    return pl.pallas_call(
        functools.partial(_in_body, n_ctx_steps),
        grid=(n,), in_specs=in_specs, out_specs=out_specs, out_shape=outs,
        compiler_params=_cparams(("arbitrary",), VMEM_LIMIT),
        name="in_proj",
    )(*args)


def _halo_tile(ref, c0, c1, t0, t_len, static_single):
    xa = ref[pl.ds(t0, TILE), c0:c1]
    if static_single:
        z = jnp.zeros((8, c1 - c0), F32)
        return xa, jnp.concatenate([z, xa, z], axis=0)
    ps = pl.multiple_of(jnp.maximum(t0 - 8, 0), 8)
    ns = pl.multiple_of(jnp.minimum(t0 + TILE, t_len - 8), 8)
    prev = jnp.where(t0 > 0, ref[pl.ds(ps, 8), c0:c1], 0.0)
    nxt = jnp.where(t0 + TILE < t_len, ref[pl.ds(ns, 8), c0:c1], 0.0)
    return xa, jnp.concatenate([prev, xa, nxt], axis=0)


def _seq_spec(t_len, width, blk_off):
    idx = lambda b: (b + blk_off, 0)
    if t_len > TILE:
        return pl.BlockSpec((t_len, width), idx, pipeline_mode=pl.Buffered(1))
    return pl.BlockSpec((t_len, width), idx)


def _gelu_tanh(x):
    return 0.5 * x * (1.0 + jnp.tanh(0.7978845608028654 * (x + 0.044715 * (x * x * x))))


def _softplus(x):
    return jnp.maximum(x, 0.0) + jnp.log1p(jnp.exp(-jnp.abs(x)))


def _lru_scan(a, u, c, r8, rev):
    n = a.shape[0]
    for s in (1, 2, 4):
        if not rev:
            m = r8 >= s
            a_sh, u_sh = pltpu.roll(a, s, 0), pltpu.roll(u, s, 0)
        else:
            m = r8 < 8 - s
            a_sh, u_sh = pltpu.roll(a, n - s, 0), pltpu.roll(u, n - s, 0)
        u = jnp.where(m, a * u_sh + u, u)
        a = jnp.where(m, a * a_sh, a)
    ng = n // 8
    outs = [None] * ng
    for g in (range(ng) if not rev else reversed(range(ng))):
        hg = u[8 * g:8 * g + 8] + a[8 * g:8 * g + 8] * c
        c = hg[7:8] if not rev else hg[0:1]
        outs[g] = hg
    return jnp.concatenate(outs, axis=0), c


def _lru_body(t_len, pa_ref, cw_ref, cb_ref, wa_ref, ba_ref, wx_ref, bx_ref, lam_ref, h0_ref,
              oa_ref, hfin_ref, hf_s, ab_s, ub_s):
    nt = t_len // TILE
    single = nt == 1
    r8 = lax.broadcasted_iota(I32, (TILE, 1), 0) & 7
    n_ext = TILE + 16

    def gates(xc, xb, d):
        r = jax.nn.sigmoid(_dot(xb, wa_ref[d]) + ba_ref[d])
        i = jax.nn.sigmoid(_dot(xb, wx_ref[d]) + bx_ref[d])
        log_a = -LRU_C * r * _softplus(-lam_ref[d])
        a = jnp.exp(log_a)
        th = jnp.tanh(log_a)
        mult = jnp.sqrt(jnp.maximum(-2.0 * th / (1.0 - th), 0.0))
        return a, mult * (i * xc)

    def fwd_tile(i, c):
        t0 = pl.multiple_of(i * TILE, TILE)
        xa, ext = _halo_tile(pa_ref, 0, 256, t0, t_len, single)
        xc = cb_ref[...] + xa * cw_ref[1:2, :]
        for j in (0, 2, 3):
            xc = xc + pltpu.roll(ext, n_ext - 7 - j, 0)[0:TILE] * cw_ref[j:j + 1, :]
        xb = xc.astype(BF16)
        a_f, u_f = gates(xc, xb, 0)
        h, c = _lru_scan(a_f, u_f, c, r8, False)
        hf_s[pl.ds(t0, TILE), :] = h
        a_b, u_b = gates(xc, xb, 1)
        ab_s[pl.ds(t0, TILE), :] = a_b
        ub_s[pl.ds(t0, TILE), :] = u_b
        return c

    def bwd_tile(k, c):
        t0 = pl.multiple_of((nt - 1 - k) * TILE, TILE)
        h_b, c = _lru_scan(ab_s[pl.ds(t0, TILE), :], ub_s[pl.ds(t0, TILE), :], c, r8, True)
        gate = pa_ref[pl.ds(t0, TILE), 256:512]
        oa_ref[pl.ds(t0, TILE), :] = (hf_s[pl.ds(t0, TILE), :] + h_b) * _gelu_tanh(gate)
        return c

    h0 = h0_ref[0]
    if single:
        c_f = fwd_tile(0, h0[0:1])
        c_b = bwd_tile(0, h0[1:2])
    else:
        c_f = lax.fori_loop(0, nt, fwd_tile, h0[0:1])
        c_b = lax.fori_loop(0, nt, bwd_tile, h0[1:2])
    hfin_ref[0, 0:1, :] = c_f
    hfin_ref[0, 1:2, :] = c_b


def _lru_call(pa, t_len, nb, blk_off, conv_w, conv_b, wa_bd, ba, wx_bd, bx, lam, h0):
    full2 = lambda shp: pl.BlockSpec(shp, lambda b: (0,) * len(shp))
    return pl.pallas_call(
        functools.partial(_lru_body, t_len),
        grid=(nb,),
        in_specs=[_seq_spec(t_len, PA_W, blk_off),
                  full2((4, 256)), full2((1, 256)), full2((2, 256, 256)), full2((2, 1, 256)),
                  full2((2, 256, 256)), full2((2, 1, 256)), full2((2, 1, 256)),
                  pl.BlockSpec((1, 2, 256), lambda b: (b, 0, 0))],
        out_specs=[pl.BlockSpec((t_len, 256), lambda b: (b, 0)),
                   pl.BlockSpec((1, 2, 256), lambda b: (b, 0, 0))],
        out_shape=[jax.ShapeDtypeStruct((nb * t_len, 256), F32), jax.ShapeDtypeStruct((nb, 2, 256), F32)],
        scratch_shapes=[pltpu.VMEM((t_len, 256), F32)] * 3,
        compiler_params=_cparams(("arbitrary",), VMEM_LIMIT),
        name="lru",
    )(pa, conv_w, conv_b, wa_bd, ba, wx_bd, bx, lam, h0)


def _pool_body(t_len, pb_ref, pw_ref, ps_ref, ob_ref):
    nt = t_len // TILE
    single = nt == 1
    n_ext = TILE + 16
    lane = lax.broadcasted_iota(I32, (1, 256), 1)
    rowi = lax.broadcasted_iota(I32, (TILE, 1), 0)

    def ahead(x, k):
        return pltpu.roll(x, n_ext - k, 0)

    def tile(i, carry):
        t0 = pl.multiple_of(i * TILE, TILE)
        xa, ext = _halo_tile(pb_ref, 0, 256, t0, t_len, single)
        p2 = ext + ahead(ext, 1)
        p4 = p2 + ahead(p2, 2)
        p8 = p4 + ahead(p4, 4)
        p16 = p8 + ahead(p8, 8)
        sums = (ahead(p2, 7)[0:TILE], ahead(p4, 6)[0:TILE], ahead(p8, 4)[0:TILE], p16[0:TILE])
        tpos = t0 + rowi
        means = []
        for w, s in zip((2, 4, 8, 16), sums):
            cnt = jnp.minimum(tpos + w // 2, t_len) - jnp.maximum(tpos - w // 2, 0)
            means.append(s / cnt.astype(F32))
        mean = jnp.where(lane < 64, means[0], jnp.where(lane < 128, means[1],
                                                          jnp.where(lane < 192, means[2], means[3])))
        pooled = mean - xa
        ob_ref[pl.ds(t0, TILE), :] = _dot(pooled.astype(BF16), pw_ref[...]) * ps_ref[...]
        return carry

    if single:
        tile(0, 0)
    else:
        lax.fori_loop(0, nt, tile, 0)


def _pool_call(pb, t_len, nb, blk_off, pw_bd, pscale):
    return pl.pallas_call(
        functools.partial(_pool_body, t_len),
        grid=(nb,),
        in_specs=[_seq_spec(t_len, PB_W, blk_off),
                  pl.BlockSpec((256, 256), lambda b: (0, 0)),
                  pl.BlockSpec((1, 256), lambda b: (0, 0))],
        out_specs=pl.BlockSpec((t_len, 256), lambda b: (b, 0)),
        out_shape=jax.ShapeDtypeStruct((nb * t_len, 256), F32),
        compiler_params=_cparams(("arbitrary",), VMEM_LIMIT),
        name="pool",
    )(pb, pw_bd, pscale)


HGRN_LEVELS = (1, 2, 4, 8, 16, 32, 64, 128)


def _hgrn_dir(rev, q, k, v, lf, st_s, att_s, ones_ref, pm_ref, rowi, r8, lane_head):
    c_rows = TILE
    half = TILE // 2
    g = _cumsum_rows(lf, r8, rev)
    vb = v.astype(BF16)
    seg = g
    head_on = [jnp.where(lane_head == h, 1.0, 0.0).astype(BF16) for h in range(4)]
    for li, m in enumerate(HGRN_LEVELS):
        up = (rowi & (2 * m - 1)) >= m
        if not rev:
            ref_q = pltpu.roll(seg, m, 0)
            qsel, ksel = up, jnp.logical_not(up)
            seg_next = jnp.where(up, seg, pltpu.roll(seg, c_rows - m, 0))
        else:
            ref_q = pltpu.roll(seg, c_rows - m, 0)
            qsel, ksel = jnp.logical_not(up), up
            seg_next = jnp.where(up, pltpu.roll(seg, m, 0), seg)
        qp = jnp.where(qsel, q * jnp.exp(g - ref_q), 0.0).astype(BF16)
        kp = jnp.where(ksel, k * jnp.exp(seg - g), 0.0).astype(BF16)

        def scores(qh, kh):
            qs = jnp.concatenate([qp[half * qh:half * (qh + 1)] * head_on[h] for h in range(4)], axis=0)
            return _dot_nt(qs, kp[half * kh:half * (kh + 1)])

        if 2 * m == c_rows:
            cross_q, cross_k = (1, 0) if not rev else (0, 1)
            cross = scores(cross_q, cross_k)
        else:
            for b in range(2):
                prod = scores(b, b)
                for h in range(4):
                    rows = slice(h * half, (h + 1) * half)
                    att_s[li, b, rows, :] = prod[rows] * pm_ref[li]
        seg = seg_next
    vm = [[vb[half * b:half * (b + 1)] * head_on[h] for h in range(4)] for b in range(2)]
    o_halves = []
    for b in range(2):
        ob = None
        for h in range(4):
            rows = slice(h * half, (h + 1) * half)
            att = att_s[0, b, rows, :]
            for li in range(1, len(HGRN_LEVELS) - 1):
                att = att + att_s[li, b, rows, :]
            term = _dot(att.astype(BF16), vm[b][h])
            if b == cross_q:
                term = term + _dot(cross[rows].astype(BF16), vm[cross_k][h])
            ob = term if ob is None else ob + term
        o_halves.append(ob)
    o = _dot((q * k).astype(BF16), ones_ref[...]) * v + jnp.concatenate(o_halves, axis=0)
    st = st_s[...]
    o = o + _dot_nt((q * jnp.exp(g)).astype(BF16), st.astype(BF16))
    g_end = g[c_rows - 1:c_rows] if not rev else g[0:1]
    kd = k * jnp.exp(g_end - g)
    upd = _dot_tn(vb, kd.astype(BF16))
    blk = (lax.broadcasted_iota(I32, (256, 1), 0) >> 6) == (lax.broadcasted_iota(I32, (1, 256), 1) >> 6)
    st_s[...] = st * jnp.exp(g_end) + jnp.where(blk, upd, 0.0)
    return o


def _hgrn_body(t_len, pc_ref, lb_ref, nw_ref, s0_ref, ones_ref, pm_ref, oc_ref, sfin_ref, of_s, st_s, att_s):
    nt = t_len // TILE
    rowi = lax.broadcasted_iota(I32, (TILE, 1), 0)
    r8 = rowi & 7
    lane_head = lax.broadcasted_iota(I32, (1, 256), 1) >> 6

    def load(t0, d):
        q = _silu(pc_ref[pl.ds(t0, TILE), 0:256]) * 0.125
        f_raw = pc_ref[pl.ds(t0, TILE), 256 * (1 + d):256 * (2 + d)]
        v = pc_ref[pl.ds(t0, TILE), 768:1024]
        lb = lb_ref[d]
        f_val = lb + (1.0 - lb) * jax.nn.sigmoid(f_raw)
        lf = jnp.log(jnp.maximum(f_val, TINY))
        return q, 1.0 - f_val, v, lf

    def fwd_tile(i, carry):
        t0 = pl.multiple_of(i * TILE, TILE)
        q, k, v, lf = load(t0, 0)
        of_s[pl.ds(t0, TILE), :] = _hgrn_dir(False, q, k, v, lf, st_s, att_s, ones_ref, pm_ref, rowi, r8, lane_head)
        return carry

    def bwd_tile(kk, carry):
        t0 = pl.multiple_of((nt - 1 - kk) * TILE, TILE)
        q, k, v, lf = load(t0, 1)
        o = of_s[pl.ds(t0, TILE), :] + _hgrn_dir(True, q, k, v, lf, st_s, att_s, ones_ref, pm_ref, rowi, r8, lane_head)
        ms = _segsum2(o * o, ones_ref[...]) * (1.0 / 64.0)
        y = o * lax.rsqrt(ms + EPS) * nw_ref[...]
        oc_ref[pl.ds(t0, TILE), :] = y * _silu(pc_ref[pl.ds(t0, TILE), 1024:1280])
        return carry

    st_s[...] = s0_ref[0, 0]
    if nt == 1:
        fwd_tile(0, 0)
    else:
        lax.fori_loop(0, nt, fwd_tile, 0)
    sfin_ref[0, 0] = st_s[...]
    st_s[...] = s0_ref[0, 1]
    if nt == 1:
        bwd_tile(0, 0)
    else:
        lax.fori_loop(0, nt, bwd_tile, 0)
    sfin_ref[0, 1] = st_s[...]


def _pair_masks():
    i = np.arange(TILE // 2)
    return np.stack([((i[:, None] // (2 * m)) == (i[None, :] // (2 * m))).astype(np.float32)
                     for m in HGRN_LEVELS[:-1]])


def _hgrn_call(pc, t_len, nb, blk_off, lower, normw, s0t, ones64):
    n_lv = len(HGRN_LEVELS) - 1
    return pl.pallas_call(
        functools.partial(_hgrn_body, t_len),
        grid=(nb,),
        in_specs=[_seq_spec(t_len, PC_W, blk_off),
                  pl.BlockSpec((2, 1, 256), lambda b: (0, 0, 0)),
                  pl.BlockSpec((1, 256), lambda b: (0, 0)),
                  pl.BlockSpec((1, 2, 256, 256), lambda b: (b, 0, 0, 0)),
                  pl.BlockSpec((256, 256), lambda b: (0, 0)),
                  pl.BlockSpec((n_lv, TILE // 2, TILE // 2), lambda b: (0, 0, 0))],
        out_specs=[pl.BlockSpec((t_len, 256), lambda b: (b, 0)),
                   pl.BlockSpec((1, 2, 256, 256), lambda b: (b, 0, 0, 0))],
        out_shape=[jax.ShapeDtypeStruct((nb * t_len, 256), F32), jax.ShapeDtypeStruct((nb, 2, 256, 256), F32)],
        scratch_shapes=[pltpu.VMEM((t_len, 256), F32), pltpu.VMEM((256, 256), F32),
                        pltpu.VMEM((len(HGRN_LEVELS) - 1, 2, 2 * TILE, TILE // 2), F32)],
        compiler_params=_cparams(("arbitrary",), VMEM_LIMIT),
        name="hgrn",
    )(pc, lower, normw, s0t, ones64, jnp.asarray(_pair_masks()))


ATT_SCALE = 96.0 ** -0.5
KEY_BLK = 512
EXP_SAFE = 40.0


def _rope512(x, cs128, sn128, lane128):
    cs = jnp.concatenate([cs128] * 4, axis=1)
    sn = jnp.concatenate([sn128] * 4, axis=1)
    partner = jnp.where(lane128 < 80, pltpu.roll(x, 512 - 16, 1), pltpu.roll(x, 16, 1))
    return x * cs + partner * sn


def _mla_body(t_len, n_ctx, use_rope, pd_ref, ckv_c_ref, kr_c_ref, cs_ref, sn_ref, qnorm_ref, wuq_ref, qnw_ref,
              kvnorm_ref, wukv_ref, knw_ref, ones_ref, od_ref, ckvn_ref, k_s, v_s, m_s, l_s, acc_s):
    nt = t_len // TILE
    t_k = n_ctx + t_len
    assert (t_k - TILE) % KEY_BLK == 0
    n_kb = (t_k - TILE) // KEY_BLK
    lane128 = lax.broadcasted_iota(I32, (1, 512), 1) & 127
    lane_head = lax.broadcasted_iota(I32, (1, 256), 1) >> 6

    def head_norm(x, w_ref):
        ss = _segsum2(x * x, ones_ref[...])
        return x * lax.rsqrt(ss * (1.0 / 96.0) + EPS) * w_ref[...]

    def put_kv(r0, ckv_n, kr128, rope_rows):
        kv = _dot(ckv_n.astype(BF16), wukv_ref[...])
        k_all = kv[:, 0:512] + jnp.concatenate([kr128] * 4, axis=1)
        kn = head_norm(k_all, knw_ref)
        if rope_rows is not None:
            kn = _rope512(kn, cs_ref[pl.ds(rope_rows, TILE), :], sn_ref[pl.ds(rope_rows, TILE), :], lane128)
        k_s[pl.ds(r0, TILE), :] = kn.astype(BF16)
        v_s[pl.ds(r0, TILE), :] = kv[:, 512:768].astype(BF16)

    if n_ctx:
        put_kv(0, ckv_c_ref[0], kr_c_ref[0], None)

    def kv_tile(i, carry):
        t0 = pl.multiple_of(i * TILE, TILE)
        ckv_n = _rms(pd_ref[pl.ds(t0, TILE), 256:384], kvnorm_ref[...])
        ckvn_ref[pl.ds(t0, TILE), :] = ckv_n
        put_kv(pl.multiple_of(n_ctx + t0, TILE), ckv_n, pd_ref[pl.ds(t0, TILE), 512:640],
               t0 if use_rope else None)
        return carry

    if nt == 1:
        kv_tile(0, 0)
    else:
        lax.fori_loop(0, nt, kv_tile, 0)

    def q_tile(bounded, i, carry):
        t0 = pl.multiple_of(i * TILE, TILE)
        qn = _rms(pd_ref[pl.ds(t0, TILE), 0:256], qnorm_ref[...])
        q = head_norm(_dot(qn.astype(BF16), wuq_ref[...]), qnw_ref)
        if use_rope:
            q = _rope512(q, cs_ref[pl.ds(t0, TILE), :], sn_ref[pl.ds(t0, TILE), :], lane128)
        qb = (q * ATT_SCALE).astype(BF16)
        qhs = [qb[:, 128 * h:128 * (h + 1)] for h in range(4)]

        def first_block(h):
            s = _dot_nt(qhs[h], k_s[0:TILE, 128 * h:128 * (h + 1)])
            m = jnp.max(s, axis=1, keepdims=True)
            p = jnp.exp(s - m)
            return m, jnp.sum(p, axis=1, keepdims=True), _dot(p.astype(BF16), v_s[0:TILE, :])

        if n_kb == 0:
            outs = []
            for h in range(4):
                _, l, acc = first_block(h)
                outs.append(acc / l)
        elif bounded:
            for h in range(4):
                p = jnp.exp(_dot_nt(qhs[h], k_s[0:TILE, 128 * h:128 * (h + 1)]))
                l_s[h] = jnp.broadcast_to(jnp.sum(p, axis=1, keepdims=True), (TILE, 128))
                acc_s[h] = _dot(p.astype(BF16), v_s[0:TILE, :])

            def kblock_bounded(j, c2):
                r0 = pl.multiple_of(TILE + j * KEY_BLK, TILE)
                for h in range(4):
                    p = jnp.exp(_dot_nt(qhs[h], k_s[pl.ds(r0, KEY_BLK), 128 * h:128 * (h + 1)]))
                    l_s[h] = l_s[h] + jnp.sum(p, axis=1, keepdims=True)
                    acc_s[h] = acc_s[h] + _dot(p.astype(BF16), v_s[pl.ds(r0, KEY_BLK), :])
                return c2

            lax.fori_loop(0, n_kb, kblock_bounded, 0)
            outs = []
            for h in range(4):
                l = l_s[h]
                outs.append(acc_s[h] / jnp.concatenate([l, l], axis=1))
        else:
            for h in range(4):
                m, l, acc = first_block(h)
                m_s[h] = jnp.broadcast_to(m, (TILE, 128))
                l_s[h] = jnp.broadcast_to(l, (TILE, 128))
                acc_s[h] = acc

            def kblock(j, c2):
                r0 = pl.multiple_of(TILE + j * KEY_BLK, TILE)
                for h in range(4):
                    s = _dot_nt(qhs[h], k_s[pl.ds(r0, KEY_BLK), 128 * h:128 * (h + 1)])
                    m_prev = m_s[h]
                    m_new = jnp.maximum(m_prev, jnp.max(s, axis=1, keepdims=True))
                    alpha = jnp.exp(m_prev - m_new)
                    p = jnp.exp(s - jnp.concatenate([m_new] * (KEY_BLK // 128), axis=1))
                    l_s[h] = alpha * l_s[h] + jnp.sum(p, axis=1, keepdims=True)
                    acc_s[h] = (jnp.concatenate([alpha, alpha], axis=1) * acc_s[h]
                                + _dot(p.astype(BF16), v_s[pl.ds(r0, KEY_BLK), :]))
                    m_s[h] = m_new
                return c2

            lax.fori_loop(0, n_kb, kblock, 0)
            outs = []
            for h in range(4):
                l = l_s[h]
                outs.append(acc_s[h] / jnp.concatenate([l, l], axis=1))
        o = outs[3]
        for h in range(3):
            o = jnp.where(lane_head == h, outs[h], o)
        od_ref[pl.ds(t0, TILE), :] = o
        return carry

    if n_kb == 0:
        if nt == 1:
            q_tile(False, 0, 0)
        else:
            lax.fori_loop(0, nt, functools.partial(q_tile, False), 0)
    else:
        score_bound = (96.0 * ATT_SCALE) * jnp.max(jnp.abs(qnw_ref[...])) * jnp.max(jnp.abs(knw_ref[...]))
        safe = score_bound < EXP_SAFE

        @pl.when(safe)
        def _():
            lax.fori_loop(0, nt, functools.partial(q_tile, True), 0)

        @pl.when(jnp.logical_not(safe))
        def _():
            lax.fori_loop(0, nt, functools.partial(q_tile, False), 0)


def _mla_call(pd, t_len, nb, blk_off, n_ctx, use_rope, ckv_c, kr_c, cs, sn, wts):
    qnorm, wuq, qnw, kvnorm, wukv, knw, ones128 = wts
    t_k = n_ctx + t_len
    c2 = lambda shp: pl.BlockSpec(shp, lambda b: (0,) * len(shp))
    return pl.pallas_call(
        functools.partial(_mla_body, t_len, n_ctx, use_rope),
        grid=(nb,),
        in_specs=[_seq_spec(t_len, PD_W, blk_off),
                  pl.BlockSpec((1, 256, 128), lambda b: (b, 0, 0)),
                  pl.BlockSpec((1, 256, 128), lambda b: (b, 0, 0)),
                  c2((t_len, 128)), c2((t_len, 128)),
                  c2((1, 256)), c2((256, 512)), c2((1, 512)), c2((1, 128)), c2((128, 768)), c2((1, 512)),
                  c2((512, 512))],
        out_specs=[pl.BlockSpec((t_len, 256), lambda b: (b, 0)),
                   pl.BlockSpec((t_len, 128), lambda b: (b, 0))],
        out_shape=[jax.ShapeDtypeStruct((nb * t_len, 256), F32), jax.ShapeDtypeStruct((nb * t_len, 128), F32)],
        scratch_shapes=[pltpu.VMEM((t_k, 512), BF16), pltpu.VMEM((t_k, 256), BF16),
                        pltpu.VMEM((4, TILE, 128), F32), pltpu.VMEM((4, TILE, 128), F32),
                        pltpu.VMEM((4, TILE, 256), F32)],
        compiler_params=_cparams(("arbitrary",), VMEM_LIMIT),
        name="mla",
    )(pd, ckv_c, kr_c, cs, sn, qnorm, wuq, qnw, kvnorm, wukv, knw, ones128)


def _out_body(n_ctx_steps, *refs):
    mix_refs = refs[:8]
    (x_ref, g1_ref, sc_ref, sh_ref, nw_ref, wout_ref, wrh_ref, wrl_ref,
     x1_ref, h2e_ref, aff_ref) = refs[8:]
    is_ctx = pl.program_id(0) < n_ctx_steps
    lane = lax.broadcasted_iota(I32, (1, 128), 1)
    rc = 128
    for r0 in range(0, x_ref.shape[0], rc):
        rows = slice(r0, r0 + rc)
        m = None
        for k in range(4):
            ok = jnp.where(is_ctx, mix_refs[2 * k][rows, :], mix_refs[2 * k + 1][rows, :]).astype(BF16)
            mk = _dot(ok, wout_ref[256 * k:256 * (k + 1), :])
            m = mk if m is None else m + mk
        x1 = x_ref[rows, :] + g1_ref[0] * m
        x1_ref[rows, :] = x1
        h2 = _rms(x1, nw_ref[...]) * (1.0 + sc_ref[0]) + sh_ref[0]
        hh, hl = _split_bf16(h2)
        lg = _dot(hh, wrh_ref[...]) + _dot(hl, wrh_ref[...]) + _dot(hh, wrl_ref[...])
        lg = jnp.where(lane < N_EXP, lg, -jnp.inf)
        ex = jnp.exp(lg - jnp.max(lg, axis=1, keepdims=True))
        aff = ex / jnp.sum(ex, axis=1, keepdims=True)
        a_hi = aff.astype(BF16).astype(F32)
        a_lo = aff - a_hi
        ext = a_hi + pltpu.roll(a_lo, N_EXP, 1)
        h2e_ref[rows, 0:D] = hh
        h2e_ref[rows, D:HEXT] = ext.astype(BF16)
        aff_ref[rows, :] = aff


def _out_call(mix, x, g1, sc2, sh2, norm2, wout_b, wr_h, wr_l):
    tm = 512
    n = NTOK // tm
    per_seg = SEG // tm
    n_ctx_steps = mix[0].shape[0] // tm
    row = lambda i: (i, 0)
    c2 = lambda shp: pl.BlockSpec(shp, lambda i: (0,) * len(shp))
    modspec = pl.BlockSpec((1, 1, D), lambda i: (i // per_seg, 0, 0))
    ctx_spec = pl.BlockSpec((tm, 256), lambda i: (jnp.minimum(i, n_ctx_steps - 1), 0))
    lat_spec = pl.BlockSpec((tm, 256), lambda i: (jnp.maximum(i - n_ctx_steps, 0), 0))
    return pl.pallas_call(
        functools.partial(_out_body, n_ctx_steps),
        grid=(n,),
        in_specs=[ctx_spec, lat_spec] * 4 + [pl.BlockSpec((tm, D), row), modspec, modspec, modspec,
                  c2((1, D)), c2((D, D)), c2((D, 128)), c2((D, 128))],
        out_specs=[pl.BlockSpec((tm, D), row), pl.BlockSpec((tm, HEXT), row), pl.BlockSpec((tm, 128), row)],
        out_shape=[jax.ShapeDtypeStruct((NTOK, D), F32), jax.ShapeDtypeStruct((NTOK, HEXT), BF16),
                   jax.ShapeDtypeStruct((NTOK, 128), F32)],
        compiler_params=_cparams(("arbitrary",), VMEM_LIMIT),
        name="out_proj",
    )(*mix, x, g1, sc2, sh2, norm2, wout_b, wr_h, wr_l)


def _sel_body(n_grp, cap, aff_ref, tri_ref, segt_ref, slot_ref, cum_ref, pref_s):
    w = SEG // n_grp
    nblk = SEG // 256
    aff = jnp.transpose(aff_ref[...])[0:N_EXP, :]
    pref_s[...] = jnp.zeros((N_EXP, SEG), I32)

    def grp_cols(fn):
        return jnp.concatenate([jnp.broadcast_to(fn(g), (N_EXP, w)) for g in range(n_grp)], axis=1)

    def it(i, carry):
        bit = lax.shift_left(jnp.int32(1), 30 - i)
        cand = pref_s[...] | bit
        ge = jnp.where(aff >= pltpu.bitcast(cand, F32), 1.0, 0.0)
        ok = grp_cols(lambda g: jnp.where(
            jnp.sum(ge[:, g * w:(g + 1) * w], axis=1, keepdims=True) >= cap, 1.0, 0.0))
        pref_s[...] = jnp.where(ok > 0.5, cand, pref_s[...])
        return carry

    lax.fori_loop(0, 31, it, 0)
    thr = pref_s[...]

    def grp_cumsum(x):
        outs, off = [], None
        for b in range(nblk):
            loc = _dot(x[:, 256 * b:256 * (b + 1)].astype(BF16), tri_ref[...])
            if (256 * b) % w == 0:
                off = None
            if off is not None:
                loc = loc + off
            off = loc[:, 255:256]
            outs.append(loc)
        return jnp.concatenate(outs, axis=1)

    gt = jnp.where(aff >= pltpu.bitcast(thr + 1, F32), 1.0, 0.0)
    eq = jnp.where(aff >= pltpu.bitcast(thr, F32), 1.0, 0.0) - gt
    room = grp_cols(lambda g: cap - jnp.sum(gt[:, g * w:(g + 1) * w], axis=1, keepdims=True))
    sel = jnp.where((gt > 0.5) | ((eq > 0.5) & (grp_cumsum(eq) <= room)), 1.0, 0.0)
    base = grp_cols(lambda g: jnp.full((N_EXP, 1), float(g * cap), F32))
    slot = base + grp_cumsum(sel) - 1.0
    slot_ref[0] = jnp.where(sel > 0.5, slot, -1.0).astype(I32)
    cum_ref[0] = _dot(sel.astype(BF16), segt_ref[...]).astype(I32)


def _sel_call(afft, n_seg, seg_off, n_grp, cap, tri, segt):
    c2 = lambda shp: pl.BlockSpec(shp, lambda s: (0,) * len(shp))
    return pl.pallas_call(
        functools.partial(_sel_body, n_grp, cap),
        grid=(n_seg,),
        in_specs=[pl.BlockSpec((SEG, 128), lambda s: (s + seg_off, 0)), c2((256, 256)), c2((SEG, 128))],
        out_specs=[pl.BlockSpec((1, N_EXP, SEG), lambda s: (s, 0, 0)),
                   pl.BlockSpec((1, N_EXP, 128), lambda s: (s, 0, 0))],
        out_shape=[jax.ShapeDtypeStruct((n_seg, N_EXP, SEG), I32), jax.ShapeDtypeStruct((n_seg, N_EXP, 128), I32)],
        scratch_shapes=[pltpu.VMEM((N_EXP, SEG), I32)],
        compiler_params=_cparams(("arbitrary",), VMEM_LIMIT),
        name="select",
    )(afft, tri, segt)


def _windows(cum_ref, s, e, tk):
    lo = cum_ref[s, e, tk]
    hi = cum_ref[s, e, tk + 1]
    w0 = lax.shift_left(lax.shift_right_logical(lo, 4), 4)
    nw = jnp.where(hi > lo, lax.shift_right_logical(hi - w0 + (WIN - 1), WIN_SHIFT), 0)
    return w0, nw


def _onehot(base, slot_row):
    rows = base + lax.broadcasted_iota(I32, (WIN, 1), 0)
    return jnp.where(rows == slot_row, 1.0, 0.0).astype(BF16)


GATHER_EXPERTS = 16


def _gather_body(cum_ref, slot_ref, h2e_ref, g_ref):
    s, half, tk = pl.program_id(0), pl.program_id(1), pl.program_id(2)

    @pl.when(tk == 0)
    def _():
        g_ref[...] = jnp.zeros(g_ref.shape, BF16)

    wins = [_windows(cum_ref, s, half * GATHER_EXPERTS + j, tk) for j in range(GATHER_EXPERTS)]
    bases = [pl.multiple_of(w0, 16) for w0, _ in wins]
    sel = jnp.concatenate([_onehot(bases[j], slot_ref[0, j:j + 1, :]) for j in range(GATHER_EXPERTS)], axis=0)
    got = _dot(sel, h2e_ref[...]).astype(BF16)
    for j in range(GATHER_EXPERTS):
        g_ref[0, j, pl.ds(bases[j], WIN), :] = (g_ref[0, j, pl.ds(bases[j], WIN), :]
                                                  + got[j * WIN:(j + 1) * WIN, :])
    for j in range(GATHER_EXPERTS):
        def wbody(w, carry, j=j):
            base = pl.multiple_of(wins[j][0] + w * WIN, 16)
            more = _dot(_onehot(base, slot_ref[0, j:j + 1, :]), h2e_ref[...])
            g_ref[0, j, pl.ds(base, WIN), :] = g_ref[0, j, pl.ds(base, WIN), :] + more.astype(BF16)
            return carry

        lax.fori_loop(1, wins[j][1], wbody, 0)


def _gather_call(cum, slot, h2e):
    n_half = N_EXP // GATHER_EXPERTS
    return pl.pallas_call(
        _gather_body,
        grid_spec=pltpu.PrefetchScalarGridSpec(
            num_scalar_prefetch=1, grid=(NSEG, n_half, SEG // TK),
            in_specs=[pl.BlockSpec((1, GATHER_EXPERTS, TK), lambda s, h, t, c: (s, h, t)),
                      pl.BlockSpec((TK, HEXT), lambda s, h, t, c: (s * (SEG // TK) + t, 0))],
            out_specs=pl.BlockSpec((1, GATHER_EXPERTS, SLOT_PAD, HEXT), lambda s, h, t, c: (s, h, 0, 0),
                                   pipeline_mode=pl.Buffered(1))),
        out_shape=jax.ShapeDtypeStruct((NSEG, N_EXP, SLOT_PAD, HEXT), BF16),
        compiler_params=_cparams(("arbitrary", "arbitrary", "arbitrary"), VMEM_LIMIT),
        name="moe_gather",
    )(cum, slot, h2e)


def _ffn_body(g_ref, wg_ref, wu_ref, wd_ref, y_ref, wgb_s, wub_s, wdb_s):
    e = pl.program_id(0)
    wgb_s[...] = wg_ref[0, 0].astype(BF16)
    wub_s[...] = wu_ref[0, 0].astype(BF16)
    wdb_s[...] = wd_ref[0, 0].astype(BF16)
    lane = lax.broadcasted_iota(I32, (1, 128), 1)
    pick = (lane == e) | (lane == e + N_EXP)
    for s in range(NSEG):
        xs = g_ref[s, 0, :, 0:D]
        ext = g_ref[s, 0, :, D:HEXT].astype(F32)
        gate = jnp.sum(jnp.where(pick, ext, 0.0), axis=1, keepdims=True)
        a = _dot(xs, wgb_s[...])
        u = _dot(xs, wub_s[...])
        y = _dot((_silu(a) * u).astype(BF16), wdb_s[...]) * gate
        y_ref[s, 0, 0:CAP_SEG, :] = y.astype(BF16)
        y_ref[s, 0, CAP_SEG:SLOT_PAD, :] = jnp.zeros((SLOT_PAD - CAP_SEG, D), BF16)


def _ffn_call(gath, layer, wg, wu, wd):
    wspec = pl.BlockSpec((1, 1, D, D), lambda e: (layer, e, 0, 0))
    return pl.pallas_call(
        _ffn_body,
        grid=(N_EXP,),
        in_specs=[pl.BlockSpec((NSEG, 1, CAP_SEG, HEXT), lambda e: (0, e, 0, 0)), wspec, wspec, wspec],
        out_specs=pl.BlockSpec((NSEG, 1, SLOT_PAD, D), lambda e: (0, e, 0, 0)),
        out_shape=jax.ShapeDtypeStruct((NSEG, N_EXP, SLOT_PAD, D), BF16),
        scratch_shapes=[pltpu.VMEM((D, D), BF16)] * 3,
        compiler_params=_cparams(("arbitrary",), VMEM_LIMIT),
        name="moe_ffn",
    )(gath, wg, wu, wd)


def _scatter_body(split, cum_ref, slot_ref, y_ref, x1_ref, g2_ref, *o_refs):
    s, tk = pl.program_id(0), pl.program_id(1)
    wins = [_windows(cum_ref, s, e, tk) for e in range(N_EXP)]
    bases = [pl.multiple_of(w0, 16) for w0, _ in wins]
    g2 = g2_ref[0]

    def run(o_ref):
        sel = jnp.concatenate([_onehot(bases[e], slot_ref[0, e:e + 1, :]) for e in range(N_EXP)], axis=0)
        rows = jnp.concatenate([y_ref[0, e, pl.ds(bases[e], WIN), :] for e in range(N_EXP)], axis=0)
        o_ref[...] = x1_ref[...] + g2 * _dot_tn(sel, rows)
        for e in range(N_EXP):
            def wbody(w, carry, e=e):
                base = pl.multiple_of(wins[e][0] + w * WIN, 16)
                o_ref[...] = o_ref[...] + g2 * _dot_tn(_onehot(base, slot_ref[0, e:e + 1, :]),
                                                       y_ref[0, e, pl.ds(base, WIN), :])
                return carry

            lax.fori_loop(1, wins[e][1], wbody, 0)

    if not split:
        run(o_refs[0])
    else:
        pl.when(s == 0)(lambda: run(o_refs[0]))
        pl.when(s > 0)(lambda: run(o_refs[1]))


def _scatter_call(cum, slot, y, x1, g2, split):
    n_t = SEG // TK
    tok_idx = lambda s, t, c: (s * n_t + t, 0)
    if split:
        out_specs = [pl.BlockSpec((TK, D), lambda s, t, c: (jnp.where(s == 0, t, n_t - 1), 0)),
                     pl.BlockSpec((TK, D), lambda s, t, c: (jnp.maximum((s - 1) * n_t + t, 0), 0))]
        out_shape = [jax.ShapeDtypeStruct((SEG, D), F32), jax.ShapeDtypeStruct((NTOK - SEG, D), F32)]
    else:
        out_specs = pl.BlockSpec((TK, D), tok_idx)
        out_shape = jax.ShapeDtypeStruct((NTOK, D), F32)
    return pl.pallas_call(
        functools.partial(_scatter_body, split),
        grid_spec=pltpu.PrefetchScalarGridSpec(
            num_scalar_prefetch=1, grid=(NSEG, n_t),
            in_specs=[pl.BlockSpec((1, N_EXP, TK), lambda s, t, c: (s, 0, t)),
                      pl.BlockSpec((1, N_EXP, SLOT_PAD, D), lambda s, t, c: (s, 0, 0, 0),
                                   pipeline_mode=pl.Buffered(1)),
                      pl.BlockSpec((TK, D), tok_idx),
                      pl.BlockSpec((1, 1, D), lambda s, t, c: (s, 0, 0))],
            out_specs=out_specs),
        out_shape=out_shape,
        compiler_params=_cparams(("arbitrary", "arbitrary"), VMEM_LIMIT),
        name="moe_scatter",
    )(cum, slot, y, x1, g2)


def _block_diag4(w):
    eye = jnp.eye(4, dtype=w.dtype)
    return jnp.einsum('hij,hg->higj', w, eye).reshape(256, 256)


def _np_block_ones(n, blk):
    i = np.arange(n) // blk
    return (i[:, None] == i[None, :]).astype(np.float32)


def _head_cols():
    j = np.arange(128)
    src = np.full(128, -1)
    src[:64] = j[:64]
    src[64:80] = 64 + 2 * (j[64:80] - 64)
    src[80:96] = 64 + 2 * (j[80:96] - 80) + 1
    return src


def _mla_weights(l, mla_q_norm, mla_w_uq, mla_kv_norm, mla_w_ukv, mla_qn, mla_kn):
    src = _head_cols()
    valid = src >= 0
    srcc = np.where(valid, src, 0)
    colq = np.concatenate([h * 96 + srcc for h in range(4)])
    maskq = jnp.asarray(np.tile(valid, 4).astype(np.float32))
    wuq = (mla_w_uq[l][:, colq] * maskq).astype(BF16)
    qnw = (jnp.tile(mla_qn[l][srcc], 4) * maskq).reshape(1, 512)
    knw = (jnp.tile(mla_kn[l][srcc], 4) * maskq).reshape(1, 512)
    jn = np.arange(128)
    nope_valid = jn < 64
    colk = np.concatenate([h * 128 + np.where(nope_valid, jn, 0) for h in range(4)])
    maskk = jnp.asarray(np.tile(nope_valid, 4).astype(np.float32))
    colv = np.concatenate([h * 128 + 64 + np.arange(64) for h in range(4)])
    wukv = jnp.concatenate([mla_w_ukv[l][:, colk] * maskk, mla_w_ukv[l][:, colv]], axis=1).astype(BF16)
    return (mla_q_norm[l].reshape(1, 256), wuq, qnw, mla_kv_norm[l].reshape(1, 128), wukv, knw,
            jnp.asarray(_np_block_ones(512, 128), BF16))


def _krope128(kr):
    z64 = jnp.zeros(kr.shape[:-1] + (64,), kr.dtype)
    z32 = jnp.zeros(kr.shape[:-1] + (32,), kr.dtype)
    return jnp.concatenate([z64, kr[..., 0::2], kr[..., 1::2], z32], axis=-1)


def _rope_tables(n_tokens, grid_w):
    rows = (np.arange(n_tokens) // grid_w).astype(np.float32)
    cols = (np.arange(n_tokens) % grid_w).astype(np.float32)
    n_freq = 8
    inv = jnp.asarray(10000.0, F32) ** (-jnp.arange(n_freq, dtype=F32) / n_freq)
    ang = jnp.concatenate([jnp.asarray(rows)[:, None] * inv, jnp.asarray(cols)[:, None] * inv], axis=-1)
    cos, sin = jnp.cos(ang), jnp.sin(ang)
    one = jnp.ones((n_tokens, 64), F32)
    zero = jnp.zeros((n_tokens, 64), F32)
    cs = jnp.concatenate([one, cos, cos, one[:, :32]], axis=1)
    sn = jnp.concatenate([zero, -sin, sin, zero[:, :32]], axis=1)
    return cs, sn


def _in_weight(w_in_l):
    a, b, c = w_in_l[:, 0:512], w_in_l[:, 512:768], w_in_l[:, 768:2048]
    cq, ckv, kr = w_in_l[:, 2048:2304], w_in_l[:, 2304:2432], w_in_l[:, 2432:2464]
    z96 = jnp.zeros((D, 96), w_in_l.dtype)
    return jnp.concatenate([a, b, c, cq, ckv, kr, z96, _krope128(kr)], axis=1).astype(BF16)


def kernel(x_prompt, x_sample, cache_mla_ckv, cache_mla_krope, state_rglru, state_hgrn, c, c_ctx, norm1_w, norm2_w, w_ada, b_ada, w_in, conv_w, conv_b, lru_wa, lru_ba, lru_wx, lru_bx, lru_lambda, pool_w, pool_scale, hgrn_lower_bounds, hgrn_norm_w, mla_q_norm, mla_w_uq, mla_kv_norm, mla_w_ukv, mla_qk_norm_q, mla_qk_norm_k, w_out, w_router, w_exp_gate, w_exp_up, w_exp_down):
    nbp, t_p = x_prompt.shape[0], x_prompt.shape[1]
    nbs, t_s = x_sample.shape[0], x_sample.shape[1]
    depth = w_in.shape[0]

    n_p = nbp * t_p
    cond8 = jnp.concatenate([c_ctx[None], c, jnp.zeros((5, D), F32)], axis=0)
    mod = _ada_call(cond8, w_ada, b_ada)

    lb_soft = jax.nn.softmax(hgrn_lower_bounds.astype(F32), axis=1)
    lower = jnp.cumsum(lb_soft, axis=1) - lb_soft[:, :1]

    ones64 = jnp.asarray(_np_block_ones(256, 64), BF16)
    tri = jnp.asarray(np.triu(np.ones((256, 256), np.float32)), BF16)
    segt = jnp.asarray((np.arange(SEG)[:, None] < TK * np.arange(128)[None, :]).astype(np.float32), BF16)
    cs_s, sn_s = _rope_tables(t_s, 64)
    cs_p, sn_p = jnp.ones((t_p, 128), F32), jnp.zeros((t_p, 128), F32)
    zero_ctx = jnp.zeros((nbp, 256, 128), F32)

    ckvs, krs, lru_states, hgrn_states = [], [], [], []
    src = (x_prompt.reshape(n_p, D), x_sample.reshape(nbs * t_s, D))
    for l in range(depth):
        m6 = mod[l, 0:NSEG].reshape(NSEG, 6, 1, D)
        sh1, sc1, g1, sh2, sc2, g2 = (m6[:, k] for k in range(6))
        res = _in_call(src, sc1, sh1, norm1_w[l].reshape(1, D), _in_weight(w_in[l]))
        x = res[0] if len(src) == 2 else src[0]
        pa, pb, pc, pd = res[-4:]

        lru_w = (conv_w[l], conv_b[l].reshape(1, 256),
                 jnp.stack([_block_diag4(lru_wa[l, d]) for d in range(2)]).astype(BF16), lru_ba[l].reshape(2, 1, 256),
                 jnp.stack([_block_diag4(lru_wx[l, d]) for d in range(2)]).astype(BF16), lru_bx[l].reshape(2, 1, 256),
                 lru_lambda[l].reshape(2, 1, 256))
        oa_c, lru_fin = _lru_call(pa, t_p, nbp, 0, *lru_w, jnp.zeros((nbp, 2, 256), F32))
        oa_l, _ = _lru_call(pa, t_s, nbs, 1, *lru_w, state_rglru[:, l].astype(F32))
        pw_bd = _block_diag4(pool_w[l]).astype(BF16)
        ob_c = _pool_call(pb, t_p, nbp, 0, pw_bd, pool_scale[l].reshape(1, 256))
        ob_l = _pool_call(pb, t_s, nbs, 1, pw_bd, pool_scale[l].reshape(1, 256))
        eye4 = jnp.eye(4, dtype=F32)
        s0t = jnp.einsum('bzhdv,hg->bzhvgd', state_hgrn[:, l].astype(F32), eye4).reshape(nbs, 2, 256, 256)
        hg_w = (lower[:, l].reshape(2, 1, 256), jnp.tile(hgrn_norm_w[l], 4).reshape(1, 256))
        oc_c, st_p = _hgrn_call(pc, t_p, nbp, 0, *hg_w, jnp.zeros((nbp, 2, 256, 256), F32), ones64)
        oc_l, _ = _hgrn_call(pc, t_s, nbs, 1, *hg_w, s0t, ones64)
        mw = _mla_weights(l, mla_q_norm, mla_w_uq, mla_kv_norm, mla_w_ukv, mla_qk_norm_q, mla_qk_norm_k)
        od_c, ckvn = _mla_call(pd, t_p, nbp, 0, 0, False, zero_ctx, zero_ctx, cs_p, sn_p, mw)
        od_l, _ = _mla_call(pd, t_s, nbs, 1, 256, True, cache_mla_ckv[:, l], _krope128(cache_mla_krope[:, l]),
                            cs_s, sn_s, mw)

        wr_h, wr_l = _split_bf16(jnp.pad(w_router[l], ((0, 0), (0, 128 - N_EXP))))
        x1, h2e, aff = _out_call((oa_c, oa_l, ob_c, ob_l, oc_c, oc_l, od_c, od_l), x, g1, sc2, sh2,
                                 norm2_w[l].reshape(1, D), w_out[l].astype(BF16), wr_h, wr_l)

        slot_p, cum_p = _sel_call(aff, 1, 0, nbp, 2 * t_p // N_EXP, tri, segt)
        slot_s, cum_s = _sel_call(aff, nbs, 1, 1, 2 * t_s // N_EXP, tri, segt)
        slot = jnp.concatenate([slot_p, slot_s], axis=0)
        cum = jnp.concatenate([cum_p, cum_s], axis=0)
        gath = _gather_call(cum, slot, h2e)
        y = _ffn_call(gath, l, w_exp_gate, w_exp_up, w_exp_down)
        src = _scatter_call(cum, slot, y, x1, g2, split=(l == depth - 1))
        src = tuple(src) if l == depth - 1 else (src,)

        ckvs.append(ckvn.reshape(nbp, t_p, 128))
        krs.append(pd[:n_p, 384:416].reshape(nbp, t_p, 32))
        lru_states.append(lru_fin)
        st6 = st_p.reshape(nbp, 2, 4, 64, 4, 64)
        hgrn_states.append(jnp.einsum('bzhvgd,hg->bzhdv', st6, eye4))

    y_c, y_l = src
    return (y_c.reshape(nbp, t_p, D), y_l.reshape(nbs, t_s, D),
            jnp.stack(ckvs, axis=1), jnp.stack(krs, axis=1),
            jnp.stack(lru_states, axis=1), jnp.stack(hgrn_states, axis=1))
```

```python
import functools

import numpy as np
import jax
import jax.numpy as jnp
from jax import lax
from jax.experimental import pallas as pl
from jax.experimental.pallas import tpu as pltpu

F32 = jnp.float32
BF16 = jnp.bfloat16
I32 = jnp.int32

D = 1024
NTOK = 12288
SEG = 4096
NSEG = 3
EPS = 1e-6
TINY = 1e-30
LRU_C = 8.0
N_EXP = 16
CAP_SEG = 512
SLOT_PAD = 576
TILE = 256
TK = 256
WIN = 64
WIN_SHIFT = 6
HEXT = D + 128
VMEM_LIMIT = 56 * 1024 * 1024

PA_W, PB_W, PC_W, PD_W = 512, 256, 1280, 640
IN_PAD_W = PA_W + PB_W + PC_W + PD_W


def _cparams(sem, vmem=None):
    return pltpu.CompilerParams(dimension_semantics=sem, vmem_limit_bytes=vmem)


def _dot(a, b):
    return jnp.dot(a, b, preferred_element_type=F32)


def _dot_nt(a, b):
    return lax.dot_general(a, b, (((1,), (1,)), ((), ())), preferred_element_type=F32)


def _dot_tn(a, b):
    return lax.dot_general(a, b, (((0,), (0,)), ((), ())), preferred_element_type=F32)


def _rms(x, w):
    ms = jnp.mean(x * x, axis=-1, keepdims=True)
    return x * lax.rsqrt(ms + EPS) * w


def _silu(x):
    return x * jax.nn.sigmoid(x)


def _split_bf16(x):
    hi = x.astype(BF16)
    lo = (x - hi.astype(F32)).astype(BF16)
    return hi, lo


def _segsum2(x, ones_blk):
    hi, lo = _split_bf16(x)
    return _dot(hi, ones_blk) + _dot(lo, ones_blk)


def _cumsum_rows(x, r8, rev):
    n = x.shape[0]
    for s in (1, 2, 4):
        if not rev:
            x = jnp.where(r8 >= s, x + pltpu.roll(x, s, 0), x)
        else:
            x = jnp.where(r8 < 8 - s, x + pltpu.roll(x, n - s, 0), x)
    ng = n // 8
    outs = [None] * ng
    c = None
    for g in (range(ng) if not rev else reversed(range(ng))):
        xg = x[8 * g:8 * g + 8]
        if c is not None:
            xg = xg + c
        c = xg[7:8] if not rev else xg[0:1]
        outs[g] = xg
    return jnp.concatenate(outs, axis=0)


def _ada_body(c_ref, w_ref, b_ref, o_ref):
    s = _silu(c_ref[...])
    o_ref[0] = _dot(s.astype(BF16), w_ref[0].astype(BF16)) + b_ref[0]


def _ada_call(cond8, w_ada, b_ada):
    nj = 4
    wj = 6 * D // nj
    return pl.pallas_call(
        _ada_body,
        grid=(2, nj),
        in_specs=[pl.BlockSpec((8, D), lambda l, j: (0, 0)),
                  pl.BlockSpec((1, D, wj), lambda l, j: (l, 0, j)),
                  pl.BlockSpec((1, 1, wj), lambda l, j: (l, 0, j))],
        out_specs=pl.BlockSpec((1, 8, wj), lambda l, j: (l, 0, j)),
        out_shape=jax.ShapeDtypeStruct((2, 8, 6 * D), F32),
        compiler_params=_cparams(("arbitrary", "arbitrary"), VMEM_LIMIT),
        name="ada",
    )(cond8, w_ada, b_ada.reshape(2, 1, 6 * D))


def _in_body(n_ctx_steps, *refs):
    if n_ctx_steps is None:
        x_ref, sc_ref, sh_ref, nw_ref, w_ref, pa_ref, pb_ref, pc_ref, pd_ref = refs
        x = x_ref[...]
    else:
        xc_ref, xl_ref, sc_ref, sh_ref, nw_ref, w_ref, x_ref, pa_ref, pb_ref, pc_ref, pd_ref = refs
        x = jnp.where(pl.program_id(0) < n_ctx_steps, xc_ref[...], xl_ref[...])
        x_ref[...] = x
    h = _rms(x, nw_ref[...]) * (1.0 + sc_ref[0]) + sh_ref[0]
    hb = h.astype(BF16)
    o = 0
    for ref, w in ((pa_ref, PA_W), (pb_ref, PB_W), (pc_ref, PC_W), (pd_ref, PD_W)):
        ref[...] = _dot(hb, w_ref[:, o:o + w])
        o += w


def _in_call(src, sc1, sh1, norm1, w_in_b):
    tm = 512
    n = NTOK // tm
    per_seg = SEG // tm
    row = lambda i: (i, 0)
    modspec = pl.BlockSpec((1, 1, D), lambda i: (i // per_seg, 0, 0))
    tokspec = pl.BlockSpec((tm, D), row)
    widths = (PA_W, PB_W, PC_W, PD_W)
    common_specs = [modspec, modspec, pl.BlockSpec((1, D), lambda i: (0, 0)),
                    pl.BlockSpec((D, IN_PAD_W), lambda i: (0, 0))]
    args = (*src, sc1, sh1, norm1, w_in_b)
    if len(src) == 2:
        n_ctx_steps = src[0].shape[0] // tm
        widths = (D,) + widths
        in_specs = [pl.BlockSpec((tm, D), lambda i: (jnp.minimum(i, n_ctx_steps - 1), 0)),
                    pl.BlockSpec((tm, D), lambda i: (jnp.maximum(i - n_ctx_steps, 0), 0))] + common_specs
    else:
        n_ctx_steps = None
        in_specs = [tokspec] + common_specs
    outs = [jax.ShapeDtypeStruct((NTOK, w), F32) for w in widths]
    out_specs = [pl.BlockSpec((tm, w), row) for w in widths]
    return pl.pallas_call(
        functools.partial(_in_body, n_ctx_steps),
        grid=(n,), in_specs=in_specs, out_specs=out_specs, out_shape=outs,
        compiler_params=_cparams(("arbitrary",), VMEM_LIMIT),
        name="in_proj",
    )(*args)


def _halo_tile(ref, c0, c1, t0, t_len, static_single):
    xa = ref[pl.ds(t0, TILE), c0:c1]
    if static_single:
        z = jnp.zeros((8, c1 - c0), F32)
        return xa, jnp.concatenate([z, xa, z], axis=0)
    ps = pl.multiple_of(jnp.maximum(t0 - 8, 0), 8)
    ns = pl.multiple_of(jnp.minimum(t0 + TILE, t_len - 8), 8)
    prev = jnp.where(t0 > 0, ref[pl.ds(ps, 8), c0:c1], 0.0)
    nxt = jnp.where(t0 + TILE < t_len, ref[pl.ds(ns, 8), c0:c1], 0.0)
    return xa, jnp.concatenate([prev, xa, nxt], axis=0)


def _seq_spec(t_len, width, blk_off):
    idx = lambda b: (b + blk_off, 0)
    if t_len > TILE:
        return pl.BlockSpec((t_len, width), idx, pipeline_mode=pl.Buffered(1))
    return pl.BlockSpec((t_len, width), idx)


def _gelu_tanh(x):
    return 0.5 * x * (1.0 + jnp.tanh(0.7978845608028654 * (x + 0.044715 * (x * x * x))))


def _softplus(x):
    return jnp.maximum(x, 0.0) + jnp.log1p(jnp.exp(-jnp.abs(x)))


def _lru_scan(a, u, c, r8, rev):
    n = a.shape[0]
    for s in (1, 2, 4):
        if not rev:
            m = r8 >= s
            a_sh, u_sh = pltpu.roll(a, s, 0), pltpu.roll(u, s, 0)
        else:
            m = r8 < 8 - s
            a_sh, u_sh = pltpu.roll(a, n - s, 0), pltpu.roll(u, n - s, 0)
        u = jnp.where(m, a * u_sh + u, u)
        a = jnp.where(m, a * a_sh, a)
    ng = n // 8
    outs = [None] * ng
    for g in (range(ng) if not rev else reversed(range(ng))):
        hg = u[8 * g:8 * g + 8] + a[8 * g:8 * g + 8] * c
        c = hg[7:8] if not rev else hg[0:1]
        outs[g] = hg
    return jnp.concatenate(outs, axis=0), c


def _lru_body(t_len, pa_ref, cw_ref, cb_ref, wa_ref, ba_ref, wx_ref, bx_ref, lam_ref, h0_ref,
              oa_ref, hfin_ref, hf_s, ab_s, ub_s):
    nt = t_len // TILE
    single = nt == 1
    r8 = lax.broadcasted_iota(I32, (TILE, 1), 0) & 7
    n_ext = TILE + 16

    def gates(xc, xb, d):
        r = jax.nn.sigmoid(_dot(xb, wa_ref[d]) + ba_ref[d])
        i = jax.nn.sigmoid(_dot(xb, wx_ref[d]) + bx_ref[d])
        log_a = -LRU_C * r * _softplus(-lam_ref[d])
        a = jnp.exp(log_a)
        th = jnp.tanh(log_a)
        mult = jnp.sqrt(jnp.maximum(-2.0 * th / (1.0 - th), 0.0))
        return a, mult * (i * xc)

    def fwd_tile(i, c):
        t0 = pl.multiple_of(i * TILE, TILE)
        xa, ext = _halo_tile(pa_ref, 0, 256, t0, t_len, single)
        xc = cb_ref[...] + xa * cw_ref[1:2, :]
        for j in (0, 2, 3):
            xc = xc + pltpu.roll(ext, n_ext - 7 - j, 0)[0:TILE] * cw_ref[j:j + 1, :]
        xb = xc.astype(BF16)
        a_f, u_f = gates(xc, xb, 0)
        h, c = _lru_scan(a_f, u_f, c, r8, False)
        hf_s[pl.ds(t0, TILE), :] = h
        a_b, u_b = gates(xc, xb, 1)
        ab_s[pl.ds(t0, TILE), :] = a_b
        ub_s[pl.ds(t0, TILE), :] = u_b
        return c

    def bwd_tile(k, c):
        t0 = pl.multiple_of((nt - 1 - k) * TILE, TILE)
        h_b, c = _lru_scan(ab_s[pl.ds(t0, TILE), :], ub_s[pl.ds(t0, TILE), :], c, r8, True)
        gate = pa_ref[pl.ds(t0, TILE), 256:512]
        oa_ref[pl.ds(t0, TILE), :] = (hf_s[pl.ds(t0, TILE), :] + h_b) * _gelu_tanh(gate)
        return c

    h0 = h0_ref[0]
    if single:
        c_f = fwd_tile(0, h0[0:1])
        c_b = bwd_tile(0, h0[1:2])
    else:
        c_f = lax.fori_loop(0, nt, fwd_tile, h0[0:1])
        c_b = lax.fori_loop(0, nt, bwd_tile, h0[1:2])
    hfin_ref[0, 0:1, :] = c_f
    hfin_ref[0, 1:2, :] = c_b


def _lru_call(pa, t_len, nb, blk_off, conv_w, conv_b, wa_bd, ba, wx_bd, bx, lam, h0):
    full2 = lambda shp: pl.BlockSpec(shp, lambda b: (0,) * len(shp))
    return pl.pallas_call(
        functools.partial(_lru_body, t_len),
        grid=(nb,),
        in_specs=[_seq_spec(t_len, PA_W, blk_off),
                  full2((4, 256)), full2((1, 256)), full2((2, 256, 256)), full2((2, 1, 256)),
                  full2((2, 256, 256)), full2((2, 1, 256)), full2((2, 1, 256)),
                  pl.BlockSpec((1, 2, 256), lambda b: (b, 0, 0))],
        out_specs=[pl.BlockSpec((t_len, 256), lambda b: (b, 0)),
                   pl.BlockSpec((1, 2, 256), lambda b: (b, 0, 0))],
        out_shape=[jax.ShapeDtypeStruct((nb * t_len, 256), F32), jax.ShapeDtypeStruct((nb, 2, 256), F32)],
        scratch_shapes=[pltpu.VMEM((t_len, 256), F32)] * 3,
        compiler_params=_cparams(("arbitrary",), VMEM_LIMIT),
        name="lru",
    )(pa, conv_w, conv_b, wa_bd, ba, wx_bd, bx, lam, h0)


def _pool_body(t_len, pb_ref, pw_ref, ps_ref, ob_ref):
    nt = t_len // TILE
    single = nt == 1
    n_ext = TILE + 16
    lane = lax.broadcasted_iota(I32, (1, 256), 1)
    rowi = lax.broadcasted_iota(I32, (TILE, 1), 0)

    def ahead(x, k):
        return pltpu.roll(x, n_ext - k, 0)

    def tile(i, carry):
        t0 = pl.multiple_of(i * TILE, TILE)
        xa, ext = _halo_tile(pb_ref, 0, 256, t0, t_len, single)
        p2 = ext + ahead(ext, 1)
        p4 = p2 + ahead(p2, 2)
        p8 = p4 + ahead(p4, 4)
        p16 = p8 + ahead(p8, 8)
        sums = (ahead(p2, 7)[0:TILE], ahead(p4, 6)[0:TILE], ahead(p8, 4)[0:TILE], p16[0:TILE])
        tpos = t0 + rowi
        means = []
        for w, s in zip((2, 4, 8, 16), sums):
            cnt = jnp.minimum(tpos + w // 2, t_len) - jnp.maximum(tpos - w // 2, 0)
            means.append(s / cnt.astype(F32))
        mean = jnp.where(lane < 64, means[0], jnp.where(lane < 128, means[1],
                                                          jnp.where(lane < 192, means[2], means[3])))
        pooled = mean - xa
        ob_ref[pl.ds(t0, TILE), :] = _dot(pooled.astype(BF16), pw_ref[...]) * ps_ref[...]
        return carry

    if single:
        tile(0, 0)
    else:
        lax.fori_loop(0, nt, tile, 0)


def _pool_call(pb, t_len, nb, blk_off, pw_bd, pscale):
    return pl.pallas_call(
        functools.partial(_pool_body, t_len),
        grid=(nb,),
        in_specs=[_seq_spec(t_len, PB_W, blk_off),
                  pl.BlockSpec((256, 256), lambda b: (0, 0)),
                  pl.BlockSpec((1, 256), lambda b: (0, 0))],
        out_specs=pl.BlockSpec((t_len, 256), lambda b: (b, 0)),
        out_shape=jax.ShapeDtypeStruct((nb * t_len, 256), F32),
        compiler_params=_cparams(("arbitrary",), VMEM_LIMIT),
        name="pool",
    )(pb, pw_bd, pscale)


HGRN_LEVELS = (1, 2, 4, 8, 16, 32, 64, 128)


def _hgrn_dir(rev, q, k, v, lf, st_s, att_s, ones_ref, pm_ref, rowi, r8, lane_head):
    c_rows = TILE
    half = TILE // 2
    g = _cumsum_rows(lf, r8, rev)
    vb = v.astype(BF16)
    seg = g
    head_on = [jnp.where(lane_head == h, 1.0, 0.0).astype(BF16) for h in range(4)]
    for li, m in enumerate(HGRN_LEVELS):
        up = (rowi & (2 * m - 1)) >= m
        if not rev:
            ref_q = pltpu.roll(seg, m, 0)
            qsel, ksel = up, jnp.logical_not(up)
            seg_next = jnp.where(up, seg, pltpu.roll(seg, c_rows - m, 0))
        else:
            ref_q = pltpu.roll(seg, c_rows - m, 0)
            qsel, ksel = jnp.logical_not(up), up
            seg_next = jnp.where(up, pltpu.roll(seg, m, 0), seg)
        qp = jnp.where(qsel, q * jnp.exp(g - ref_q), 0.0).astype(BF16)
        kp = jnp.where(ksel, k * jnp.exp(seg - g), 0.0).astype(BF16)

        def scores(qh, kh):
            qs = jnp.concatenate([qp[half * qh:half * (qh + 1)] * head_on[h] for h in range(4)], axis=0)
            return _dot_nt(qs, kp[half * kh:half * (kh + 1)])

        if 2 * m == c_rows:
            cross_q, cross_k = (1, 0) if not rev else (0, 1)
            cross = scores(cross_q, cross_k)
        else:
            for b in range(2):
                prod = scores(b, b)
                for h in range(4):
                    rows = slice(h * half, (h + 1) * half)
                    att_s[li, b, rows, :] = prod[rows] * pm_ref[li]
        seg = seg_next
    vm = [[vb[half * b:half * (b + 1)] * head_on[h] for h in range(4)] for b in range(2)]
    o_halves = []
    for b in range(2):
        ob = None
        for h in range(4):
            rows = slice(h * half, (h + 1) * half)
            att = att_s[0, b, rows, :]
            for li in range(1, len(HGRN_LEVELS) - 1):
                att = att + att_s[li, b, rows, :]
            term = _dot(att.astype(BF16), vm[b][h])
            if b == cross_q:
                term = term + _dot(cross[rows].astype(BF16), vm[cross_k][h])
            ob = term if ob is None else ob + term
        o_halves.append(ob)
    o = _dot((q * k).astype(BF16), ones_ref[...]) * v + jnp.concatenate(o_halves, axis=0)
    st = st_s[...]
    o = o + _dot_nt((q * jnp.exp(g)).astype(BF16), st.astype(BF16))
    g_end = g[c_rows - 1:c_rows] if not rev else g[0:1]
    kd = k * jnp.exp(g_end - g)
    upd = _dot_tn(vb, kd.astype(BF16))
    blk = (lax.broadcasted_iota(I32, (256, 1), 0) >> 6) == (lax.broadcasted_iota(I32, (1, 256), 1) >> 6)
    st_s[...] = st * jnp.exp(g_end) + jnp.where(blk, upd, 0.0)
    return o


def _hgrn_body(t_len, ctx_pass, pc_ref, lb_ref, nw_ref, s_ref, ones_ref, pm_ref, oc_ref, *rest):
    if ctx_pass:
        sfin_ref, of_s, st_s, att_s = rest
    else:
        of_s, st_s, att_s = rest
    nt = t_len // TILE
    rowi = lax.broadcasted_iota(I32, (TILE, 1), 0)
    r8 = rowi & 7
    lane_head = lax.broadcasted_iota(I32, (1, 256), 1) >> 6

    def load(t0, d):
        q = _silu(pc_ref[pl.ds(t0, TILE), 0:256]) * 0.125
        f_raw = pc_ref[pl.ds(t0, TILE), 256 * (1 + d):256 * (2 + d)]
        v = pc_ref[pl.ds(t0, TILE), 768:1024]
        lb = lb_ref[d]
        f_val = lb + (1.0 - lb) * jax.nn.sigmoid(f_raw)
        lf = jnp.log(jnp.maximum(f_val, TINY))
        return q, 1.0 - f_val, v, lf

    def fwd_tile(i, carry):
        t0 = pl.multiple_of(i * TILE, TILE)
        q, k, v, lf = load(t0, 0)
        of_s[pl.ds(t0, TILE), :] = _hgrn_dir(False, q, k, v, lf, st_s, att_s, ones_ref, pm_ref, rowi, r8, lane_head)
        return carry

    def bwd_tile(kk, carry):
        t0 = pl.multiple_of((nt - 1 - kk) * TILE, TILE)
        q, k, v, lf = load(t0, 1)
        o = of_s[pl.ds(t0, TILE), :] + _hgrn_dir(True, q, k, v, lf, st_s, att_s, ones_ref, pm_ref, rowi, r8, lane_head)
        ms = _segsum2(o * o, ones_ref[...]) * (1.0 / 64.0)
        y = o * lax.rsqrt(ms + EPS) * nw_ref[...]
        oc_ref[pl.ds(t0, TILE), :] = y * _silu(pc_ref[pl.ds(t0, TILE), 1024:1280])
        return carry

    def put_state(d):
        st = st_s[...]
        hi = st.astype(BF16)
        r1 = st - hi.astype(F32)
        mid = r1.astype(BF16)
        lo = (r1 - mid.astype(F32)).astype(BF16)
        sfin_ref[0, d] = _dot(hi, s_ref[...]) + _dot(mid, s_ref[...]) + _dot(lo, s_ref[...])

    for d, tile_fn in ((0, fwd_tile), (1, bwd_tile)):
        st_s[...] = jnp.zeros((256, 256), F32) if ctx_pass else s_ref[0, d]
        if nt == 1:
            tile_fn(0, 0)
        else:
            lax.fori_loop(0, nt, tile_fn, 0, unroll=2)
        if ctx_pass:
            put_state(d)


def _pair_masks():
    i = np.arange(TILE // 2)
    return np.stack([((i[:, None] // (2 * m)) == (i[None, :] // (2 * m))).astype(np.float32)
                     for m in HGRN_LEVELS[:-1]])


def _hgrn_call(pc, t_len, nb, blk_off, lower, normw, s0t, ones64):
    n_lv = len(HGRN_LEVELS) - 1
    ctx_pass = s0t is None
    out_specs = [pl.BlockSpec((t_len, 256), lambda b: (b, 0))]
    out_shape = [jax.ShapeDtypeStruct((nb * t_len, 256), F32)]
    if ctx_pass:
        fold = np.zeros((256, 128), np.float32)
        fold[np.arange(256), np.arange(256) % 64] = 1.0
        s_arg, s_spec = jnp.asarray(fold, BF16), pl.BlockSpec((256, 128), lambda b: (0, 0))
        out_specs.append(pl.BlockSpec((1, 2, 256, 128), lambda b: (b, 0, 0, 0)))
        out_shape.append(jax.ShapeDtypeStruct((nb, 2, 256, 128), F32))
    else:
        s_arg, s_spec = s0t, pl.BlockSpec((1, 2, 256, 256), lambda b: (b, 0, 0, 0))
    return pl.pallas_call(
        functools.partial(_hgrn_body, t_len, ctx_pass),
        grid=(nb,),
        in_specs=[_seq_spec(t_len, PC_W, blk_off),
                  pl.BlockSpec((2, 1, 256), lambda b: (0, 0, 0)),
                  pl.BlockSpec((1, 256), lambda b: (0, 0)),
                  s_spec,
                  pl.BlockSpec((256, 256), lambda b: (0, 0)),
                  pl.BlockSpec((n_lv, TILE // 2, TILE // 2), lambda b: (0, 0, 0))],
        out_specs=out_specs,
        out_shape=out_shape,
        scratch_shapes=[pltpu.VMEM((t_len, 256), F32), pltpu.VMEM((256, 256), F32),
                        pltpu.VMEM((len(HGRN_LEVELS) - 1, 2, 2 * TILE, TILE // 2), F32)],
        compiler_params=_cparams(("arbitrary",), VMEM_LIMIT),
        name="hgrn",
    )(pc, lower, normw, s_arg, ones64, jnp.asarray(_pair_masks()))


ATT_SCALE = 96.0 ** -0.5
KEY_BLK = 512
EXP_SAFE = 40.0


def _rope512(x, cs128, sn128, lane128):
    cs = jnp.concatenate([cs128] * 4, axis=1)
    sn = jnp.concatenate([sn128] * 4, axis=1)
    partner = jnp.where(lane128 < 80, pltpu.roll(x, 512 - 16, 1), pltpu.roll(x, 16, 1))
    return x * cs + partner * sn


def _mla_body(t_len, n_ctx, use_rope, pd_ref, ckv_c_ref, kr_c_ref, cs_ref, sn_ref, qnorm_ref, wuq_ref, qnw_ref,
              kvnorm_ref, wukv_ref, knw_ref, ones_ref, od_ref, ckvn_ref, k_s, v_s, m_s, l_s, acc_s):
    nt = t_len // TILE
    t_k = n_ctx + t_len
    assert (t_k - TILE) % KEY_BLK == 0
    n_kb = (t_k - TILE) // KEY_BLK
    lane128 = lax.broadcasted_iota(I32, (1, 512), 1) & 127
    lane_head = lax.broadcasted_iota(I32, (1, 256), 1) >> 6

    def head_norm(x, w_ref):
        ss = _segsum2(x * x, ones_ref[...])
        return x * lax.rsqrt(ss * (1.0 / 96.0) + EPS) * w_ref[...]

    def put_kv(r0, ckv_n, kr128, rope_rows):
        kv = _dot(ckv_n.astype(BF16), wukv_ref[...])
        k_all = kv[:, 0:512] + jnp.concatenate([kr128] * 4, axis=1)
        kn = head_norm(k_all, knw_ref)
        if rope_rows is not None:
            kn = _rope512(kn, cs_ref[pl.ds(rope_rows, TILE), :], sn_ref[pl.ds(rope_rows, TILE), :], lane128)
        k_s[pl.ds(r0, TILE), :] = kn.astype(BF16)
        v_s[pl.ds(r0, TILE), :] = kv[:, 512:768].astype(BF16)

    if n_ctx:
        put_kv(0, ckv_c_ref[0], kr_c_ref[0], None)

    def kv_tile(i, carry):
        t0 = pl.multiple_of(i * TILE, TILE)
        ckv_n = _rms(pd_ref[pl.ds(t0, TILE), 256:384], kvnorm_ref[...])
        ckvn_ref[pl.ds(t0, TILE), :] = ckv_n
        put_kv(pl.multiple_of(n_ctx + t0, TILE), ckv_n, pd_ref[pl.ds(t0, TILE), 512:640],
               t0 if use_rope else None)
        return carry

    if nt == 1:
        kv_tile(0, 0)
    else:
        lax.fori_loop(0, nt, kv_tile, 0)

    def q_tile(bounded, i, carry):
        t0 = pl.multiple_of(i * TILE, TILE)
        qn = _rms(pd_ref[pl.ds(t0, TILE), 0:256], qnorm_ref[...])
        q = head_norm(_dot(qn.astype(BF16), wuq_ref[...]), qnw_ref)
        if use_rope:
            q = _rope512(q, cs_ref[pl.ds(t0, TILE), :], sn_ref[pl.ds(t0, TILE), :], lane128)
        qb = (q * ATT_SCALE).astype(BF16)
        qhs = [qb[:, 128 * h:128 * (h + 1)] for h in range(4)]

        def first_block(h):
            s = _dot_nt(qhs[h], k_s[0:TILE, 128 * h:128 * (h + 1)])
            m = jnp.max(s, axis=1, keepdims=True)
            p = jnp.exp(s - m)
            return m, jnp.sum(p, axis=1, keepdims=True), _dot(p.astype(BF16), v_s[0:TILE, :])

        if n_kb == 0:
            outs = []
            for h in range(4):
                _, l, acc = first_block(h)
                outs.append(acc / l)
        elif bounded:
            for h in range(4):
                p = jnp.exp(_dot_nt(qhs[h], k_s[0:TILE, 128 * h:128 * (h + 1)]))
                l_s[h] = jnp.broadcast_to(jnp.sum(p, axis=1, keepdims=True), (TILE, 128))
                acc_s[h] = _dot(p.astype(BF16), v_s[0:TILE, :])

            def kblock_bounded(j, c2):
                r0 = pl.multiple_of(TILE + j * KEY_BLK, TILE)
                for h in range(4):
                    p = jnp.exp(_dot_nt(qhs[h], k_s[pl.ds(r0, KEY_BLK), 128 * h:128 * (h + 1)]))
                    l_s[h] = l_s[h] + jnp.sum(p, axis=1, keepdims=True)
                    acc_s[h] = acc_s[h] + _dot(p.astype(BF16), v_s[pl.ds(r0, KEY_BLK), :])
                return c2

            lax.fori_loop(0, n_kb, kblock_bounded, 0, unroll=4)
            outs = []
            for h in range(4):
                l = l_s[h]
                outs.append(acc_s[h] / jnp.concatenate([l, l], axis=1))
        else:
            for h in range(4):
                m, l, acc = first_block(h)
                m_s[h] = jnp.broadcast_to(m, (TILE, 128))
                l_s[h] = jnp.broadcast_to(l, (TILE, 128))
                acc_s[h] = acc

            def kblock(j, c2):
                r0 = pl.multiple_of(TILE + j * KEY_BLK, TILE)
                for h in range(4):
                    s = _dot_nt(qhs[h], k_s[pl.ds(r0, KEY_BLK), 128 * h:128 * (h + 1)])
                    m_prev = m_s[h]
                    m_new = jnp.maximum(m_prev, jnp.max(s, axis=1, keepdims=True))
                    alpha = jnp.exp(m_prev - m_new)
                    p = jnp.exp(s - jnp.concatenate([m_new] * (KEY_BLK // 128), axis=1))
                    l_s[h] = alpha * l_s[h] + jnp.sum(p, axis=1, keepdims=True)
                    acc_s[h] = (jnp.concatenate([alpha, alpha], axis=1) * acc_s[h]
                                + _dot(p.astype(BF16), v_s[pl.ds(r0, KEY_BLK), :]))
                    m_s[h] = m_new
                return c2

            lax.fori_loop(0, n_kb, kblock, 0)
            outs = []
            for h in range(4):
                l = l_s[h]
                outs.append(acc_s[h] / jnp.concatenate([l, l], axis=1))
        o = outs[3]
        for h in range(3):
            o = jnp.where(lane_head == h, outs[h], o)
        od_ref[pl.ds(t0, TILE), :] = o
        return carry

    if n_kb == 0:
        if nt == 1:
            q_tile(False, 0, 0)
        else:
            lax.fori_loop(0, nt, functools.partial(q_tile, False), 0)
    else:
        score_bound = (96.0 * ATT_SCALE) * jnp.max(jnp.abs(qnw_ref[...])) * jnp.max(jnp.abs(knw_ref[...]))
        safe = score_bound < EXP_SAFE

        @pl.when(safe)
        def _():
            lax.fori_loop(0, nt, functools.partial(q_tile, True), 0)

        @pl.when(jnp.logical_not(safe))
        def _():
            lax.fori_loop(0, nt, functools.partial(q_tile, False), 0)


def _mla_call(pd, t_len, nb, blk_off, n_ctx, use_rope, ckv_c, kr_c, cs, sn, wts):
    qnorm, wuq, qnw, kvnorm, wukv, knw, ones128 = wts
    t_k = n_ctx + t_len
    c2 = lambda shp: pl.BlockSpec(shp, lambda b: (0,) * len(shp))
    return pl.pallas_call(
        functools.partial(_mla_body, t_len, n_ctx, use_rope),
        grid=(nb,),
        in_specs=[_seq_spec(t_len, PD_W, blk_off),
                  pl.BlockSpec((1, 256, 128), lambda b: (b, 0, 0)),
                  pl.BlockSpec((1, 256, 128), lambda b: (b, 0, 0)),
                  c2((t_len, 128)), c2((t_len, 128)),
                  c2((1, 256)), c2((256, 512)), c2((1, 512)), c2((1, 128)), c2((128, 768)), c2((1, 512)),
                  c2((512, 512))],
        out_specs=[pl.BlockSpec((t_len, 256), lambda b: (b, 0)),
                   pl.BlockSpec((t_len, 128), lambda b: (b, 0))],
        out_shape=[jax.ShapeDtypeStruct((nb * t_len, 256), F32), jax.ShapeDtypeStruct((nb * t_len, 128), F32)],
        scratch_shapes=[pltpu.VMEM((t_k, 512), BF16), pltpu.VMEM((t_k, 256), BF16),
                        pltpu.VMEM((4, TILE, 128), F32), pltpu.VMEM((4, TILE, 128), F32),
                        pltpu.VMEM((4, TILE, 256), F32)],
        compiler_params=_cparams(("arbitrary",), VMEM_LIMIT),
        name="mla",
    )(pd, ckv_c, kr_c, cs, sn, qnorm, wuq, qnw, kvnorm, wukv, knw, ones128)


def _out_body(n_ctx_steps, *refs):
    mix_refs = refs[:8]
    (x_ref, g1_ref, sc_ref, sh_ref, nw_ref, wout_ref, wrh_ref, wrl_ref,
     x1_ref, h2e_ref, aff_ref) = refs[8:]
    is_ctx = pl.program_id(0) < n_ctx_steps
    lane = lax.broadcasted_iota(I32, (1, 128), 1)
    rc = 128
    for r0 in range(0, x_ref.shape[0], rc):
        rows = slice(r0, r0 + rc)
        m = None
        for k in range(4):
            ok = jnp.where(is_ctx, mix_refs[2 * k][rows, :], mix_refs[2 * k + 1][rows, :]).astype(BF16)
            mk = _dot(ok, wout_ref[256 * k:256 * (k + 1), :])
            m = mk if m is None else m + mk
        x1 = x_ref[rows, :] + g1_ref[0] * m
        x1_ref[rows, :] = x1
        h2 = _rms(x1, nw_ref[...]) * (1.0 + sc_ref[0]) + sh_ref[0]
        hh, hl = _split_bf16(h2)
        lg = _dot(hh, wrh_ref[...]) + _dot(hl, wrh_ref[...]) + _dot(hh, wrl_ref[...])
        lg = jnp.where(lane < N_EXP, lg, -jnp.inf)
        ex = jnp.exp(lg - jnp.max(lg, axis=1, keepdims=True))
        aff = ex / jnp.sum(ex, axis=1, keepdims=True)
        a_hi = aff.astype(BF16).astype(F32)
        a_lo = aff - a_hi
        ext = a_hi + pltpu.roll(a_lo, N_EXP, 1)
        h2e_ref[rows, 0:D] = hh
        h2e_ref[rows, D:HEXT] = ext.astype(BF16)
        aff_ref[rows, :] = aff


def _out_call(mix, x, g1, sc2, sh2, norm2, wout_b, wr_h, wr_l):
    tm = 512
    n = NTOK // tm
    per_seg = SEG // tm
    n_ctx_steps = mix[0].shape[0] // tm
    row = lambda i: (i, 0)
    c2 = lambda shp: pl.BlockSpec(shp, lambda i: (0,) * len(shp))
    modspec = pl.BlockSpec((1, 1, D), lambda i: (i // per_seg, 0, 0))
    ctx_spec = pl.BlockSpec((tm, 256), lambda i: (jnp.minimum(i, n_ctx_steps - 1), 0))
    lat_spec = pl.BlockSpec((tm, 256), lambda i: (jnp.maximum(i - n_ctx_steps, 0), 0))
    return pl.pallas_call(
        functools.partial(_out_body, n_ctx_steps),
        grid=(n,),
        in_specs=[ctx_spec, lat_spec] * 4 + [pl.BlockSpec((tm, D), row), modspec, modspec, modspec,
                  c2((1, D)), c2((D, D)), c2((D, 128)), c2((D, 128))],
        out_specs=[pl.BlockSpec((tm, D), row), pl.BlockSpec((tm, HEXT), row), pl.BlockSpec((tm, 128), row)],
        out_shape=[jax.ShapeDtypeStruct((NTOK, D), F32), jax.ShapeDtypeStruct((NTOK, HEXT), BF16),
                   jax.ShapeDtypeStruct((NTOK, 128), F32)],
        compiler_params=_cparams(("arbitrary",), VMEM_LIMIT),
        name="out_proj",
    )(*mix, x, g1, sc2, sh2, norm2, wout_b, wr_h, wr_l)


def _sel_body(n_grp, cap, aff_ref, tri_ref, segt_ref, slot_ref, cum_ref, pref_s):
    w = SEG // n_grp
    nblk = SEG // 256
    aff = jnp.transpose(aff_ref[...])[0:N_EXP, :]
    pref_s[...] = jnp.zeros((N_EXP, SEG), I32)

    def grp_cols(fn):
        return jnp.concatenate([jnp.broadcast_to(fn(g), (N_EXP, w)) for g in range(n_grp)], axis=1)

    def it(i, carry):
        bit = lax.shift_left(jnp.int32(1), 30 - i)
        cand = pref_s[...] | bit
        ge = jnp.where(aff >= pltpu.bitcast(cand, F32), 1.0, 0.0)
        ok = grp_cols(lambda g: jnp.where(
            jnp.sum(ge[:, g * w:(g + 1) * w], axis=1, keepdims=True) >= cap, 1.0, 0.0))
        pref_s[...] = jnp.where(ok > 0.5, cand, pref_s[...])
        return carry

    lax.fori_loop(0, 31, it, 0)
    thr = pref_s[...]

    def grp_cumsum(x):
        outs, off = [], None
        for b in range(nblk):
            loc = _dot(x[:, 256 * b:256 * (b + 1)].astype(BF16), tri_ref[...])
            if (256 * b) % w == 0:
                off = None
            if off is not None:
                loc = loc + off
            off = loc[:, 255:256]
            outs.append(loc)
        return jnp.concatenate(outs, axis=1)

    gt = jnp.where(aff >= pltpu.bitcast(thr + 1, F32), 1.0, 0.0)
    eq = jnp.where(aff >= pltpu.bitcast(thr, F32), 1.0, 0.0) - gt
    room = grp_cols(lambda g: cap - jnp.sum(gt[:, g * w:(g + 1) * w], axis=1, keepdims=True))
    sel = jnp.where((gt > 0.5) | ((eq > 0.5) & (grp_cumsum(eq) <= room)), 1.0, 0.0)
    base = grp_cols(lambda g: jnp.full((N_EXP, 1), float(g * cap), F32))
    slot = base + grp_cumsum(sel) - 1.0
    slot_ref[0] = jnp.where(sel > 0.5, slot, -1.0).astype(I32)
    cum_ref[0] = _dot(sel.astype(BF16), segt_ref[...]).astype(I32)


def _sel_call(afft, n_seg, seg_off, n_grp, cap, tri, segt):
    c2 = lambda shp: pl.BlockSpec(shp, lambda s: (0,) * len(shp))
    return pl.pallas_call(
        functools.partial(_sel_body, n_grp, cap),
        grid=(n_seg,),
        in_specs=[pl.BlockSpec((SEG, 128), lambda s: (s + seg_off, 0)), c2((256, 256)), c2((SEG, 128))],
        out_specs=[pl.BlockSpec((1, N_EXP, SEG), lambda s: (s, 0, 0)),
                   pl.BlockSpec((1, N_EXP, 128), lambda s: (s, 0, 0))],
        out_shape=[jax.ShapeDtypeStruct((n_seg, N_EXP, SEG), I32), jax.ShapeDtypeStruct((n_seg, N_EXP, 128), I32)],
        scratch_shapes=[pltpu.VMEM((N_EXP, SEG), I32)],
        compiler_params=_cparams(("arbitrary",), VMEM_LIMIT),
        name="select",
    )(afft, tri, segt)


def _windows(cum_ref, s, e, tk):
    lo = cum_ref[s, e, tk]
    hi = cum_ref[s, e, tk + 1]
    w0 = lax.shift_left(lax.shift_right_logical(lo, 4), 4)
    nw = jnp.where(hi > lo, lax.shift_right_logical(hi - w0 + (WIN - 1), WIN_SHIFT), 0)
    return w0, nw


def _onehot(base, slot_row):
    rows = base + lax.broadcasted_iota(I32, (WIN, 1), 0)
    return jnp.where(rows == slot_row, 1.0, 0.0).astype(BF16)


GATHER_EXPERTS = 16


def _gather_body(cum_ref, slot_ref, h2e_ref, g_ref):
    s, half, tk = pl.program_id(0), pl.program_id(1), pl.program_id(2)

    @pl.when(tk == 0)
    def _():
        g_ref[...] = jnp.zeros(g_ref.shape, BF16)

    wins = [_windows(cum_ref, s, half * GATHER_EXPERTS + j, tk) for j in range(GATHER_EXPERTS)]
    bases = [pl.multiple_of(w0, 16) for w0, _ in wins]
    sel = jnp.concatenate([_onehot(bases[j], slot_ref[0, j:j + 1, :]) for j in range(GATHER_EXPERTS)], axis=0)
    got = _dot(sel, h2e_ref[...]).astype(BF16)
    for j in range(GATHER_EXPERTS):
        g_ref[0, j, pl.ds(bases[j], WIN), :] = (g_ref[0, j, pl.ds(bases[j], WIN), :]
                                                  + got[j * WIN:(j + 1) * WIN, :])
    for j in range(GATHER_EXPERTS):
        def wbody(w, carry, j=j):
            base = pl.multiple_of(wins[j][0] + w * WIN, 16)
            more = _dot(_onehot(base, slot_ref[0, j:j + 1, :]), h2e_ref[...])
            g_ref[0, j, pl.ds(base, WIN), :] = g_ref[0, j, pl.ds(base, WIN), :] + more.astype(BF16)
            return carry

        lax.fori_loop(1, wins[j][1], wbody, 0)


def _gather_call(cum, slot, h2e):
    n_half = N_EXP // GATHER_EXPERTS
    return pl.pallas_call(
        _gather_body,
        grid_spec=pltpu.PrefetchScalarGridSpec(
            num_scalar_prefetch=1, grid=(NSEG, n_half, SEG // TK),
            in_specs=[pl.BlockSpec((1, GATHER_EXPERTS, TK), lambda s, h, t, c: (s, h, t)),
                      pl.BlockSpec((TK, HEXT), lambda s, h, t, c: (s * (SEG // TK) + t, 0))],
            out_specs=pl.BlockSpec((1, GATHER_EXPERTS, SLOT_PAD, HEXT), lambda s, h, t, c: (s, h, 0, 0),
                                   pipeline_mode=pl.Buffered(1))),
        out_shape=jax.ShapeDtypeStruct((NSEG, N_EXP, SLOT_PAD, HEXT), BF16),
        compiler_params=_cparams(("arbitrary", "arbitrary", "arbitrary"), VMEM_LIMIT),
        name="moe_gather",
    )(cum, slot, h2e)


def _ffn_body(g_ref, wg_ref, wu_ref, wd_ref, y_ref, wgb_s, wub_s, wdb_s):
    e = pl.program_id(0)
    wgb_s[...] = wg_ref[0, 0].astype(BF16)
    wub_s[...] = wu_ref[0, 0].astype(BF16)
    wdb_s[...] = wd_ref[0, 0].astype(BF16)
    lane = lax.broadcasted_iota(I32, (1, 128), 1)
    pick = (lane == e) | (lane == e + N_EXP)
    for s in range(NSEG):
        xs = g_ref[s, 0, :, 0:D]
        ext = g_ref[s, 0, :, D:HEXT].astype(F32)
        gate = jnp.sum(jnp.where(pick, ext, 0.0), axis=1, keepdims=True)
        a = _dot(xs, wgb_s[...])
        u = _dot(xs, wub_s[...])
        y = _dot((_silu(a) * u).astype(BF16), wdb_s[...]) * gate
        y_ref[s, 0, 0:CAP_SEG, :] = y.astype(BF16)
        y_ref[s, 0, CAP_SEG:SLOT_PAD, :] = jnp.zeros((SLOT_PAD - CAP_SEG, D), BF16)


def _ffn_call(gath, layer, wg, wu, wd):
    wspec = pl.BlockSpec((1, 1, D, D), lambda e: (layer, e, 0, 0))
    return pl.pallas_call(
        _ffn_body,
        grid=(N_EXP,),
        in_specs=[pl.BlockSpec((NSEG, 1, CAP_SEG, HEXT), lambda e: (0, e, 0, 0)), wspec, wspec, wspec],
        out_specs=pl.BlockSpec((NSEG, 1, SLOT_PAD, D), lambda e: (0, e, 0, 0)),
        out_shape=jax.ShapeDtypeStruct((NSEG, N_EXP, SLOT_PAD, D), BF16),
        scratch_shapes=[pltpu.VMEM((D, D), BF16)] * 3,
        compiler_params=_cparams(("arbitrary",), VMEM_LIMIT),
        name="moe_ffn",
    )(gath, wg, wu, wd)


def _scatter_body(split, cum_ref, slot_ref, y_ref, x1_ref, g2_ref, *o_refs):
    s, tk = pl.program_id(0), pl.program_id(1)
    wins = [_windows(cum_ref, s, e, tk) for e in range(N_EXP)]
    bases = [pl.multiple_of(w0, 16) for w0, _ in wins]
    g2 = g2_ref[0]

    def run(o_ref):
        sel = jnp.concatenate([_onehot(bases[e], slot_ref[0, e:e + 1, :]) for e in range(N_EXP)], axis=0)
        rows = jnp.concatenate([y_ref[0, e, pl.ds(bases[e], WIN), :] for e in range(N_EXP)], axis=0)
        o_ref[...] = x1_ref[...] + g2 * _dot_tn(sel, rows)
        for e in range(N_EXP):
            def wbody(w, carry, e=e):
                base = pl.multiple_of(wins[e][0] + w * WIN, 16)
                o_ref[...] = o_ref[...] + g2 * _dot_tn(_onehot(base, slot_ref[0, e:e + 1, :]),
                                                       y_ref[0, e, pl.ds(base, WIN), :])
                return carry

            lax.fori_loop(1, wins[e][1], wbody, 0)

    if not split:
        run(o_refs[0])
    else:
        pl.when(s == 0)(lambda: run(o_refs[0]))
        pl.when(s > 0)(lambda: run(o_refs[1]))


def _scatter_call(cum, slot, y, x1, g2, split):
    n_t = SEG // TK
    tok_idx = lambda s, t, c: (s * n_t + t, 0)
    if split:
        out_specs = [pl.BlockSpec((TK, D), lambda s, t, c: (jnp.where(s == 0, t, n_t - 1), 0)),
                     pl.BlockSpec((TK, D), lambda s, t, c: (jnp.maximum((s - 1) * n_t + t, 0), 0))]
        out_shape = [jax.ShapeDtypeStruct((SEG, D), F32), jax.ShapeDtypeStruct((NTOK - SEG, D), F32)]
    else:
        out_specs = pl.BlockSpec((TK, D), tok_idx)
        out_shape = jax.ShapeDtypeStruct((NTOK, D), F32)
    return pl.pallas_call(
        functools.partial(_scatter_body, split),
        grid_spec=pltpu.PrefetchScalarGridSpec(
            num_scalar_prefetch=1, grid=(NSEG, n_t),
            in_specs=[pl.BlockSpec((1, N_EXP, TK), lambda s, t, c: (s, 0, t)),
                      pl.BlockSpec((1, N_EXP, SLOT_PAD, D), lambda s, t, c: (s, 0, 0, 0),
                                   pipeline_mode=pl.Buffered(1)),
                      pl.BlockSpec((TK, D), tok_idx),
                      pl.BlockSpec((1, 1, D), lambda s, t, c: (s, 0, 0))],
            out_specs=out_specs),
        out_shape=out_shape,
        compiler_params=_cparams(("arbitrary", "arbitrary"), VMEM_LIMIT),
        name="moe_scatter",
    )(cum, slot, y, x1, g2)


def _block_diag4(w):
    eye = jnp.eye(4, dtype=w.dtype)
    return jnp.einsum('hij,hg->higj', w, eye).reshape(256, 256)


def _np_block_ones(n, blk):
    i = np.arange(n) // blk
    return (i[:, None] == i[None, :]).astype(np.float32)


def _head_cols():
    j = np.arange(128)
    src = np.full(128, -1)
    src[:64] = j[:64]
    src[64:80] = 64 + 2 * (j[64:80] - 64)
    src[80:96] = 64 + 2 * (j[80:96] - 80) + 1
    return src


def _mla_weights(l, mla_q_norm, mla_w_uq, mla_kv_norm, mla_w_ukv, mla_qn, mla_kn):
    src = _head_cols()
    valid = src >= 0
    srcc = np.where(valid, src, 0)
    colq = np.concatenate([h * 96 + srcc for h in range(4)])
    maskq = jnp.asarray(np.tile(valid, 4).astype(np.float32))
    wuq = (mla_w_uq[l][:, colq] * maskq).astype(BF16)
    qnw = (jnp.tile(mla_qn[l][srcc], 4) * maskq).reshape(1, 512)
    knw = (jnp.tile(mla_kn[l][srcc], 4) * maskq).reshape(1, 512)
    jn = np.arange(128)
    nope_valid = jn < 64
    colk = np.concatenate([h * 128 + np.where(nope_valid, jn, 0) for h in range(4)])
    maskk = jnp.asarray(np.tile(nope_valid, 4).astype(np.float32))
    colv = np.concatenate([h * 128 + 64 + np.arange(64) for h in range(4)])
    wukv = jnp.concatenate([mla_w_ukv[l][:, colk] * maskk, mla_w_ukv[l][:, colv]], axis=1).astype(BF16)
    return (mla_q_norm[l].reshape(1, 256), wuq, qnw, mla_kv_norm[l].reshape(1, 128), wukv, knw,
            jnp.asarray(_np_block_ones(512, 128), BF16))


def _krope128(kr):
    z64 = jnp.zeros(kr.shape[:-1] + (64,), kr.dtype)
    z32 = jnp.zeros(kr.shape[:-1] + (32,), kr.dtype)
    return jnp.concatenate([z64, kr[..., 0::2], kr[..., 1::2], z32], axis=-1)


def _rope_tables(n_tokens, grid_w):
    rows = (np.arange(n_tokens) // grid_w).astype(np.float32)
    cols = (np.arange(n_tokens) % grid_w).astype(np.float32)
    n_freq = 8
    inv = jnp.asarray(10000.0, F32) ** (-jnp.arange(n_freq, dtype=F32) / n_freq)
    ang = jnp.concatenate([jnp.asarray(rows)[:, None] * inv, jnp.asarray(cols)[:, None] * inv], axis=-1)
    cos, sin = jnp.cos(ang), jnp.sin(ang)
    one = jnp.ones((n_tokens, 64), F32)
    zero = jnp.zeros((n_tokens, 64), F32)
    cs = jnp.concatenate([one, cos, cos, one[:, :32]], axis=1)
    sn = jnp.concatenate([zero, -sin, sin, zero[:, :32]], axis=1)
    return cs, sn


def _in_weight(w_in_l):
    a, b, c = w_in_l[:, 0:512], w_in_l[:, 512:768], w_in_l[:, 768:2048]
    cq, ckv, kr = w_in_l[:, 2048:2304], w_in_l[:, 2304:2432], w_in_l[:, 2432:2464]
    z96 = jnp.zeros((D, 96), w_in_l.dtype)
    return jnp.concatenate([a, b, c, cq, ckv, kr, z96, _krope128(kr)], axis=1).astype(BF16)


def kernel(x_prompt, x_sample, cache_mla_ckv, cache_mla_krope, state_rglru, state_hgrn, c, c_ctx, norm1_w, norm2_w, w_ada, b_ada, w_in, conv_w, conv_b, lru_wa, lru_ba, lru_wx, lru_bx, lru_lambda, pool_w, pool_scale, hgrn_lower_bounds, hgrn_norm_w, mla_q_norm, mla_w_uq, mla_kv_norm, mla_w_ukv, mla_qk_norm_q, mla_qk_norm_k, w_out, w_router, w_exp_gate, w_exp_up, w_exp_down):
    nbp, t_p = x_prompt.shape[0], x_prompt.shape[1]
    nbs, t_s = x_sample.shape[0], x_sample.shape[1]
    depth = w_in.shape[0]

    n_p = nbp * t_p
    cond8 = jnp.concatenate([c_ctx[None], c, jnp.zeros((5, D), F32)], axis=0)
    mod = _ada_call(cond8, w_ada, b_ada)

    lb_soft = jax.nn.softmax(hgrn_lower_bounds.astype(F32), axis=1)
    lower = jnp.cumsum(lb_soft, axis=1) - lb_soft[:, :1]

    ones64 = jnp.asarray(_np_block_ones(256, 64), BF16)
    tri = jnp.asarray(np.triu(np.ones((256, 256), np.float32)), BF16)
    segt = jnp.asarray((np.arange(SEG)[:, None] < TK * np.arange(128)[None, :]).astype(np.float32), BF16)
    cs_s, sn_s = _rope_tables(t_s, 64)
    cs_p, sn_p = jnp.ones((t_p, 128), F32), jnp.zeros((t_p, 128), F32)
    zero_ctx = jnp.zeros((nbp, 256, 128), F32)

    ckvs, krs, lru_states, hgrn_states = [], [], [], []
    src = (x_prompt.reshape(n_p, D), x_sample.reshape(nbs * t_s, D))
    for l in range(depth):
        m6 = mod[l, 0:NSEG].reshape(NSEG, 6, 1, D)
        sh1, sc1, g1, sh2, sc2, g2 = (m6[:, k] for k in range(6))
        res = _in_call(src, sc1, sh1, norm1_w[l].reshape(1, D), _in_weight(w_in[l]))
        x = res[0] if len(src) == 2 else src[0]
        pa, pb, pc, pd = res[-4:]

        lru_w = (conv_w[l], conv_b[l].reshape(1, 256),
                 jnp.stack([_block_diag4(lru_wa[l, d]) for d in range(2)]).astype(BF16), lru_ba[l].reshape(2, 1, 256),
                 jnp.stack([_block_diag4(lru_wx[l, d]) for d in range(2)]).astype(BF16), lru_bx[l].reshape(2, 1, 256),
                 lru_lambda[l].reshape(2, 1, 256))
        oa_c, lru_fin = _lru_call(pa, t_p, nbp, 0, *lru_w, jnp.zeros((nbp, 2, 256), F32))
        oa_l, _ = _lru_call(pa, t_s, nbs, 1, *lru_w, state_rglru[:, l].astype(F32))
        pw_bd = _block_diag4(pool_w[l]).astype(BF16)
        ob_c = _pool_call(pb, t_p, nbp, 0, pw_bd, pool_scale[l].reshape(1, 256))
        ob_l = _pool_call(pb, t_s, nbs, 1, pw_bd, pool_scale[l].reshape(1, 256))
        eye4 = jnp.eye(4, dtype=F32)
        s0t = jnp.einsum('bzhdv,hg->bzhvgd', state_hgrn[:, l].astype(F32), eye4).reshape(nbs, 2, 256, 256)
        hg_w = (lower[:, l].reshape(2, 1, 256), jnp.tile(hgrn_norm_w[l], 4).reshape(1, 256))
        oc_c, st_p = _hgrn_call(pc, t_p, nbp, 0, *hg_w, None, ones64)
        (oc_l,) = _hgrn_call(pc, t_s, nbs, 1, *hg_w, s0t, ones64)
        mw = _mla_weights(l, mla_q_norm, mla_w_uq, mla_kv_norm, mla_w_ukv, mla_qk_norm_q, mla_qk_norm_k)
        od_c, ckvn = _mla_call(pd, t_p, nbp, 0, 0, False, zero_ctx, zero_ctx, cs_p, sn_p, mw)
        od_l, _ = _mla_call(pd, t_s, nbs, 1, 256, True, cache_mla_ckv[:, l], _krope128(cache_mla_krope[:, l]),
                            cs_s, sn_s, mw)

        wr_h, wr_l = _split_bf16(jnp.pad(w_router[l], ((0, 0), (0, 128 - N_EXP))))
        x1, h2e, aff = _out_call((oa_c, oa_l, ob_c, ob_l, oc_c, oc_l, od_c, od_l), x, g1, sc2, sh2,
                                 norm2_w[l].reshape(1, D), w_out[l].astype(BF16), wr_h, wr_l)

        slot_p, cum_p = _sel_call(aff, 1, 0, nbp, 2 * t_p // N_EXP, tri, segt)
        slot_s, cum_s = _sel_call(aff, nbs, 1, 1, 2 * t_s // N_EXP, tri, segt)
        slot = jnp.concatenate([slot_p, slot_s], axis=0)
        cum = jnp.concatenate([cum_p, cum_s], axis=0)
        gath = _gather_call(cum, slot, h2e)
        y = _ffn_call(gath, l, w_exp_gate, w_exp_up, w_exp_down)
        src = _scatter_call(cum, slot, y, x1, g2, split=(l == depth - 1))
        src = tuple(src) if l == depth - 1 else (src,)

        ckvs.append(ckvn.reshape(nbp, t_p, 128))
        krs.append(pd[:n_p, 384:416].reshape(nbp, t_p, 32))
        lru_states.append(lru_fin)
        hgrn_states.append(jnp.swapaxes(st_p[..., :64].reshape(nbp, 2, 4, 64, 64), -1, -2))

    y_c, y_l = src
    return (y_c.reshape(nbp, t_p, D), y_l.reshape(nbs, t_s, D),
            jnp.stack(ckvs, axis=1), jnp.stack(krs, axis=1),
            jnp.stack(lru_states, axis=1), jnp.stack(hgrn_states, axis=1))
```

```python
import functools

import numpy as np
import jax
import jax.numpy as jnp
from jax import lax
from jax.experimental import pallas as pl
from jax.experimental.pallas import tpu as pltpu

F32 = jnp.float32
BF16 = jnp.bfloat16
I32 = jnp.int32

D = 1024
NTOK = 12288
SEG = 4096
NSEG = 3
EPS = 1e-6
TINY = 1e-30
LRU_C = 8.0
N_EXP = 16
CAP_SEG = 512
SLOT_PAD = 576
TILE = 256
TK = 256
WIN = 64
WIN_SHIFT = 6
HEXT = D + 128
VMEM_LIMIT = 56 * 1024 * 1024

PA_W, PB_W, PC_W, PD_W = 512, 256, 1280, 640
IN_PAD_W = PA_W + PB_W + PC_W + PD_W


def _cparams(sem, vmem=None):
    return pltpu.CompilerParams(dimension_semantics=sem, vmem_limit_bytes=vmem)


def _lspec(shape, layer):
    nd = len(shape)
    return pl.BlockSpec((None,) + tuple(shape), lambda *g: (layer,) + (0,) * nd)


def _mod_spec(layer, k, rows_per_seg):
    return pl.BlockSpec((None, None, 1, 1, D), lambda i, *_: (layer, k, i // rows_per_seg, 0, 0))


def _dot(a, b):
    return jnp.dot(a, b, preferred_element_type=F32)


def _dot_nt(a, b):
    return lax.dot_general(a, b, (((1,), (1,)), ((), ())), preferred_element_type=F32)


def _dot_tn(a, b):
    return lax.dot_general(a, b, (((0,), (0,)), ((), ())), preferred_element_type=F32)


def _rms(x, w):
    ms = jnp.mean(x * x, axis=-1, keepdims=True)
    return x * lax.rsqrt(ms + EPS) * w


def _silu(x):
    return x * jax.nn.sigmoid(x)


def _split_bf16(x):
    hi = x.astype(BF16)
    lo = (x - hi.astype(F32)).astype(BF16)
    return hi, lo


def _segsum2(x, ones_blk):
    hi, lo = _split_bf16(x)
    return _dot(hi, ones_blk) + _dot(lo, ones_blk)


def _cumsum_rows(x, r8, rev):
    n = x.shape[0]
    for s in (1, 2, 4):
        if not rev:
            x = jnp.where(r8 >= s, x + pltpu.roll(x, s, 0), x)
        else:
            x = jnp.where(r8 < 8 - s, x + pltpu.roll(x, n - s, 0), x)
    ng = n // 8
    outs = [None] * ng
    c = None
    for g in (range(ng) if not rev else reversed(range(ng))):
        xg = x[8 * g:8 * g + 8]
        if c is not None:
            xg = xg + c
        c = xg[7:8] if not rev else xg[0:1]
        outs[g] = xg
    return jnp.concatenate(outs, axis=0)


def _ada_body(c_ref, w_ref, b_ref, o_ref):
    s = _silu(c_ref[...])
    o_ref[0] = _dot(s.astype(BF16), w_ref[0].astype(BF16)) + b_ref[0]


def _ada_call(cond8, w_ada, b_ada):
    nj = 4
    wj = 6 * D // nj
    return pl.pallas_call(
        _ada_body,
        grid=(2, nj),
        in_specs=[pl.BlockSpec((8, D), lambda l, j: (0, 0)),
                  pl.BlockSpec((1, D, wj), lambda l, j: (l, 0, j)),
                  pl.BlockSpec((1, 1, wj), lambda l, j: (l, 0, j))],
        out_specs=pl.BlockSpec((1, 8, wj), lambda l, j: (l, 0, j)),
        out_shape=jax.ShapeDtypeStruct((2, 8, 6 * D), F32),
        compiler_params=_cparams(("arbitrary", "arbitrary"), VMEM_LIMIT),
        name="ada",
    )(cond8, w_ada, b_ada.reshape(2, 1, 6 * D))


def _in_body(n_ctx_steps, *refs):
    if n_ctx_steps is None:
        x_ref, sc_ref, sh_ref, nw_ref, w_ref, pa_ref, pb_ref, pc_ref, pd_ref = refs
        x = x_ref[...]
    else:
        xc_ref, xl_ref, sc_ref, sh_ref, nw_ref, w_ref, x_ref, pa_ref, pb_ref, pc_ref, pd_ref = refs
        x = jnp.where(pl.program_id(0) < n_ctx_steps, xc_ref[...], xl_ref[...])
        x_ref[...] = x
    h = _rms(x, nw_ref[...]) * (1.0 + sc_ref[0]) + sh_ref[0]
    hb = h.astype(BF16)
    o = 0
    for ref, w in ((pa_ref, PA_W), (pb_ref, PB_W), (pc_ref, PC_W), (pd_ref, PD_W)):
        ref[...] = _dot(hb, w_ref[:, o:o + w])
        o += w


def _in_call(src, layer, mod, norm1, w_in_b):
    tm = 512
    n = NTOK // tm
    per_seg = SEG // tm
    row = lambda i: (i, 0)
    tokspec = pl.BlockSpec((tm, D), row)
    widths = (PA_W, PB_W, PC_W, PD_W)
    common_specs = [_mod_spec(layer, 1, per_seg), _mod_spec(layer, 0, per_seg), _lspec((1, D), layer),
                    _lspec((D, IN_PAD_W), layer)]
    args = (*src, mod, mod, norm1, w_in_b)
    if len(src) == 2:
        n_ctx_steps = src[0].shape[0] // tm
        widths = (D,) + widths
        in_specs = [pl.BlockSpec((tm, D), lambda i: (jnp.minimum(i, n_ctx_steps - 1), 0)),
                    pl.BlockSpec((tm, D), lambda i: (jnp.maximum(i - n_ctx_steps, 0), 0))] + common_specs
    else:
        n_ctx_steps = None
        in_specs = [tokspec] + common_specs
    outs = [jax.ShapeDtypeStruct((NTOK, w), F32) for w in widths]
    out_specs = [pl.BlockSpec((tm, w), row) for w in widths]
    return pl.pallas_call(
        functools.partial(_in_body, n_ctx_steps),
        grid=(n,), in_specs=in_specs, out_specs=out_specs, out_shape=outs,
        compiler_params=_cparams(("arbitrary",), VMEM_LIMIT),
        name="in_proj",
    )(*args)


def _halo_tile(ref, c0, c1, t0, t_len, static_single):
    xa = ref[pl.ds(t0, TILE), c0:c1]
    if static_single:
        z = jnp.zeros((8, c1 - c0), F32)
        return xa, jnp.concatenate([z, xa, z], axis=0)
    ps = pl.multiple_of(jnp.maximum(t0 - 8, 0), 8)
    ns = pl.multiple_of(jnp.minimum(t0 + TILE, t_len - 8), 8)
    prev = jnp.where(t0 > 0, ref[pl.ds(ps, 8), c0:c1], 0.0)
    nxt = jnp.where(t0 + TILE < t_len, ref[pl.ds(ns, 8), c0:c1], 0.0)
    return xa, jnp.concatenate([prev, xa, nxt], axis=0)


def _seq_spec(t_len, width, blk_off):
    idx = lambda b: (b + blk_off, 0)
    if t_len > TILE:
        return pl.BlockSpec((t_len, width), idx, pipeline_mode=pl.Buffered(1))
    return pl.BlockSpec((t_len, width), idx)


def _gelu_tanh(x):
    return 0.5 * x * (1.0 + jnp.tanh(0.7978845608028654 * (x + 0.044715 * (x * x * x))))


def _softplus(x):
    return jnp.maximum(x, 0.0) + jnp.log1p(jnp.exp(-jnp.abs(x)))


def _lru_scan(a, u, c, r8, rev):
    n = a.shape[0]
    for s in (1, 2, 4):
        if not rev:
            m = r8 >= s
            a_sh, u_sh = pltpu.roll(a, s, 0), pltpu.roll(u, s, 0)
        else:
            m = r8 < 8 - s
            a_sh, u_sh = pltpu.roll(a, n - s, 0), pltpu.roll(u, n - s, 0)
        u = jnp.where(m, a * u_sh + u, u)
        a = jnp.where(m, a * a_sh, a)
    ng = n // 8
    outs = [None] * ng
    for g in (range(ng) if not rev else reversed(range(ng))):
        hg = u[8 * g:8 * g + 8] + a[8 * g:8 * g + 8] * c
        c = hg[7:8] if not rev else hg[0:1]
        outs[g] = hg
    return jnp.concatenate(outs, axis=0), c


def _lru_body(t_len, pa_ref, cw_ref, cb_ref, wa_ref, ba_ref, wx_ref, bx_ref, lam_ref, h0_ref,
              oa_ref, hfin_ref, hf_s, ab_s, ub_s):
    nt = t_len // TILE
    single = nt == 1
    r8 = lax.broadcasted_iota(I32, (TILE, 1), 0) & 7
    n_ext = TILE + 16

    def gates(xc, xb, d):
        r = jax.nn.sigmoid(_dot(xb, wa_ref[d]) + ba_ref[d])
        i = jax.nn.sigmoid(_dot(xb, wx_ref[d]) + bx_ref[d])
        log_a = -LRU_C * r * _softplus(-lam_ref[d])
        a = jnp.exp(log_a)
        th = jnp.tanh(log_a)
        mult = jnp.sqrt(jnp.maximum(-2.0 * th / (1.0 - th), 0.0))
        return a, mult * (i * xc)

    def fwd_tile(i, c):
        t0 = pl.multiple_of(i * TILE, TILE)
        xa, ext = _halo_tile(pa_ref, 0, 256, t0, t_len, single)
        xc = cb_ref[...] + xa * cw_ref[1:2, :]
        for j in (0, 2, 3):
            xc = xc + pltpu.roll(ext, n_ext - 7 - j, 0)[0:TILE] * cw_ref[j:j + 1, :]
        xb = xc.astype(BF16)
        a_f, u_f = gates(xc, xb, 0)
        h, c = _lru_scan(a_f, u_f, c, r8, False)
        hf_s[pl.ds(t0, TILE), :] = h
        a_b, u_b = gates(xc, xb, 1)
        ab_s[pl.ds(t0, TILE), :] = a_b
        ub_s[pl.ds(t0, TILE), :] = u_b
        return c

    def bwd_tile(k, c):
        t0 = pl.multiple_of((nt - 1 - k) * TILE, TILE)
        h_b, c = _lru_scan(ab_s[pl.ds(t0, TILE), :], ub_s[pl.ds(t0, TILE), :], c, r8, True)
        gate = pa_ref[pl.ds(t0, TILE), 256:512]
        oa_ref[pl.ds(t0, TILE), :] = (hf_s[pl.ds(t0, TILE), :] + h_b) * _gelu_tanh(gate)
        return c

    h0 = h0_ref[0]
    if single:
        c_f = fwd_tile(0, h0[0:1])
        c_b = bwd_tile(0, h0[1:2])
    else:
        c_f = lax.fori_loop(0, nt, fwd_tile, h0[0:1])
        c_b = lax.fori_loop(0, nt, bwd_tile, h0[1:2])
    hfin_ref[0, 0:1, :] = c_f
    hfin_ref[0, 1:2, :] = c_b


def _lru_call(pa, t_len, nb, blk_off, layer, conv_w, conv_b, wa_bd, ba, wx_bd, bx, lam, h0):
    full2 = lambda shp: _lspec(shp, layer)
    if h0.ndim == 4:
        h0_spec = pl.BlockSpec((1, None, 2, 256), lambda b: (b, layer, 0, 0))
    else:
        h0_spec = pl.BlockSpec((1, 2, 256), lambda b: (b, 0, 0))
    return pl.pallas_call(
        functools.partial(_lru_body, t_len),
        grid=(nb,),
        in_specs=[_seq_spec(t_len, PA_W, blk_off),
                  full2((4, 256)), full2((1, 256)), full2((2, 256, 256)), full2((2, 1, 256)),
                  full2((2, 256, 256)), full2((2, 1, 256)), full2((2, 1, 256)),
                  h0_spec],
        out_specs=[pl.BlockSpec((t_len, 256), lambda b: (b, 0)),
                   pl.BlockSpec((1, 2, 256), lambda b: (b, 0, 0))],
        out_shape=[jax.ShapeDtypeStruct((nb * t_len, 256), F32), jax.ShapeDtypeStruct((nb, 2, 256), F32)],
        scratch_shapes=[pltpu.VMEM((t_len, 256), F32)] * 3,
        compiler_params=_cparams(("arbitrary",), VMEM_LIMIT),
        name="lru",
    )(pa, conv_w, conv_b, wa_bd, ba, wx_bd, bx, lam, h0)


def _pool_body(t_len, pb_ref, pw_ref, ps_ref, ob_ref):
    nt = t_len // TILE
    single = nt == 1
    n_ext = TILE + 16
    lane = lax.broadcasted_iota(I32, (1, 256), 1)
    rowi = lax.broadcasted_iota(I32, (TILE, 1), 0)

    def ahead(x, k):
        return pltpu.roll(x, n_ext - k, 0)

    def tile(i, carry):
        t0 = pl.multiple_of(i * TILE, TILE)
        xa, ext = _halo_tile(pb_ref, 0, 256, t0, t_len, single)
        p2 = ext + ahead(ext, 1)
        p4 = p2 + ahead(p2, 2)
        p8 = p4 + ahead(p4, 4)
        p16 = p8 + ahead(p8, 8)
        sums = (ahead(p2, 7)[0:TILE], ahead(p4, 6)[0:TILE], ahead(p8, 4)[0:TILE], p16[0:TILE])
        tpos = t0 + rowi
        means = []
        for w, s in zip((2, 4, 8, 16), sums):
            cnt = jnp.minimum(tpos + w // 2, t_len) - jnp.maximum(tpos - w // 2, 0)
            means.append(s / cnt.astype(F32))
        mean = jnp.where(lane < 64, means[0], jnp.where(lane < 128, means[1],
                                                          jnp.where(lane < 192, means[2], means[3])))
        pooled = mean - xa
        ob_ref[pl.ds(t0, TILE), :] = _dot(pooled.astype(BF16), pw_ref[...]) * ps_ref[...]
        return carry

    if single:
        tile(0, 0)
    else:
        lax.fori_loop(0, nt, tile, 0)


def _pool_call(pb, t_len, nb, blk_off, layer, pw_bd, pscale):
    return pl.pallas_call(
        functools.partial(_pool_body, t_len),
        grid=(nb,),
        in_specs=[_seq_spec(t_len, PB_W, blk_off), _lspec((256, 256), layer), _lspec((1, 256), layer)],
        out_specs=pl.BlockSpec((t_len, 256), lambda b: (b, 0)),
        out_shape=jax.ShapeDtypeStruct((nb * t_len, 256), F32),
        compiler_params=_cparams(("arbitrary",), VMEM_LIMIT),
        name="pool",
    )(pb, pw_bd, pscale)


HGRN_LEVELS = (1, 2, 4, 8, 16, 32, 64, 128)


def _hgrn_dir(rev, q, k, v, lf, st_s, att_s, ones_ref, pm_ref, rowi, r8, lane_head):
    c_rows = TILE
    half = TILE // 2
    g = _cumsum_rows(lf, r8, rev)
    vb = v.astype(BF16)
    seg = g
    head_on = [jnp.where(lane_head == h, 1.0, 0.0).astype(BF16) for h in range(4)]
    for li, m in enumerate(HGRN_LEVELS):
        up = (rowi & (2 * m - 1)) >= m
        if not rev:
            ref_q = pltpu.roll(seg, m, 0)
            qsel, ksel = up, jnp.logical_not(up)
            seg_next = jnp.where(up, seg, pltpu.roll(seg, c_rows - m, 0))
        else:
            ref_q = pltpu.roll(seg, c_rows - m, 0)
            qsel, ksel = jnp.logical_not(up), up
            seg_next = jnp.where(up, pltpu.roll(seg, m, 0), seg)
        qp = jnp.where(qsel, q * jnp.exp(g - ref_q), 0.0).astype(BF16)
        kp = jnp.where(ksel, k * jnp.exp(seg - g), 0.0).astype(BF16)

        def scores(qh, kh):
            qs = jnp.concatenate([qp[half * qh:half * (qh + 1)] * head_on[h] for h in range(4)], axis=0)
            return _dot_nt(qs, kp[half * kh:half * (kh + 1)])

        if 2 * m == c_rows:
            cross_q, cross_k = (1, 0) if not rev else (0, 1)
            cross = scores(cross_q, cross_k)
        else:
            for b in range(2):
                prod = scores(b, b)
                for h in range(4):
                    rows = slice(h * half, (h + 1) * half)
                    att_s[li, b, rows, :] = prod[rows] * pm_ref[li]
        seg = seg_next
    vm = [[vb[half * b:half * (b + 1)] * head_on[h] for h in range(4)] for b in range(2)]
    o_halves = []
    for b in range(2):
        ob = None
        for h in range(4):
            rows = slice(h * half, (h + 1) * half)
            att = att_s[0, b, rows, :]
            for li in range(1, len(HGRN_LEVELS) - 1):
                att = att + att_s[li, b, rows, :]
            term = _dot(att.astype(BF16), vm[b][h])
            if b == cross_q:
                term = term + _dot(cross[rows].astype(BF16), vm[cross_k][h])
            ob = term if ob is None else ob + term
        o_halves.append(ob)
    o = _dot((q * k).astype(BF16), ones_ref[...]) * v + jnp.concatenate(o_halves, axis=0)
    st = st_s[...]
    o = o + _dot_nt((q * jnp.exp(g)).astype(BF16), st.astype(BF16))
    g_end = g[c_rows - 1:c_rows] if not rev else g[0:1]
    kd = k * jnp.exp(g_end - g)
    upd = _dot_tn(vb, kd.astype(BF16))
    blk = (lax.broadcasted_iota(I32, (256, 1), 0) >> 6) == (lax.broadcasted_iota(I32, (1, 256), 1) >> 6)
    st_s[...] = st * jnp.exp(g_end) + jnp.where(blk, upd, 0.0)
    return o


def _hgrn_body(t_len, ctx_pass, pc_ref, lb_ref, nw_ref, s_ref, ones_ref, pm_ref, oc_ref, *rest):
    if ctx_pass:
        sfin_ref, of_s, st_s, att_s = rest
    else:
        of_s, st_s, att_s = rest
    nt = t_len // TILE
    rowi = lax.broadcasted_iota(I32, (TILE, 1), 0)
    r8 = rowi & 7
    lane_head = lax.broadcasted_iota(I32, (1, 256), 1) >> 6

    def load(t0, d):
        q = _silu(pc_ref[pl.ds(t0, TILE), 0:256]) * 0.125
        f_raw = pc_ref[pl.ds(t0, TILE), 256 * (1 + d):256 * (2 + d)]
        v = pc_ref[pl.ds(t0, TILE), 768:1024]
        lb = lb_ref[d]
        f_val = lb + (1.0 - lb) * jax.nn.sigmoid(f_raw)
        lf = jnp.log(jnp.maximum(f_val, TINY))
        return q, 1.0 - f_val, v, lf

    def fwd_tile(i, carry):
        t0 = pl.multiple_of(i * TILE, TILE)
        q, k, v, lf = load(t0, 0)
        of_s[pl.ds(t0, TILE), :] = _hgrn_dir(False, q, k, v, lf, st_s, att_s, ones_ref, pm_ref, rowi, r8, lane_head)
        return carry

    def bwd_tile(kk, carry):
        t0 = pl.multiple_of((nt - 1 - kk) * TILE, TILE)
        q, k, v, lf = load(t0, 1)
        o = of_s[pl.ds(t0, TILE), :] + _hgrn_dir(True, q, k, v, lf, st_s, att_s, ones_ref, pm_ref, rowi, r8, lane_head)
        ms = _segsum2(o * o, ones_ref[...]) * (1.0 / 64.0)
        y = o * lax.rsqrt(ms + EPS) * nw_ref[...]
        oc_ref[pl.ds(t0, TILE), :] = y * _silu(pc_ref[pl.ds(t0, TILE), 1024:1280])
        return carry

    def put_state(d):
        st = st_s[...]
        hi = st.astype(BF16)
        r1 = st - hi.astype(F32)
        mid = r1.astype(BF16)
        lo = (r1 - mid.astype(F32)).astype(BF16)
        sfin_ref[0, d] = _dot(hi, s_ref[...]) + _dot(mid, s_ref[...]) + _dot(lo, s_ref[...])

    for d, tile_fn in ((0, fwd_tile), (1, bwd_tile)):
        st_s[...] = jnp.zeros((256, 256), F32) if ctx_pass else s_ref[0, d]
        if nt == 1:
            tile_fn(0, 0)
        else:
            lax.fori_loop(0, nt, tile_fn, 0, unroll=2)
        if ctx_pass:
            put_state(d)


def _pair_masks():
    i = np.arange(TILE // 2)
    return np.stack([((i[:, None] // (2 * m)) == (i[None, :] // (2 * m))).astype(np.float32)
                     for m in HGRN_LEVELS[:-1]])


def _hgrn_call(pc, t_len, nb, blk_off, layer, lower, normw, s0t, ones64):
    n_lv = len(HGRN_LEVELS) - 1
    ctx_pass = s0t is None
    out_specs = [pl.BlockSpec((t_len, 256), lambda b: (b, 0))]
    out_shape = [jax.ShapeDtypeStruct((nb * t_len, 256), F32)]
    if ctx_pass:
        fold = np.zeros((256, 128), np.float32)
        fold[np.arange(256), np.arange(256) % 64] = 1.0
        s_arg, s_spec = jnp.asarray(fold, BF16), pl.BlockSpec((256, 128), lambda b: (0, 0))
        out_specs.append(pl.BlockSpec((1, 2, 256, 128), lambda b: (b, 0, 0, 0)))
        out_shape.append(jax.ShapeDtypeStruct((nb, 2, 256, 128), F32))
    else:
        s_arg, s_spec = s0t, pl.BlockSpec((1, None, 2, 256, 256), lambda b: (b, layer, 0, 0, 0))
    return pl.pallas_call(
        functools.partial(_hgrn_body, t_len, ctx_pass),
        grid=(nb,),
        in_specs=[_seq_spec(t_len, PC_W, blk_off),
                  _lspec((2, 1, 256), layer),
                  _lspec((1, 256), layer),
                  s_spec,
                  pl.BlockSpec((256, 256), lambda b: (0, 0)),
                  pl.BlockSpec((n_lv, TILE // 2, TILE // 2), lambda b: (0, 0, 0))],
        out_specs=out_specs,
        out_shape=out_shape,
        scratch_shapes=[pltpu.VMEM((t_len, 256), F32), pltpu.VMEM((256, 256), F32),
                        pltpu.VMEM((len(HGRN_LEVELS) - 1, 2, 2 * TILE, TILE // 2), F32)],
        compiler_params=_cparams(("arbitrary",), VMEM_LIMIT),
        name="hgrn",
    )(pc, lower, normw, s_arg, ones64, jnp.asarray(_pair_masks()))


ATT_SCALE = 96.0 ** -0.5
KEY_BLK = 512
EXP_SAFE = 40.0


def _rope512(x, cs128, sn128, lane128):
    cs = jnp.concatenate([cs128] * 4, axis=1)
    sn = jnp.concatenate([sn128] * 4, axis=1)
    partner = jnp.where(lane128 < 80, pltpu.roll(x, 512 - 16, 1), pltpu.roll(x, 16, 1))
    return x * cs + partner * sn


def _mla_body(t_len, n_ctx, use_rope, pd_ref, ckv_c_ref, kr_c_ref, cs_ref, sn_ref, qnorm_ref, wuq_ref, qnw_ref,
              kvnorm_ref, wukv_ref, knw_ref, ones_ref, od_ref, ckvn_ref, k_s, v_s, m_s, l_s, acc_s):
    nt = t_len // TILE
    t_k = n_ctx + t_len
    assert (t_k - TILE) % KEY_BLK == 0
    n_kb = (t_k - TILE) // KEY_BLK
    lane128 = lax.broadcasted_iota(I32, (1, 512), 1) & 127
    lane_head = lax.broadcasted_iota(I32, (1, 256), 1) >> 6

    def head_norm(x, w_ref):
        ss = _segsum2(x * x, ones_ref[...])
        return x * lax.rsqrt(ss * (1.0 / 96.0) + EPS) * w_ref[...]

    def put_kv(r0, ckv_n, kr128, rope_rows):
        kv = _dot(ckv_n.astype(BF16), wukv_ref[...])
        k_all = kv[:, 0:512] + jnp.concatenate([kr128] * 4, axis=1)
        kn = head_norm(k_all, knw_ref)
        if rope_rows is not None:
            kn = _rope512(kn, cs_ref[pl.ds(rope_rows, TILE), :], sn_ref[pl.ds(rope_rows, TILE), :], lane128)
        k_s[pl.ds(r0, TILE), :] = kn.astype(BF16)
        v_s[pl.ds(r0, TILE), :] = kv[:, 512:768].astype(BF16)

    if n_ctx:
        put_kv(0, ckv_c_ref[0], kr_c_ref[0], None)

    def kv_tile(i, carry):
        t0 = pl.multiple_of(i * TILE, TILE)
        ckv_n = _rms(pd_ref[pl.ds(t0, TILE), 256:384], kvnorm_ref[...])
        ckvn_ref[pl.ds(t0, TILE), :] = ckv_n
        put_kv(pl.multiple_of(n_ctx + t0, TILE), ckv_n, pd_ref[pl.ds(t0, TILE), 512:640],
               t0 if use_rope else None)
        return carry

    if nt == 1:
        kv_tile(0, 0)
    else:
        lax.fori_loop(0, nt, kv_tile, 0)

    def q_tile(bounded, i, carry):
        t0 = pl.multiple_of(i * TILE, TILE)
        qn = _rms(pd_ref[pl.ds(t0, TILE), 0:256], qnorm_ref[...])
        q = head_norm(_dot(qn.astype(BF16), wuq_ref[...]), qnw_ref)
        if use_rope:
            q = _rope512(q, cs_ref[pl.ds(t0, TILE), :], sn_ref[pl.ds(t0, TILE), :], lane128)
        qb = (q * ATT_SCALE).astype(BF16)
        qhs = [qb[:, 128 * h:128 * (h + 1)] for h in range(4)]

        def keys(h, rows):
            return k_s[rows, 128 * h:128 * (h + 1)]

        def vals(h, rows):
            return v_s[rows, :]

        first = slice(0, TILE)

        def first_block(h):
            s = _dot_nt(qhs[h], keys(h, first))
            m = jnp.max(s, axis=1, keepdims=True)
            p = jnp.exp(s - m)
            return m, jnp.sum(p, axis=1, keepdims=True), _dot(p.astype(BF16), vals(h, first))

        if n_kb == 0:
            outs = []
            for h in range(4):
                _, l, acc = first_block(h)
                outs.append(acc / l)
        elif bounded:
            for h in range(4):
                p = jnp.exp(_dot_nt(qhs[h], keys(h, first)))
                l_s[h] = jnp.broadcast_to(jnp.sum(p, axis=1, keepdims=True), (TILE, 128))
                acc_s[h] = _dot(p.astype(BF16), vals(h, first))

            def kblock_bounded(j, c2):
                rows = pl.ds(pl.multiple_of(TILE + j * KEY_BLK, TILE), KEY_BLK)
                for h in range(4):
                    p = jnp.exp(_dot_nt(qhs[h], keys(h, rows)))
                    l_s[h] = l_s[h] + jnp.sum(p, axis=1, keepdims=True)
                    acc_s[h] = acc_s[h] + _dot(p.astype(BF16), vals(h, rows))
                return c2

            lax.fori_loop(0, n_kb, kblock_bounded, 0, unroll=True)
            outs = [acc_s[h] / jnp.concatenate([l_s[h], l_s[h]], axis=1) for h in range(4)]
        else:
            for h in range(4):
                m, l, acc = first_block(h)
                m_s[h] = jnp.broadcast_to(m, (TILE, 128))
                l_s[h] = jnp.broadcast_to(l, (TILE, 128))
                acc_s[h] = acc

            def kblock(j, c2):
                rows = pl.ds(pl.multiple_of(TILE + j * KEY_BLK, TILE), KEY_BLK)
                for h in range(4):
                    s = _dot_nt(qhs[h], keys(h, rows))
                    m_prev = m_s[h]
                    m_new = jnp.maximum(m_prev, jnp.max(s, axis=1, keepdims=True))
                    alpha = jnp.exp(m_prev - m_new)
                    p = jnp.exp(s - jnp.concatenate([m_new] * (KEY_BLK // 128), axis=1))
                    l_s[h] = alpha * l_s[h] + jnp.sum(p, axis=1, keepdims=True)
                    acc_s[h] = (jnp.concatenate([alpha, alpha], axis=1) * acc_s[h]
                                + _dot(p.astype(BF16), vals(h, rows)))
                    m_s[h] = m_new
                return c2

            lax.fori_loop(0, n_kb, kblock, 0)
            outs = [acc_s[h] / jnp.concatenate([l_s[h], l_s[h]], axis=1) for h in range(4)]
        o = outs[3]
        for h in range(3):
            o = jnp.where(lane_head == h, outs[h], o)
        od_ref[pl.ds(t0, TILE), :] = o
        return carry

    if n_kb == 0:
        if nt == 1:
            q_tile(False, 0, 0)
        else:
            lax.fori_loop(0, nt, functools.partial(q_tile, False), 0)
    else:
        score_bound = (96.0 * ATT_SCALE) * jnp.max(jnp.abs(qnw_ref[...])) * jnp.max(jnp.abs(knw_ref[...]))
        safe = score_bound < EXP_SAFE

        @pl.when(safe)
        def _():
            lax.fori_loop(0, nt, functools.partial(q_tile, True), 0)

        @pl.when(jnp.logical_not(safe))
        def _():
            lax.fori_loop(0, nt, functools.partial(q_tile, False), 0)


def _mla_call(pd, t_len, nb, blk_off, layer, n_ctx, use_rope, ckv_c, kr_c, cs, sn, wts):
    qnorm, wuq, qnw, kvnorm, wukv, knw, ones128 = wts
    t_k = n_ctx + t_len
    c2 = lambda shp: pl.BlockSpec(shp, lambda b: (0,) * len(shp))
    lw = lambda shp: _lspec(shp, layer)
    if ckv_c.ndim == 4:
        ctx_spec = pl.BlockSpec((1, None, 256, 128), lambda b: (b, layer, 0, 0))
    else:
        ctx_spec = pl.BlockSpec((1, 256, 128), lambda b: (b, 0, 0))
    return pl.pallas_call(
        functools.partial(_mla_body, t_len, n_ctx, use_rope),
        grid=(nb,),
        in_specs=[_seq_spec(t_len, PD_W, blk_off), ctx_spec, ctx_spec,
                  c2((t_len, 128)), c2((t_len, 128)),
                  lw((1, 256)), lw((256, 512)), lw((1, 512)), lw((1, 128)), lw((128, 768)), lw((1, 512)),
                  c2((512, 512))],
        out_specs=[pl.BlockSpec((t_len, 256), lambda b: (b, 0)),
                   pl.BlockSpec((t_len, 128), lambda b: (b, 0))],
        out_shape=[jax.ShapeDtypeStruct((nb * t_len, 256), F32), jax.ShapeDtypeStruct((nb * t_len, 128), F32)],
        scratch_shapes=[pltpu.VMEM((t_k, 512), BF16), pltpu.VMEM((t_k, 256), BF16),
                        pltpu.VMEM((4, TILE, 128), F32), pltpu.VMEM((4, TILE, 128), F32),
                        pltpu.VMEM((4, TILE, 256), F32)],
        compiler_params=_cparams(("arbitrary",), VMEM_LIMIT),
        name="mla",
    )(pd, ckv_c, kr_c, cs, sn, qnorm, wuq, qnw, kvnorm, wukv, knw, ones128)


def _out_body(n_ctx_steps, *refs):
    mix_refs = refs[:8]
    (x_ref, g1_ref, sc_ref, sh_ref, nw_ref, wout_ref, wrh_ref, wrl_ref,
     x1_ref, h2e_ref, aff_ref) = refs[8:]
    is_ctx = pl.program_id(0) < n_ctx_steps
    lane = lax.broadcasted_iota(I32, (1, 128), 1)
    rc = 128
    for r0 in range(0, x_ref.shape[0], rc):
        rows = slice(r0, r0 + rc)
        m = None
        for k in range(4):
            ok = jnp.where(is_ctx, mix_refs[2 * k][rows, :], mix_refs[2 * k + 1][rows, :]).astype(BF16)
            mk = _dot(ok, wout_ref[256 * k:256 * (k + 1), :])
            m = mk if m is None else m + mk
        x1 = x_ref[rows, :] + g1_ref[0] * m
        x1_ref[rows, :] = x1
        h2 = _rms(x1, nw_ref[...]) * (1.0 + sc_ref[0]) + sh_ref[0]
        hh, hl = _split_bf16(h2)
        lg = _dot(hh, wrh_ref[...]) + _dot(hl, wrh_ref[...]) + _dot(hh, wrl_ref[...])
        lg = jnp.where(lane < N_EXP, lg, -jnp.inf)
        ex = jnp.exp(lg - jnp.max(lg, axis=1, keepdims=True))
        aff = ex / jnp.sum(ex, axis=1, keepdims=True)
        a_hi = aff.astype(BF16).astype(F32)
        a_lo = aff - a_hi
        ext = a_hi + pltpu.roll(a_lo, N_EXP, 1)
        h2e_ref[rows, 0:D] = hh
        h2e_ref[rows, D:HEXT] = ext.astype(BF16)
        aff_ref[rows, :] = aff


def _out_call(mix, x, layer, mod, norm2, wout_b, wr_h, wr_l):
    tm = 512
    n = NTOK // tm
    per_seg = SEG // tm
    n_ctx_steps = mix[0].shape[0] // tm
    row = lambda i: (i, 0)
    c2 = lambda shp: _lspec(shp, layer)
    ctx_spec = pl.BlockSpec((tm, 256), lambda i: (jnp.minimum(i, n_ctx_steps - 1), 0))
    lat_spec = pl.BlockSpec((tm, 256), lambda i: (jnp.maximum(i - n_ctx_steps, 0), 0))
    return pl.pallas_call(
        functools.partial(_out_body, n_ctx_steps),
        grid=(n,),
        in_specs=[ctx_spec, lat_spec] * 4 + [pl.BlockSpec((tm, D), row), _mod_spec(layer, 2, per_seg),
                  _mod_spec(layer, 4, per_seg), _mod_spec(layer, 3, per_seg),
                  c2((1, D)), c2((D, D)), c2((D, 128)), c2((D, 128))],
        out_specs=[pl.BlockSpec((tm, D), row), pl.BlockSpec((tm, HEXT), row), pl.BlockSpec((tm, 128), row)],
        out_shape=[jax.ShapeDtypeStruct((NTOK, D), F32), jax.ShapeDtypeStruct((NTOK, HEXT), BF16),
                   jax.ShapeDtypeStruct((NTOK, 128), F32)],
        compiler_params=_cparams(("arbitrary",), VMEM_LIMIT),
        name="out_proj",
    )(*mix, x, mod, mod, mod, norm2, wout_b, wr_h, wr_l)


def _sel_body(n_grp, cap, aff_ref, tri_ref, segt_ref, slot_ref, cum_ref, pref_s):
    w = SEG // n_grp
    nblk = SEG // 256
    aff = jnp.transpose(aff_ref[...])[0:N_EXP, :]
    pref_s[...] = jnp.zeros((N_EXP, SEG), I32)

    def grp_cols(fn):
        return jnp.concatenate([jnp.broadcast_to(fn(g), (N_EXP, w)) for g in range(n_grp)], axis=1)

    def it(i, carry):
        bit = lax.shift_left(jnp.int32(1), 30 - i)
        cand = pref_s[...] | bit
        ge = jnp.where(aff >= pltpu.bitcast(cand, F32), 1.0, 0.0)
        ok = grp_cols(lambda g: jnp.where(
            jnp.sum(ge[:, g * w:(g + 1) * w], axis=1, keepdims=True) >= cap, 1.0, 0.0))
        pref_s[...] = jnp.where(ok > 0.5, cand, pref_s[...])
        return carry

    lax.fori_loop(0, 31, it, 0)
    thr = pref_s[...]

    def grp_cumsum(x):
        outs, off = [], None
        for b in range(nblk):
            loc = _dot(x[:, 256 * b:256 * (b + 1)].astype(BF16), tri_ref[...])
            if (256 * b) % w == 0:
                off = None
            if off is not None:
                loc = loc + off
            off = loc[:, 255:256]
            outs.append(loc)
        return jnp.concatenate(outs, axis=1)

    gt = jnp.where(aff >= pltpu.bitcast(thr + 1, F32), 1.0, 0.0)
    eq = jnp.where(aff >= pltpu.bitcast(thr, F32), 1.0, 0.0) - gt
    room = grp_cols(lambda g: cap - jnp.sum(gt[:, g * w:(g + 1) * w], axis=1, keepdims=True))
    sel = jnp.where((gt > 0.5) | ((eq > 0.5) & (grp_cumsum(eq) <= room)), 1.0, 0.0)
    base = grp_cols(lambda g: jnp.full((N_EXP, 1), float(g * cap), F32))
    slot = base + grp_cumsum(sel) - 1.0
    slot_ref[0] = jnp.where(sel > 0.5, slot, -1.0).astype(I32)
    cum_ref[0] = _dot(sel.astype(BF16), segt_ref[...]).astype(I32)


def _sel_call(afft, n_seg, seg_off, n_grp, cap, tri, segt):
    c2 = lambda shp: pl.BlockSpec(shp, lambda s: (0,) * len(shp))
    return pl.pallas_call(
        functools.partial(_sel_body, n_grp, cap),
        grid=(n_seg,),
        in_specs=[pl.BlockSpec((SEG, 128), lambda s: (s + seg_off, 0)), c2((256, 256)), c2((SEG, 128))],
        out_specs=[pl.BlockSpec((1, N_EXP, SEG), lambda s: (s, 0, 0)),
                   pl.BlockSpec((1, N_EXP, 128), lambda s: (s, 0, 0))],
        out_shape=[jax.ShapeDtypeStruct((n_seg, N_EXP, SEG), I32), jax.ShapeDtypeStruct((n_seg, N_EXP, 128), I32)],
        scratch_shapes=[pltpu.VMEM((N_EXP, SEG), I32)],
        compiler_params=_cparams(("arbitrary",), VMEM_LIMIT),
        name="select",
    )(afft, tri, segt)


def _windows(cum_ref, s, e, tk):
    lo = cum_ref[s, e, tk]
    hi = cum_ref[s, e, tk + 1]
    w0 = lax.shift_left(lax.shift_right_logical(lo, 4), 4)
    nw = jnp.where(hi > lo, lax.shift_right_logical(hi - w0 + (WIN - 1), WIN_SHIFT), 0)
    return w0, nw


def _onehot(base, slot_row):
    rows = base + lax.broadcasted_iota(I32, (WIN, 1), 0)
    return jnp.where(rows == slot_row, 1.0, 0.0).astype(BF16)


GATHER_EXPERTS = 16


def _gather_body(cum_ref, slot_ref, h2e_ref, g_ref):
    s, half, tk = pl.program_id(0), pl.program_id(1), pl.program_id(2)

    @pl.when(tk == 0)
    def _():
        g_ref[...] = jnp.zeros(g_ref.shape, BF16)

    wins = [_windows(cum_ref, s, half * GATHER_EXPERTS + j, tk) for j in range(GATHER_EXPERTS)]
    bases = [pl.multiple_of(w0, 16) for w0, _ in wins]
    sel = jnp.concatenate([_onehot(bases[j], slot_ref[0, j:j + 1, :]) for j in range(GATHER_EXPERTS)], axis=0)
    got = _dot(sel, h2e_ref[...]).astype(BF16)
    for j in range(GATHER_EXPERTS):
        g_ref[0, j, pl.ds(bases[j], WIN), :] = (g_ref[0, j, pl.ds(bases[j], WIN), :]
                                                  + got[j * WIN:(j + 1) * WIN, :])
    for j in range(GATHER_EXPERTS):
        def wbody(w, carry, j=j):
            base = pl.multiple_of(wins[j][0] + w * WIN, 16)
            more = _dot(_onehot(base, slot_ref[0, j:j + 1, :]), h2e_ref[...])
            g_ref[0, j, pl.ds(base, WIN), :] = g_ref[0, j, pl.ds(base, WIN), :] + more.astype(BF16)
            return carry

        lax.fori_loop(1, wins[j][1], wbody, 0)


def _gather_call(cum, slot, h2e):
    n_half = N_EXP // GATHER_EXPERTS
    return pl.pallas_call(
        _gather_body,
        grid_spec=pltpu.PrefetchScalarGridSpec(
            num_scalar_prefetch=1, grid=(NSEG, n_half, SEG // TK),
            in_specs=[pl.BlockSpec((1, GATHER_EXPERTS, TK), lambda s, h, t, c: (s, h, t)),
                      pl.BlockSpec((TK, HEXT), lambda s, h, t, c: (s * (SEG // TK) + t, 0))],
            out_specs=pl.BlockSpec((1, GATHER_EXPERTS, SLOT_PAD, HEXT), lambda s, h, t, c: (s, h, 0, 0),
                                   pipeline_mode=pl.Buffered(1))),
        out_shape=jax.ShapeDtypeStruct((NSEG, N_EXP, SLOT_PAD, HEXT), BF16),
        compiler_params=_cparams(("arbitrary", "arbitrary", "arbitrary"), VMEM_LIMIT),
        name="moe_gather",
    )(cum, slot, h2e)


def _ffn_body(g_ref, wg_ref, wu_ref, wd_ref, y_ref, wgb_s, wub_s, wdb_s):
    e = pl.program_id(0)
    wgb_s[...] = wg_ref[0, 0].astype(BF16)
    wub_s[...] = wu_ref[0, 0].astype(BF16)
    wdb_s[...] = wd_ref[0, 0].astype(BF16)
    lane = lax.broadcasted_iota(I32, (1, 128), 1)
    pick = (lane == e) | (lane == e + N_EXP)
    for s in range(NSEG):
        xs = g_ref[s, 0, :, 0:D]
        ext = g_ref[s, 0, :, D:HEXT].astype(F32)
        gate = jnp.sum(jnp.where(pick, ext, 0.0), axis=1, keepdims=True)
        a = _dot(xs, wgb_s[...])
        u = _dot(xs, wub_s[...])
        y = _dot((_silu(a) * u).astype(BF16), wdb_s[...]) * gate
        y_ref[s, 0, 0:CAP_SEG, :] = y.astype(BF16)
        y_ref[s, 0, CAP_SEG:SLOT_PAD, :] = jnp.zeros((SLOT_PAD - CAP_SEG, D), BF16)


def _ffn_call(gath, layer, wg, wu, wd):
    wspec = pl.BlockSpec((1, 1, D, D), lambda e: (layer, e, 0, 0))
    return pl.pallas_call(
        _ffn_body,
        grid=(N_EXP,),
        in_specs=[pl.BlockSpec((NSEG, 1, CAP_SEG, HEXT), lambda e: (0, e, 0, 0)), wspec, wspec, wspec],
        out_specs=pl.BlockSpec((NSEG, 1, SLOT_PAD, D), lambda e: (0, e, 0, 0)),
        out_shape=jax.ShapeDtypeStruct((NSEG, N_EXP, SLOT_PAD, D), BF16),
        scratch_shapes=[pltpu.VMEM((D, D), BF16)] * 3,
        compiler_params=_cparams(("arbitrary",), VMEM_LIMIT),
        name="moe_ffn",
    )(gath, wg, wu, wd)


def _scatter_body(split, cum_ref, slot_ref, y_ref, x1_ref, g2_ref, *o_refs):
    s, tk = pl.program_id(0), pl.program_id(1)
    wins = [_windows(cum_ref, s, e, tk) for e in range(N_EXP)]
    bases = [pl.multiple_of(w0, 16) for w0, _ in wins]
    g2 = g2_ref[0]

    def run(o_ref):
        sel = jnp.concatenate([_onehot(bases[e], slot_ref[0, e:e + 1, :]) for e in range(N_EXP)], axis=0)
        rows = jnp.concatenate([y_ref[0, e, pl.ds(bases[e], WIN), :] for e in range(N_EXP)], axis=0)
        o_ref[...] = x1_ref[...] + g2 * _dot_tn(sel, rows)
        for e in range(N_EXP):
            def wbody(w, carry, e=e):
                base = pl.multiple_of(wins[e][0] + w * WIN, 16)
                o_ref[...] = o_ref[...] + g2 * _dot_tn(_onehot(base, slot_ref[0, e:e + 1, :]),
                                                       y_ref[0, e, pl.ds(base, WIN), :])
                return carry

            lax.fori_loop(1, wins[e][1], wbody, 0)

    if not split:
        run(o_refs[0])
    else:
        pl.when(s == 0)(lambda: run(o_refs[0]))
        pl.when(s > 0)(lambda: run(o_refs[1]))


def _scatter_call(cum, slot, y, x1, layer, mod, split):
    n_t = SEG // TK
    tok_idx = lambda s, t, c: (s * n_t + t, 0)
    if split:
        out_specs = [pl.BlockSpec((TK, D), lambda s, t, c: (jnp.where(s == 0, t, n_t - 1), 0)),
                     pl.BlockSpec((TK, D), lambda s, t, c: (jnp.maximum((s - 1) * n_t + t, 0), 0))]
        out_shape = [jax.ShapeDtypeStruct((SEG, D), F32), jax.ShapeDtypeStruct((NTOK - SEG, D), F32)]
    else:
        out_specs = pl.BlockSpec((TK, D), tok_idx)
        out_shape = jax.ShapeDtypeStruct((NTOK, D), F32)
    return pl.pallas_call(
        functools.partial(_scatter_body, split),
        grid_spec=pltpu.PrefetchScalarGridSpec(
            num_scalar_prefetch=1, grid=(NSEG, n_t),
            in_specs=[pl.BlockSpec((1, N_EXP, TK), lambda s, t, c: (s, 0, t)),
                      pl.BlockSpec((1, N_EXP, SLOT_PAD, D), lambda s, t, c: (s, 0, 0, 0),
                                   pipeline_mode=pl.Buffered(1)),
                      pl.BlockSpec((TK, D), tok_idx),
                      pl.BlockSpec((None, None, 1, 1, D), lambda s, t, c: (layer, 5, s, 0, 0))],
            out_specs=out_specs),
        out_shape=out_shape,
        compiler_params=_cparams(("arbitrary", "arbitrary"), VMEM_LIMIT),
        name="moe_scatter",
    )(cum, slot, y, x1, mod)


def _block_diag4(w):
    eye = jnp.eye(4, dtype=w.dtype)
    return jnp.einsum('...hij,hg->...higj', w, eye).reshape(w.shape[:-3] + (256, 256))


def _np_block_ones(n, blk):
    i = np.arange(n) // blk
    return (i[:, None] == i[None, :]).astype(np.float32)


def _head_cols():
    j = np.arange(128)
    src = np.full(128, -1)
    src[:64] = j[:64]
    src[64:80] = 64 + 2 * (j[64:80] - 64)
    src[80:96] = 64 + 2 * (j[80:96] - 80) + 1
    return src


def _mla_weights(mla_q_norm, mla_w_uq, mla_kv_norm, mla_w_ukv, mla_qn, mla_kn):
    depth = mla_w_uq.shape[0]
    src = _head_cols()
    valid = src >= 0
    srcc = np.where(valid, src, 0)
    colq = np.concatenate([h * 96 + srcc for h in range(4)])
    maskq = jnp.asarray(np.tile(valid, 4).astype(np.float32))
    wuq = (mla_w_uq[:, :, colq] * maskq).astype(BF16)
    qnw = (jnp.tile(mla_qn[:, srcc], (1, 4)) * maskq).reshape(depth, 1, 512)
    knw = (jnp.tile(mla_kn[:, srcc], (1, 4)) * maskq).reshape(depth, 1, 512)
    jn = np.arange(128)
    nope_valid = jn < 64
    colk = np.concatenate([h * 128 + np.where(nope_valid, jn, 0) for h in range(4)])
    maskk = jnp.asarray(np.tile(nope_valid, 4).astype(np.float32))
    colv = np.concatenate([h * 128 + 64 + np.arange(64) for h in range(4)])
    wukv = jnp.concatenate([mla_w_ukv[:, :, colk] * maskk, mla_w_ukv[:, :, colv]], axis=2).astype(BF16)
    return (mla_q_norm.reshape(depth, 1, 256), wuq, qnw, mla_kv_norm.reshape(depth, 1, 128), wukv, knw,
            jnp.asarray(_np_block_ones(512, 128), BF16))


def _krope128(kr):
    z64 = jnp.zeros(kr.shape[:-1] + (64,), kr.dtype)
    z32 = jnp.zeros(kr.shape[:-1] + (32,), kr.dtype)
    return jnp.concatenate([z64, kr[..., 0::2], kr[..., 1::2], z32], axis=-1)


def _rope_tables(n_tokens, grid_w):
    rows = (np.arange(n_tokens) // grid_w).astype(np.float32)
    cols = (np.arange(n_tokens) % grid_w).astype(np.float32)
    n_freq = 8
    inv = jnp.asarray(10000.0, F32) ** (-jnp.arange(n_freq, dtype=F32) / n_freq)
    ang = jnp.concatenate([jnp.asarray(rows)[:, None] * inv, jnp.asarray(cols)[:, None] * inv], axis=-1)
    cos, sin = jnp.cos(ang), jnp.sin(ang)
    one = jnp.ones((n_tokens, 64), F32)
    zero = jnp.zeros((n_tokens, 64), F32)
    cs = jnp.concatenate([one, cos, cos, one[:, :32]], axis=1)
    sn = jnp.concatenate([zero, -sin, sin, zero[:, :32]], axis=1)
    return cs, sn


def _in_weight(w_in):
    kr = w_in[..., 2432:2464]
    z96 = jnp.zeros(w_in.shape[:-1] + (96,), w_in.dtype)
    return jnp.concatenate([w_in, z96, _krope128(kr)], axis=-1).astype(BF16)


def kernel(x_prompt, x_sample, cache_mla_ckv, cache_mla_krope, state_rglru, state_hgrn, c, c_ctx, norm1_w, norm2_w, w_ada, b_ada, w_in, conv_w, conv_b, lru_wa, lru_ba, lru_wx, lru_bx, lru_lambda, pool_w, pool_scale, hgrn_lower_bounds, hgrn_norm_w, mla_q_norm, mla_w_uq, mla_kv_norm, mla_w_ukv, mla_qk_norm_q, mla_qk_norm_k, w_out, w_router, w_exp_gate, w_exp_up, w_exp_down):
    nbp, t_p = x_prompt.shape[0], x_prompt.shape[1]
    nbs, t_s = x_sample.shape[0], x_sample.shape[1]
    depth = w_in.shape[0]

    n_p = nbp * t_p
    cond8 = jnp.concatenate([c_ctx[None], c, jnp.zeros((5, D), F32)], axis=0)
    mod = _ada_call(cond8, w_ada, b_ada)
    mod = jnp.swapaxes(mod.reshape(depth, 8, 6, 1, D), 1, 2)

    f32 = lambda a: a.astype(F32)
    norm1 = norm1_w.reshape(depth, 1, D)
    norm2 = norm2_w.reshape(depth, 1, D)
    w_in_b = _in_weight(w_in)
    lru_w = (conv_w, conv_b.reshape(depth, 1, 256),
             _block_diag4(lru_wa).astype(BF16), lru_ba.reshape(depth, 2, 1, 256),
             _block_diag4(lru_wx).astype(BF16), lru_bx.reshape(depth, 2, 1, 256),
             lru_lambda.reshape(depth, 2, 1, 256))
    pw_bd = _block_diag4(pool_w).astype(BF16)
    pscale = pool_scale.reshape(depth, 1, 256)
    lb_soft = jax.nn.softmax(f32(hgrn_lower_bounds), axis=1)
    lower = jnp.swapaxes(jnp.cumsum(lb_soft, axis=1) - lb_soft[:, :1], 0, 1).reshape(depth, 2, 1, 256)
    hg_w = (lower, jnp.tile(hgrn_norm_w, (1, 4)).reshape(depth, 1, 256))
    eye4 = jnp.eye(4, dtype=F32)
    s0t = jnp.einsum('blzhdv,hg->blzhvgd', f32(state_hgrn), eye4).reshape(nbs, depth, 2, 256, 256)
    mw = _mla_weights(mla_q_norm, mla_w_uq, mla_kv_norm, mla_w_ukv, mla_qk_norm_q, mla_qk_norm_k)
    kr_ctx = _krope128(cache_mla_krope)
    wout_b = w_out.astype(BF16)
    wr_h, wr_l = _split_bf16(jnp.pad(w_router, ((0, 0), (0, 0), (0, 128 - N_EXP))))
    h0_lat = f32(state_rglru)

    ones64 = jnp.asarray(_np_block_ones(256, 64), BF16)
    tri = jnp.asarray(np.triu(np.ones((256, 256), np.float32)), BF16)
    segt = jnp.asarray((np.arange(SEG)[:, None] < TK * np.arange(128)[None, :]).astype(np.float32), BF16)
    cs_s, sn_s = _rope_tables(t_s, 64)
    cs_p, sn_p = jnp.ones((t_p, 128), F32), jnp.zeros((t_p, 128), F32)
    zero_ctx = jnp.zeros((nbp, 256, 128), F32)
    zero_h0 = jnp.zeros((nbp, 2, 256), F32)

    ckvs, krs, lru_states, hgrn_states = [], [], [], []
    src = (x_prompt.reshape(n_p, D), x_sample.reshape(nbs * t_s, D))
    for l in range(depth):
        res = _in_call(src, l, mod, norm1, w_in_b)
        x = res[0] if len(src) == 2 else src[0]
        pa, pb, pc, pd = res[-4:]

        oa_c, lru_fin = _lru_call(pa, t_p, nbp, 0, l, *lru_w, zero_h0)
        oa_l, _ = _lru_call(pa, t_s, nbs, 1, l, *lru_w, h0_lat)
        ob_c = _pool_call(pb, t_p, nbp, 0, l, pw_bd, pscale)
        ob_l = _pool_call(pb, t_s, nbs, 1, l, pw_bd, pscale)
        oc_c, st_p = _hgrn_call(pc, t_p, nbp, 0, l, *hg_w, None, ones64)
        (oc_l,) = _hgrn_call(pc, t_s, nbs, 1, l, *hg_w, s0t, ones64)
        od_c, ckvn = _mla_call(pd, t_p, nbp, 0, l, 0, False, zero_ctx, zero_ctx, cs_p, sn_p, mw)
        od_l, _ = _mla_call(pd, t_s, nbs, 1, l, 256, True, f32(cache_mla_ckv), kr_ctx, cs_s, sn_s, mw)

        x1, h2e, aff = _out_call((oa_c, oa_l, ob_c, ob_l, oc_c, oc_l, od_c, od_l), x, l, mod,
                                 norm2, wout_b, wr_h, wr_l)

        slot_p, cum_p = _sel_call(aff, 1, 0, nbp, 2 * t_p // N_EXP, tri, segt)
        slot_s, cum_s = _sel_call(aff, nbs, 1, 1, 2 * t_s // N_EXP, tri, segt)
        slot = jnp.concatenate([slot_p, slot_s], axis=0)
        cum = jnp.concatenate([cum_p, cum_s], axis=0)
        gath = _gather_call(cum, slot, h2e)
        y = _ffn_call(gath, l, w_exp_gate, w_exp_up, w_exp_down)
        src = _scatter_call(cum, slot, y, x1, l, mod, split=(l == depth - 1))
        src = tuple(src) if l == depth - 1 else (src,)

        ckvs.append(ckvn.reshape(nbp, t_p, 128))
        krs.append(pd[:n_p, 384:416].reshape(nbp, t_p, 32))
        lru_states.append(lru_fin)
        hgrn_states.append(jnp.swapaxes(st_p[..., :64].reshape(nbp, 2, 4, 64, 64), -1, -2))

    y_c, y_l = src
    return (y_c.reshape(nbp, t_p, D), y_l.reshape(nbs, t_s, D),
            jnp.stack(ckvs, axis=1), jnp.stack(krs, axis=1),
            jnp.stack(lru_states, axis=1), jnp.stack(hgrn_states, axis=1))
```

```python
import functools

import numpy as np
import jax
import jax.numpy as jnp
from jax import lax
from jax.experimental import pallas as pl
from jax.experimental.pallas import tpu as pltpu

F32 = jnp.float32
BF16 = jnp.bfloat16
I32 = jnp.int32

D = 1024
NTOK = 12288
SEG = 4096
NSEG = 3
EPS = 1e-6
TINY = 1e-30
LRU_C = 8.0
N_EXP = 16
CAP_SEG = 512
SLOT_PAD = 576
TILE = 256
TK = 256
WIN = 64
WIN_SHIFT = 6
HEXT = D + 128
VMEM_LIMIT = 56 * 1024 * 1024

PA_W, PB_W, PC_W, PD_W = 512, 256, 1280, 640
IN_PAD_W = PA_W + PB_W + PC_W + PD_W


def _cparams(sem, vmem=None):
    return pltpu.CompilerParams(dimension_semantics=sem, vmem_limit_bytes=vmem)


def _lspec(shape, layer):
    nd = len(shape)
    return pl.BlockSpec((None,) + tuple(shape), lambda *g: (layer,) + (0,) * nd)


def _mod_spec(layer, k, rows_per_seg):
    return pl.BlockSpec((None, None, 1, 1, D), lambda i, *_: (layer, k, i // rows_per_seg, 0, 0))


def _dot(a, b):
    return jnp.dot(a, b, preferred_element_type=F32)


def _dot_nt(a, b):
    return lax.dot_general(a, b, (((1,), (1,)), ((), ())), preferred_element_type=F32)


def _dot_tn(a, b):
    return lax.dot_general(a, b, (((0,), (0,)), ((), ())), preferred_element_type=F32)


def _rms(x, w):
    ms = jnp.mean(x * x, axis=-1, keepdims=True)
    return x * lax.rsqrt(ms + EPS) * w


def _silu(x):
    return x * jax.nn.sigmoid(x)


def _split_bf16(x):
    hi = x.astype(BF16)
    lo = (x - hi.astype(F32)).astype(BF16)
    return hi, lo


def _segsum2(x, ones_blk):
    hi, lo = _split_bf16(x)
    return _dot(hi, ones_blk) + _dot(lo, ones_blk)


def _cumsum_rows(x, r8, rev):
    n = x.shape[0]
    for s in (1, 2, 4):
        if not rev:
            x = jnp.where(r8 >= s, x + pltpu.roll(x, s, 0), x)
        else:
            x = jnp.where(r8 < 8 - s, x + pltpu.roll(x, n - s, 0), x)
    ng = n // 8
    outs = [None] * ng
    c = None
    for g in (range(ng) if not rev else reversed(range(ng))):
        xg = x[8 * g:8 * g + 8]
        if c is not None:
            xg = xg + c
        c = xg[7:8] if not rev else xg[0:1]
        outs[g] = xg
    return jnp.concatenate(outs, axis=0)


def _ada_body(c_ref, w_ref, b_ref, o_ref):
    s = _silu(c_ref[...])
    o_ref[0] = _dot(s.astype(BF16), w_ref[0].astype(BF16)) + b_ref[0]


def _ada_call(cond8, w_ada, b_ada):
    nj = 4
    wj = 6 * D // nj
    return pl.pallas_call(
        _ada_body,
        grid=(2, nj),
        in_specs=[pl.BlockSpec((8, D), lambda l, j: (0, 0)),
                  pl.BlockSpec((1, D, wj), lambda l, j: (l, 0, j)),
                  pl.BlockSpec((1, 1, wj), lambda l, j: (l, 0, j))],
        out_specs=pl.BlockSpec((1, 8, wj), lambda l, j: (l, 0, j)),
        out_shape=jax.ShapeDtypeStruct((2, 8, 6 * D), F32),
        compiler_params=_cparams(("arbitrary", "arbitrary"), VMEM_LIMIT),
        name="ada",
    )(cond8, w_ada, b_ada.reshape(2, 1, 6 * D))


def _in_body(n_ctx_steps, *refs):
    if n_ctx_steps is None:
        x_ref, sc_ref, sh_ref, nw_ref, w_ref, pa_ref, pb_ref, pc_ref, pd_ref = refs
        x = x_ref[...]
    else:
        xc_ref, xl_ref, sc_ref, sh_ref, nw_ref, w_ref, x_ref, pa_ref, pb_ref, pc_ref, pd_ref = refs
        x = jnp.where(pl.program_id(0) < n_ctx_steps, xc_ref[...], xl_ref[...])
        x_ref[...] = x
    h = _rms(x, nw_ref[...]) * (1.0 + sc_ref[0]) + sh_ref[0]
    hb = h.astype(BF16)
    o = 0
    for ref, w in ((pa_ref, PA_W), (pb_ref, PB_W), (pc_ref, PC_W), (pd_ref, PD_W)):
        ref[...] = _dot(hb, w_ref[:, o:o + w])
        o += w


def _in_call(src, layer, mod, norm1, w_in_b):
    tm = 512
    n = NTOK // tm
    per_seg = SEG // tm
    row = lambda i: (i, 0)
    tokspec = pl.BlockSpec((tm, D), row)
    widths = (PA_W, PB_W, PC_W, PD_W)
    common_specs = [_mod_spec(layer, 1, per_seg), _mod_spec(layer, 0, per_seg), _lspec((1, D), layer),
                    _lspec((D, IN_PAD_W), layer)]
    args = (*src, mod, mod, norm1, w_in_b)
    if len(src) == 2:
        n_ctx_steps = src[0].shape[0] // tm
        widths = (D,) + widths
        in_specs = [pl.BlockSpec((tm, D), lambda i: (jnp.minimum(i, n_ctx_steps - 1), 0)),
                    pl.BlockSpec((tm, D), lambda i: (jnp.maximum(i - n_ctx_steps, 0), 0))] + common_specs
    else:
        n_ctx_steps = None
        in_specs = [tokspec] + common_specs
    outs = [jax.ShapeDtypeStruct((NTOK, w), F32) for w in widths]
    out_specs = [pl.BlockSpec((tm, w), row) for w in widths]
    return pl.pallas_call(
        functools.partial(_in_body, n_ctx_steps),
        grid=(n,), in_specs=in_specs, out_specs=out_specs, out_shape=outs,
        compiler_params=_cparams(("arbitrary",), VMEM_LIMIT),
        name="in_proj",
    )(*args)


def _halo_tile(ref, c0, c1, t0, t_len, static_single):
    xa = ref[pl.ds(t0, TILE), c0:c1]
    if static_single:
        z = jnp.zeros((8, c1 - c0), F32)
        return xa, jnp.concatenate([z, xa, z], axis=0)
    ps = pl.multiple_of(jnp.maximum(t0 - 8, 0), 8)
    ns = pl.multiple_of(jnp.minimum(t0 + TILE, t_len - 8), 8)
    prev = jnp.where(t0 > 0, ref[pl.ds(ps, 8), c0:c1], 0.0)
    nxt = jnp.where(t0 + TILE < t_len, ref[pl.ds(ns, 8), c0:c1], 0.0)
    return xa, jnp.concatenate([prev, xa, nxt], axis=0)


def _seq_spec(t_len, width, blk_off):
    idx = lambda b: (b + blk_off, 0)
    if t_len > TILE:
        return pl.BlockSpec((t_len, width), idx, pipeline_mode=pl.Buffered(1))
    return pl.BlockSpec((t_len, width), idx)


def _gelu_tanh(x):
    return 0.5 * x * (1.0 + jnp.tanh(0.7978845608028654 * (x + 0.044715 * (x * x * x))))


def _softplus(x):
    return jnp.maximum(x, 0.0) + jnp.log1p(jnp.exp(-jnp.abs(x)))


def _lru_scan(a, u, c, r8, rev):
    n = a.shape[0]
    for s in (1, 2, 4):
        if not rev:
            m = r8 >= s
            a_sh, u_sh = pltpu.roll(a, s, 0), pltpu.roll(u, s, 0)
        else:
            m = r8 < 8 - s
            a_sh, u_sh = pltpu.roll(a, n - s, 0), pltpu.roll(u, n - s, 0)
        u = jnp.where(m, a * u_sh + u, u)
        a = jnp.where(m, a * a_sh, a)
    ng = n // 8
    outs = [None] * ng
    for g in (range(ng) if not rev else reversed(range(ng))):
        hg = u[8 * g:8 * g + 8] + a[8 * g:8 * g + 8] * c
        c = hg[7:8] if not rev else hg[0:1]
        outs[g] = hg
    return jnp.concatenate(outs, axis=0), c


def _lru_body(t_len, pa_ref, cw_ref, cb_ref, wa_ref, ba_ref, wx_ref, bx_ref, lam_ref, h0_ref,
              oa_ref, hfin_ref, hf_s, ab_s, ub_s):
    nt = t_len // TILE
    single = nt == 1
    r8 = lax.broadcasted_iota(I32, (TILE, 1), 0) & 7
    n_ext = TILE + 16

    def gates(xc, xb, d):
        r = jax.nn.sigmoid(_dot(xb, wa_ref[d]) + ba_ref[d])
        i = jax.nn.sigmoid(_dot(xb, wx_ref[d]) + bx_ref[d])
        log_a = -LRU_C * r * _softplus(-lam_ref[d])
        a = jnp.exp(log_a)
        th = jnp.tanh(log_a)
        mult = jnp.sqrt(jnp.maximum(-2.0 * th / (1.0 - th), 0.0))
        return a, mult * (i * xc)

    def fwd_tile(i, c):
        t0 = pl.multiple_of(i * TILE, TILE)
        xa, ext = _halo_tile(pa_ref, 0, 256, t0, t_len, single)
        xc = cb_ref[...] + xa * cw_ref[1:2, :]
        for j in (0, 2, 3):
            xc = xc + pltpu.roll(ext, n_ext - 7 - j, 0)[0:TILE] * cw_ref[j:j + 1, :]
        xb = xc.astype(BF16)
        a_f, u_f = gates(xc, xb, 0)
        h, c = _lru_scan(a_f, u_f, c, r8, False)
        hf_s[pl.ds(t0, TILE), :] = h
        a_b, u_b = gates(xc, xb, 1)
        ab_s[pl.ds(t0, TILE), :] = a_b
        ub_s[pl.ds(t0, TILE), :] = u_b
        return c

    def bwd_tile(k, c):
        t0 = pl.multiple_of((nt - 1 - k) * TILE, TILE)
        h_b, c = _lru_scan(ab_s[pl.ds(t0, TILE), :], ub_s[pl.ds(t0, TILE), :], c, r8, True)
        gate = pa_ref[pl.ds(t0, TILE), 256:512]
        oa_ref[pl.ds(t0, TILE), :] = (hf_s[pl.ds(t0, TILE), :] + h_b) * _gelu_tanh(gate)
        return c

    h0 = h0_ref[0]
    if single:
        c_f = fwd_tile(0, h0[0:1])
        c_b = bwd_tile(0, h0[1:2])
    else:
        c_f = lax.fori_loop(0, nt, fwd_tile, h0[0:1])
        c_b = lax.fori_loop(0, nt, bwd_tile, h0[1:2])
    hfin_ref[0, 0:1, :] = c_f
    hfin_ref[0, 1:2, :] = c_b


def _lru_call(pa, t_len, nb, blk_off, layer, conv_w, conv_b, wa_bd, ba, wx_bd, bx, lam, h0):
    full2 = lambda shp: _lspec(shp, layer)
    if h0.ndim == 4:
        h0_spec = pl.BlockSpec((1, None, 2, 256), lambda b: (b, layer, 0, 0))
    else:
        h0_spec = pl.BlockSpec((1, 2, 256), lambda b: (b, 0, 0))
    return pl.pallas_call(
        functools.partial(_lru_body, t_len),
        grid=(nb,),
        in_specs=[_seq_spec(t_len, PA_W, blk_off),
                  full2((4, 256)), full2((1, 256)), full2((2, 256, 256)), full2((2, 1, 256)),
                  full2((2, 256, 256)), full2((2, 1, 256)), full2((2, 1, 256)),
                  h0_spec],
        out_specs=[pl.BlockSpec((t_len, 256), lambda b: (b, 0)),
                   pl.BlockSpec((1, 2, 256), lambda b: (b, 0, 0))],
        out_shape=[jax.ShapeDtypeStruct((nb * t_len, 256), F32), jax.ShapeDtypeStruct((nb, 2, 256), F32)],
        scratch_shapes=[pltpu.VMEM((t_len, 256), F32)] * 3,
        compiler_params=_cparams(("arbitrary",), VMEM_LIMIT),
        name="lru",
    )(pa, conv_w, conv_b, wa_bd, ba, wx_bd, bx, lam, h0)


def _pool_body(t_len, pb_ref, pw_ref, ps_ref, ob_ref):
    nt = t_len // TILE
    single = nt == 1
    n_ext = TILE + 16
    lane = lax.broadcasted_iota(I32, (1, 256), 1)
    rowi = lax.broadcasted_iota(I32, (TILE, 1), 0)

    def ahead(x, k):
        return pltpu.roll(x, n_ext - k, 0)

    def tile(i, carry):
        t0 = pl.multiple_of(i * TILE, TILE)
        xa, ext = _halo_tile(pb_ref, 0, 256, t0, t_len, single)
        p2 = ext + ahead(ext, 1)
        p4 = p2 + ahead(p2, 2)
        p8 = p4 + ahead(p4, 4)
        p16 = p8 + ahead(p8, 8)
        sums = (ahead(p2, 7)[0:TILE], ahead(p4, 6)[0:TILE], ahead(p8, 4)[0:TILE], p16[0:TILE])
        tpos = t0 + rowi
        means = []
        for w, s in zip((2, 4, 8, 16), sums):
            cnt = jnp.minimum(tpos + w // 2, t_len) - jnp.maximum(tpos - w // 2, 0)
            means.append(s / cnt.astype(F32))
        mean = jnp.where(lane < 64, means[0], jnp.where(lane < 128, means[1],
                                                          jnp.where(lane < 192, means[2], means[3])))
        pooled = mean - xa
        ob_ref[pl.ds(t0, TILE), :] = _dot(pooled.astype(BF16), pw_ref[...]) * ps_ref[...]
        return carry

    if single:
        tile(0, 0)
    else:
        lax.fori_loop(0, nt, tile, 0)


def _pool_call(pb, t_len, nb, blk_off, layer, pw_bd, pscale):
    return pl.pallas_call(
        functools.partial(_pool_body, t_len),
        grid=(nb,),
        in_specs=[_seq_spec(t_len, PB_W, blk_off), _lspec((256, 256), layer), _lspec((1, 256), layer)],
        out_specs=pl.BlockSpec((t_len, 256), lambda b: (b, 0)),
        out_shape=jax.ShapeDtypeStruct((nb * t_len, 256), F32),
        compiler_params=_cparams(("arbitrary",), VMEM_LIMIT),
        name="pool",
    )(pb, pw_bd, pscale)


HGRN_LEVELS = (1, 2, 4, 8, 16, 32, 64, 128)


def _hgrn_dir(rev, q, k, v, lf, st_s, att_s, ones_ref, pm_ref, rowi, r8, lane_head):
    c_rows = TILE
    half = TILE // 2
    g = _cumsum_rows(lf, r8, rev)
    vb = v.astype(BF16)
    seg = g
    head_on = [jnp.where(lane_head == h, 1.0, 0.0).astype(BF16) for h in range(4)]
    for li, m in enumerate(HGRN_LEVELS):
        up = (rowi & (2 * m - 1)) >= m
        if not rev:
            ref_q = pltpu.roll(seg, m, 0)
            qsel, ksel = up, jnp.logical_not(up)
            seg_next = jnp.where(up, seg, pltpu.roll(seg, c_rows - m, 0))
        else:
            ref_q = pltpu.roll(seg, c_rows - m, 0)
            qsel, ksel = jnp.logical_not(up), up
            seg_next = jnp.where(up, pltpu.roll(seg, m, 0), seg)
        qp = jnp.where(qsel, q * jnp.exp(g - ref_q), 0.0).astype(BF16)
        kp = jnp.where(ksel, k * jnp.exp(seg - g), 0.0).astype(BF16)

        def scores(qh, kh):
            qs = jnp.concatenate([qp[half * qh:half * (qh + 1)] * head_on[h] for h in range(4)], axis=0)
            return _dot_nt(qs, kp[half * kh:half * (kh + 1)])

        if 2 * m == c_rows:
            cross_q, cross_k = (1, 0) if not rev else (0, 1)
            cross = scores(cross_q, cross_k)
        else:
            for b in range(2):
                prod = scores(b, b)
                for h in range(4):
                    rows = slice(h * half, (h + 1) * half)
                    att_s[li, b, rows, :] = prod[rows] * pm_ref[li]
        seg = seg_next
    vm = [[vb[half * b:half * (b + 1)] * head_on[h] for h in range(4)] for b in range(2)]
    o_halves = []
    for b in range(2):
        ob = None
        for h in range(4):
            rows = slice(h * half, (h + 1) * half)
            att = att_s[0, b, rows, :]
            for li in range(1, len(HGRN_LEVELS) - 1):
                att = att + att_s[li, b, rows, :]
            term = _dot(att.astype(BF16), vm[b][h])
            if b == cross_q:
                term = term + _dot(cross[rows].astype(BF16), vm[cross_k][h])
            ob = term if ob is None else ob + term
        o_halves.append(ob)
    o = _dot((q * k).astype(BF16), ones_ref[...]) * v + jnp.concatenate(o_halves, axis=0)
    st = st_s[...]
    o = o + _dot_nt((q * jnp.exp(g)).astype(BF16), st.astype(BF16))
    g_end = g[c_rows - 1:c_rows] if not rev else g[0:1]
    kd = k * jnp.exp(g_end - g)
    upd = _dot_tn(vb, kd.astype(BF16))
    blk = (lax.broadcasted_iota(I32, (256, 1), 0) >> 6) == (lax.broadcasted_iota(I32, (1, 256), 1) >> 6)
    st_s[...] = st * jnp.exp(g_end) + jnp.where(blk, upd, 0.0)
    return o


def _hgrn_body(t_len, ctx_pass, pc_ref, lb_ref, nw_ref, s_ref, ones_ref, pm_ref, oc_ref, *rest):
    if ctx_pass:
        sfin_ref, of_s, st_s, att_s = rest
    else:
        of_s, st_s, att_s = rest
    nt = t_len // TILE
    rowi = lax.broadcasted_iota(I32, (TILE, 1), 0)
    r8 = rowi & 7
    lane_head = lax.broadcasted_iota(I32, (1, 256), 1) >> 6

    def load(t0, d):
        q = _silu(pc_ref[pl.ds(t0, TILE), 0:256]) * 0.125
        f_raw = pc_ref[pl.ds(t0, TILE), 256 * (1 + d):256 * (2 + d)]
        v = pc_ref[pl.ds(t0, TILE), 768:1024]
        lb = lb_ref[d]
        f_val = lb + (1.0 - lb) * jax.nn.sigmoid(f_raw)
        lf = jnp.log(jnp.maximum(f_val, TINY))
        return q, 1.0 - f_val, v, lf

    def fwd_tile(i, carry):
        t0 = pl.multiple_of(i * TILE, TILE)
        q, k, v, lf = load(t0, 0)
        of_s[pl.ds(t0, TILE), :] = _hgrn_dir(False, q, k, v, lf, st_s, att_s, ones_ref, pm_ref, rowi, r8, lane_head)
        return carry

    def bwd_tile(kk, carry):
        t0 = pl.multiple_of((nt - 1 - kk) * TILE, TILE)
        q, k, v, lf = load(t0, 1)
        o = of_s[pl.ds(t0, TILE), :] + _hgrn_dir(True, q, k, v, lf, st_s, att_s, ones_ref, pm_ref, rowi, r8, lane_head)
        ms = _segsum2(o * o, ones_ref[...]) * (1.0 / 64.0)
        y = o * lax.rsqrt(ms + EPS) * nw_ref[...]
        oc_ref[pl.ds(t0, TILE), :] = y * _silu(pc_ref[pl.ds(t0, TILE), 1024:1280])
        return carry

    def put_state(d):
        st = st_s[...]
        hi = st.astype(BF16)
        r1 = st - hi.astype(F32)
        mid = r1.astype(BF16)
        lo = (r1 - mid.astype(F32)).astype(BF16)
        sfin_ref[0, d] = _dot(hi, s_ref[...]) + _dot(mid, s_ref[...]) + _dot(lo, s_ref[...])

    for d, tile_fn in ((0, fwd_tile), (1, bwd_tile)):
        st_s[...] = jnp.zeros((256, 256), F32) if ctx_pass else s_ref[0, d]
        if nt == 1:
            tile_fn(0, 0)
        else:
            lax.fori_loop(0, nt, tile_fn, 0, unroll=2)
        if ctx_pass:
            put_state(d)


def _pair_masks():
    i = np.arange(TILE // 2)
    return np.stack([((i[:, None] // (2 * m)) == (i[None, :] // (2 * m))).astype(np.float32)
                     for m in HGRN_LEVELS[:-1]])


def _hgrn_call(pc, t_len, nb, blk_off, layer, lower, normw, s0t, ones64):
    n_lv = len(HGRN_LEVELS) - 1
    ctx_pass = s0t is None
    out_specs = [pl.BlockSpec((t_len, 256), lambda b: (b, 0))]
    out_shape = [jax.ShapeDtypeStruct((nb * t_len, 256), F32)]
    if ctx_pass:
        fold = np.zeros((256, 128), np.float32)
        fold[np.arange(256), np.arange(256) % 64] = 1.0
        s_arg, s_spec = jnp.asarray(fold, BF16), pl.BlockSpec((256, 128), lambda b: (0, 0))
        out_specs.append(pl.BlockSpec((1, 2, 256, 128), lambda b: (b, 0, 0, 0)))
        out_shape.append(jax.ShapeDtypeStruct((nb, 2, 256, 128), F32))
    else:
        s_arg, s_spec = s0t, pl.BlockSpec((1, None, 2, 256, 256), lambda b: (b, layer, 0, 0, 0))
    return pl.pallas_call(
        functools.partial(_hgrn_body, t_len, ctx_pass),
        grid=(nb,),
        in_specs=[_seq_spec(t_len, PC_W, blk_off),
                  _lspec((2, 1, 256), layer),
                  _lspec((1, 256), layer),
                  s_spec,
                  pl.BlockSpec((256, 256), lambda b: (0, 0)),
                  pl.BlockSpec((n_lv, TILE // 2, TILE // 2), lambda b: (0, 0, 0))],
        out_specs=out_specs,
        out_shape=out_shape,
        scratch_shapes=[pltpu.VMEM((t_len, 256), F32), pltpu.VMEM((256, 256), F32),
                        pltpu.VMEM((len(HGRN_LEVELS) - 1, 2, 2 * TILE, TILE // 2), F32)],
        compiler_params=_cparams(("arbitrary",), VMEM_LIMIT),
        name="hgrn",
    )(pc, lower, normw, s_arg, ones64, jnp.asarray(_pair_masks()))


ATT_SCALE = 96.0 ** -0.5
KEY_BLK = 512
EXP_SAFE = 40.0


def _rope512(x, cs128, sn128, lane128):
    cs = jnp.concatenate([cs128] * 4, axis=1)
    sn = jnp.concatenate([sn128] * 4, axis=1)
    partner = jnp.where(lane128 < 80, pltpu.roll(x, 512 - 16, 1), pltpu.roll(x, 16, 1))
    return x * cs + partner * sn


def _mla_body(t_len, n_ctx, use_rope, pd_ref, ckv_c_ref, kr_c_ref, cs_ref, sn_ref, qnorm_ref, wuq_ref, qnw_ref,
              kvnorm_ref, wukv_ref, knw_ref, ones_ref, od_ref, ckvn_ref, k_s, v_s, m_s, l_s, acc_s):
    nt = t_len // TILE
    t_k = n_ctx + t_len
    assert (t_k - TILE) % KEY_BLK == 0
    n_kb = (t_k - TILE) // KEY_BLK
    lane128 = lax.broadcasted_iota(I32, (1, 512), 1) & 127
    lane_head = lax.broadcasted_iota(I32, (1, 256), 1) >> 6

    def head_norm(x, w_ref):
        ss = _segsum2(x * x, ones_ref[...])
        return x * lax.rsqrt(ss * (1.0 / 96.0) + EPS) * w_ref[...]

    def put_kv(r0, ckv_n, kr128, rope_rows):
        kv = _dot(ckv_n.astype(BF16), wukv_ref[...])
        k_all = kv[:, 0:512] + jnp.concatenate([kr128] * 4, axis=1)
        kn = head_norm(k_all, knw_ref)
        if rope_rows is not None:
            kn = _rope512(kn, cs_ref[pl.ds(rope_rows, TILE), :], sn_ref[pl.ds(rope_rows, TILE), :], lane128)
        k_s[pl.ds(r0, TILE), :] = kn.astype(BF16)
        v_s[pl.ds(r0, TILE), :] = kv[:, 512:768].astype(BF16)

    if n_ctx:
        put_kv(0, ckv_c_ref[0], kr_c_ref[0], None)

    def kv_tile(i, carry):
        t0 = pl.multiple_of(i * TILE, TILE)
        ckv_n = _rms(pd_ref[pl.ds(t0, TILE), 256:384], kvnorm_ref[...])
        ckvn_ref[pl.ds(t0, TILE), :] = ckv_n
        put_kv(pl.multiple_of(n_ctx + t0, TILE), ckv_n, pd_ref[pl.ds(t0, TILE), 512:640],
               t0 if use_rope else None)
        return carry

    if nt == 1:
        kv_tile(0, 0)
    else:
        lax.fori_loop(0, nt, kv_tile, 0)

    def q_tile(bounded, i, carry):
        t0 = pl.multiple_of(i * TILE, TILE)
        qn = _rms(pd_ref[pl.ds(t0, TILE), 0:256], qnorm_ref[...])
        q = head_norm(_dot(qn.astype(BF16), wuq_ref[...]), qnw_ref)
        if use_rope:
            q = _rope512(q, cs_ref[pl.ds(t0, TILE), :], sn_ref[pl.ds(t0, TILE), :], lane128)
        qb = (q * ATT_SCALE).astype(BF16)
        qhs = [qb[:, 128 * h:128 * (h + 1)] for h in range(4)]

        def keys(h, rows):
            return k_s[rows, 128 * h:128 * (h + 1)]

        def vals(h, rows):
            return v_s[rows, :]

        first = slice(0, TILE)

        def first_block(h):
            s = _dot_nt(qhs[h], keys(h, first))
            m = jnp.max(s, axis=1, keepdims=True)
            p = jnp.exp(s - m)
            return m, jnp.sum(p, axis=1, keepdims=True), _dot(p.astype(BF16), vals(h, first))

        if n_kb == 0:
            outs = []
            for h in range(4):
                _, l, acc = first_block(h)
                outs.append(acc / l)
        elif bounded:
            for h in range(4):
                p = jnp.exp(_dot_nt(qhs[h], keys(h, first)))
                l_s[h] = jnp.broadcast_to(jnp.sum(p, axis=1, keepdims=True), (TILE, 128))
                acc_s[h] = _dot(p.astype(BF16), vals(h, first))

            def kblock_bounded(j, c2):
                rows = pl.ds(pl.multiple_of(TILE + j * KEY_BLK, TILE), KEY_BLK)
                for h in range(4):
                    p = jnp.exp(_dot_nt(qhs[h], keys(h, rows)))
                    l_s[h] = l_s[h] + jnp.sum(p, axis=1, keepdims=True)
                    acc_s[h] = acc_s[h] + _dot(p.astype(BF16), vals(h, rows))
                return c2

            lax.fori_loop(0, n_kb, kblock_bounded, 0, unroll=True)
            outs = [acc_s[h] / jnp.concatenate([l_s[h], l_s[h]], axis=1) for h in range(4)]
        else:
            for h in range(4):
                m, l, acc = first_block(h)
                m_s[h] = jnp.broadcast_to(m, (TILE, 128))
                l_s[h] = jnp.broadcast_to(l, (TILE, 128))
                acc_s[h] = acc

            def kblock(j, c2):
                rows = pl.ds(pl.multiple_of(TILE + j * KEY_BLK, TILE), KEY_BLK)
                for h in range(4):
                    s = _dot_nt(qhs[h], keys(h, rows))
                    m_prev = m_s[h]
                    m_new = jnp.maximum(m_prev, jnp.max(s, axis=1, keepdims=True))
                    alpha = jnp.exp(m_prev - m_new)
                    p = jnp.exp(s - jnp.concatenate([m_new] * (KEY_BLK // 128), axis=1))
                    l_s[h] = alpha * l_s[h] + jnp.sum(p, axis=1, keepdims=True)
                    acc_s[h] = (jnp.concatenate([alpha, alpha], axis=1) * acc_s[h]
                                + _dot(p.astype(BF16), vals(h, rows)))
                    m_s[h] = m_new
                return c2

            lax.fori_loop(0, n_kb, kblock, 0)
            outs = [acc_s[h] / jnp.concatenate([l_s[h], l_s[h]], axis=1) for h in range(4)]
        o = outs[3]
        for h in range(3):
            o = jnp.where(lane_head == h, outs[h], o)
        od_ref[pl.ds(t0, TILE), :] = o
        return carry

    if n_kb == 0:
        if nt == 1:
            q_tile(False, 0, 0)
        else:
            lax.fori_loop(0, nt, functools.partial(q_tile, False), 0)
    else:
        score_bound = (96.0 * ATT_SCALE) * jnp.max(jnp.abs(qnw_ref[...])) * jnp.max(jnp.abs(knw_ref[...]))
        safe = score_bound < EXP_SAFE

        @pl.when(safe)
        def _():
            lax.fori_loop(0, nt, functools.partial(q_tile, True), 0)

        @pl.when(jnp.logical_not(safe))
        def _():
            lax.fori_loop(0, nt, functools.partial(q_tile, False), 0)


def _mla_call(pd, t_len, nb, blk_off, layer, n_ctx, use_rope, ckv_c, kr_c, cs, sn, wts):
    qnorm, wuq, qnw, kvnorm, wukv, knw, ones128 = wts
    t_k = n_ctx + t_len
    c2 = lambda shp: pl.BlockSpec(shp, lambda b: (0,) * len(shp))
    lw = lambda shp: _lspec(shp, layer)
    if ckv_c.ndim == 4:
        ctx_spec = pl.BlockSpec((1, None, 256, 128), lambda b: (b, layer, 0, 0))
    else:
        ctx_spec = pl.BlockSpec((1, 256, 128), lambda b: (b, 0, 0))
    return pl.pallas_call(
        functools.partial(_mla_body, t_len, n_ctx, use_rope),
        grid=(nb,),
        in_specs=[_seq_spec(t_len, PD_W, blk_off), ctx_spec, ctx_spec,
                  c2((t_len, 128)), c2((t_len, 128)),
                  lw((1, 256)), lw((256, 512)), lw((1, 512)), lw((1, 128)), lw((128, 768)), lw((1, 512)),
                  c2((512, 512))],
        out_specs=[pl.BlockSpec((t_len, 256), lambda b: (b, 0)),
                   pl.BlockSpec((t_len, 128), lambda b: (b, 0))],
        out_shape=[jax.ShapeDtypeStruct((nb * t_len, 256), F32), jax.ShapeDtypeStruct((nb * t_len, 128), F32)],
        scratch_shapes=[pltpu.VMEM((t_k, 512), BF16), pltpu.VMEM((t_k, 256), BF16),
                        pltpu.VMEM((4, TILE, 128), F32), pltpu.VMEM((4, TILE, 128), F32),
                        pltpu.VMEM((4, TILE, 256), F32)],
        compiler_params=_cparams(("arbitrary",), VMEM_LIMIT),
        name="mla",
    )(pd, ckv_c, kr_c, cs, sn, qnorm, wuq, qnw, kvnorm, wukv, knw, ones128)


def _out_body(n_ctx_steps, *refs):
    mix_refs = refs[:8]
    (x_ref, g1_ref, sc_ref, sh_ref, nw_ref, wout_ref, wrh_ref, wrl_ref,
     x1_ref, h2e_ref, aff_ref) = refs[8:]
    is_ctx = pl.program_id(0) < n_ctx_steps
    lane = lax.broadcasted_iota(I32, (1, 128), 1)
    rc = 256
    for r0 in range(0, x_ref.shape[0], rc):
        rows = slice(r0, r0 + rc)
        m = None
        for k in range(4):
            ok = jnp.where(is_ctx, mix_refs[2 * k][rows, :], mix_refs[2 * k + 1][rows, :]).astype(BF16)
            mk = _dot(ok, wout_ref[256 * k:256 * (k + 1), :])
            m = mk if m is None else m + mk
        x1 = x_ref[rows, :] + g1_ref[0] * m
        x1_ref[rows, :] = x1
        h2 = _rms(x1, nw_ref[...]) * (1.0 + sc_ref[0]) + sh_ref[0]
        hh, hl = _split_bf16(h2)
        lg = _dot(hh, wrh_ref[...]) + _dot(hl, wrh_ref[...]) + _dot(hh, wrl_ref[...])
        lg = jnp.where(lane < N_EXP, lg, -jnp.inf)
        ex = jnp.exp(lg - jnp.max(lg, axis=1, keepdims=True))
        aff = ex / jnp.sum(ex, axis=1, keepdims=True)
        a_hi = aff.astype(BF16).astype(F32)
        a_lo = aff - a_hi
        ext = a_hi + pltpu.roll(a_lo, N_EXP, 1)
        h2e_ref[rows, 0:D] = hh
        h2e_ref[rows, D:HEXT] = ext.astype(BF16)
        aff_ref[rows, :] = aff


def _out_call(mix, x, layer, mod, norm2, wout_b, wr_h, wr_l):
    tm = 512
    n = NTOK // tm
    per_seg = SEG // tm
    n_ctx_steps = mix[0].shape[0] // tm
    row = lambda i: (i, 0)
    c2 = lambda shp: _lspec(shp, layer)
    ctx_spec = pl.BlockSpec((tm, 256), lambda i: (jnp.minimum(i, n_ctx_steps - 1), 0))
    lat_spec = pl.BlockSpec((tm, 256), lambda i: (jnp.maximum(i - n_ctx_steps, 0), 0))
    return pl.pallas_call(
        functools.partial(_out_body, n_ctx_steps),
        grid=(n,),
        in_specs=[ctx_spec, lat_spec] * 4 + [pl.BlockSpec((tm, D), row), _mod_spec(layer, 2, per_seg),
                  _mod_spec(layer, 4, per_seg), _mod_spec(layer, 3, per_seg),
                  c2((1, D)), c2((D, D)), c2((D, 128)), c2((D, 128))],
        out_specs=[pl.BlockSpec((tm, D), row), pl.BlockSpec((tm, HEXT), row), pl.BlockSpec((tm, 128), row)],
        out_shape=[jax.ShapeDtypeStruct((NTOK, D), F32), jax.ShapeDtypeStruct((NTOK, HEXT), BF16),
                   jax.ShapeDtypeStruct((NTOK, 128), F32)],
        compiler_params=_cparams(("arbitrary",), VMEM_LIMIT),
        name="out_proj",
    )(*mix, x, mod, mod, mod, norm2, wout_b, wr_h, wr_l)


def _sel_body(n_grp, cap, aff_ref, tri_ref, segt_ref, slot_ref, cum_ref, pref_s):
    w = SEG // n_grp
    nblk = SEG // 256
    aff = jnp.transpose(aff_ref[...])[0:N_EXP, :]
    pref_s[...] = jnp.zeros((N_EXP, SEG), I32)

    def grp_cols(fn):
        return jnp.concatenate([jnp.broadcast_to(fn(g), (N_EXP, w)) for g in range(n_grp)], axis=1)

    def it(i, carry):
        bit = lax.shift_left(jnp.int32(1), 30 - i)
        cand = pref_s[...] | bit
        ge = jnp.where(aff >= pltpu.bitcast(cand, F32), 1.0, 0.0)
        ok = grp_cols(lambda g: jnp.where(
            jnp.sum(ge[:, g * w:(g + 1) * w], axis=1, keepdims=True) >= cap, 1.0, 0.0))
        pref_s[...] = jnp.where(ok > 0.5, cand, pref_s[...])
        return carry

    lax.fori_loop(0, 31, it, 0)
    thr = pref_s[...]

    def grp_cumsum(x):
        outs, off = [], None
        for b in range(nblk):
            loc = _dot(x[:, 256 * b:256 * (b + 1)].astype(BF16), tri_ref[...])
            if (256 * b) % w == 0:
                off = None
            if off is not None:
                loc = loc + off
            off = loc[:, 255:256]
            outs.append(loc)
        return jnp.concatenate(outs, axis=1)

    gt = jnp.where(aff >= pltpu.bitcast(thr + 1, F32), 1.0, 0.0)
    eq = jnp.where(aff >= pltpu.bitcast(thr, F32), 1.0, 0.0) - gt
    room = grp_cols(lambda g: cap - jnp.sum(gt[:, g * w:(g + 1) * w], axis=1, keepdims=True))
    sel = jnp.where((gt > 0.5) | ((eq > 0.5) & (grp_cumsum(eq) <= room)), 1.0, 0.0)
    base = grp_cols(lambda g: jnp.full((N_EXP, 1), float(g * cap), F32))
    slot = base + grp_cumsum(sel) - 1.0
    slot_ref[0] = jnp.where(sel > 0.5, slot, -1.0).astype(I32)
    cum_ref[0] = _dot(sel.astype(BF16), segt_ref[...]).astype(I32)


def _sel_call(afft, n_seg, seg_off, n_grp, cap, tri, segt):
    c2 = lambda shp: pl.BlockSpec(shp, lambda s: (0,) * len(shp))
    return pl.pallas_call(
        functools.partial(_sel_body, n_grp, cap),
        grid=(n_seg,),
        in_specs=[pl.BlockSpec((SEG, 128), lambda s: (s + seg_off, 0)), c2((256, 256)), c2((SEG, 128))],
        out_specs=[pl.BlockSpec((1, N_EXP, SEG), lambda s: (s, 0, 0)),
                   pl.BlockSpec((1, N_EXP, 128), lambda s: (s, 0, 0))],
        out_shape=[jax.ShapeDtypeStruct((n_seg, N_EXP, SEG), I32), jax.ShapeDtypeStruct((n_seg, N_EXP, 128), I32)],
        scratch_shapes=[pltpu.VMEM((N_EXP, SEG), I32)],
        compiler_params=_cparams(("arbitrary",), VMEM_LIMIT),
        name="select",
    )(afft, tri, segt)


def _windows(cum_ref, s, e, tk):
    lo = cum_ref[s, e, tk]
    hi = cum_ref[s, e, tk + 1]
    w0 = lax.shift_left(lax.shift_right_logical(lo, 4), 4)
    nw = jnp.where(hi > lo, lax.shift_right_logical(hi - w0 + (WIN - 1), WIN_SHIFT), 0)
    return w0, nw


def _onehot(base, slot_row):
    rows = base + lax.broadcasted_iota(I32, (WIN, 1), 0)
    return jnp.where(rows == slot_row, 1.0, 0.0).astype(BF16)


GATHER_EXPERTS = 16


def _gather_body(cum_ref, slot_ref, h2e_ref, g_ref):
    s, half, tk = pl.program_id(0), pl.program_id(1), pl.program_id(2)

    @pl.when(tk == 0)
    def _():
        g_ref[...] = jnp.zeros(g_ref.shape, BF16)

    wins = [_windows(cum_ref, s, half * GATHER_EXPERTS + j, tk) for j in range(GATHER_EXPERTS)]
    bases = [pl.multiple_of(w0, 16) for w0, _ in wins]
    sel = jnp.concatenate([_onehot(bases[j], slot_ref[0, j:j + 1, :]) for j in range(GATHER_EXPERTS)], axis=0)
    got = _dot(sel, h2e_ref[...]).astype(BF16)
    for j in range(GATHER_EXPERTS):
        g_ref[0, j, pl.ds(bases[j], WIN), :] = (g_ref[0, j, pl.ds(bases[j], WIN), :]
                                                  + got[j * WIN:(j + 1) * WIN, :])
    for j in range(GATHER_EXPERTS):
        def wbody(w, carry, j=j):
            base = pl.multiple_of(wins[j][0] + w * WIN, 16)
            more = _dot(_onehot(base, slot_ref[0, j:j + 1, :]), h2e_ref[...])
            g_ref[0, j, pl.ds(base, WIN), :] = g_ref[0, j, pl.ds(base, WIN), :] + more.astype(BF16)
            return carry

        lax.fori_loop(1, wins[j][1], wbody, 0)


def _gather_call(cum, slot, h2e):
    n_half = N_EXP // GATHER_EXPERTS
    return pl.pallas_call(
        _gather_body,
        grid_spec=pltpu.PrefetchScalarGridSpec(
            num_scalar_prefetch=1, grid=(NSEG, n_half, SEG // TK),
            in_specs=[pl.BlockSpec((1, GATHER_EXPERTS, TK), lambda s, h, t, c: (s, h, t)),
                      pl.BlockSpec((TK, HEXT), lambda s, h, t, c: (s * (SEG // TK) + t, 0))],
            out_specs=pl.BlockSpec((1, GATHER_EXPERTS, SLOT_PAD, HEXT), lambda s, h, t, c: (s, h, 0, 0),
                                   pipeline_mode=pl.Buffered(1))),
        out_shape=jax.ShapeDtypeStruct((NSEG, N_EXP, SLOT_PAD, HEXT), BF16),
        compiler_params=_cparams(("arbitrary", "arbitrary", "arbitrary"), VMEM_LIMIT),
        name="moe_gather",
    )(cum, slot, h2e)


def _ffn_body(g_ref, wg_ref, wu_ref, wd_ref, y_ref, wgb_s, wub_s, wdb_s):
    e = pl.program_id(0)
    wgb_s[...] = wg_ref[0, 0].astype(BF16)
    wub_s[...] = wu_ref[0, 0].astype(BF16)
    wdb_s[...] = wd_ref[0, 0].astype(BF16)
    lane = lax.broadcasted_iota(I32, (1, 128), 1)
    pick = (lane == e) | (lane == e + N_EXP)
    for s in range(NSEG):
        xs = g_ref[s, 0, :, 0:D]
        ext = g_ref[s, 0, :, D:HEXT].astype(F32)
        gate = jnp.sum(jnp.where(pick, ext, 0.0), axis=1, keepdims=True)
        a = _dot(xs, wgb_s[...])
        u = _dot(xs, wub_s[...])
        y = _dot((_silu(a) * u).astype(BF16), wdb_s[...]) * gate
        y_ref[s, 0, 0:CAP_SEG, :] = y.astype(BF16)
        y_ref[s, 0, CAP_SEG:SLOT_PAD, :] = jnp.zeros((SLOT_PAD - CAP_SEG, D), BF16)


def _ffn_call(gath, layer, wg, wu, wd):
    wspec = pl.BlockSpec((1, 1, D, D), lambda e: (layer, e, 0, 0))
    return pl.pallas_call(
        _ffn_body,
        grid=(N_EXP,),
        in_specs=[pl.BlockSpec((NSEG, 1, CAP_SEG, HEXT), lambda e: (0, e, 0, 0)), wspec, wspec, wspec],
        out_specs=pl.BlockSpec((NSEG, 1, SLOT_PAD, D), lambda e: (0, e, 0, 0)),
        out_shape=jax.ShapeDtypeStruct((NSEG, N_EXP, SLOT_PAD, D), BF16),
        scratch_shapes=[pltpu.VMEM((D, D), BF16)] * 3,
        compiler_params=_cparams(("arbitrary",), VMEM_LIMIT),
        name="moe_ffn",
    )(gath, wg, wu, wd)


def _scatter_body(split, cum_ref, slot_ref, y_ref, x1_ref, g2_ref, *o_refs):
    s, tk = pl.program_id(0), pl.program_id(1)
    wins = [_windows(cum_ref, s, e, tk) for e in range(N_EXP)]
    bases = [pl.multiple_of(w0, 16) for w0, _ in wins]
    g2 = g2_ref[0]

    def run(o_ref):
        sel = jnp.concatenate([_onehot(bases[e], slot_ref[0, e:e + 1, :]) for e in range(N_EXP)], axis=0)
        rows = jnp.concatenate([y_ref[0, e, pl.ds(bases[e], WIN), :] for e in range(N_EXP)], axis=0)
        o_ref[...] = x1_ref[...] + g2 * _dot_tn(sel, rows)
        for e in range(N_EXP):
            def wbody(w, carry, e=e):
                base = pl.multiple_of(wins[e][0] + w * WIN, 16)
                o_ref[...] = o_ref[...] + g2 * _dot_tn(_onehot(base, slot_ref[0, e:e + 1, :]),
                                                       y_ref[0, e, pl.ds(base, WIN), :])
                return carry

            lax.fori_loop(1, wins[e][1], wbody, 0)

    if not split:
        run(o_refs[0])
    else:
        pl.when(s == 0)(lambda: run(o_refs[0]))
        pl.when(s > 0)(lambda: run(o_refs[1]))


def _scatter_call(cum, slot, y, x1, layer, mod, split):
    n_t = SEG // TK
    tok_idx = lambda s, t, c: (s * n_t + t, 0)
    if split:
        out_specs = [pl.BlockSpec((TK, D), lambda s, t, c: (jnp.where(s == 0, t, n_t - 1), 0)),
                     pl.BlockSpec((TK, D), lambda s, t, c: (jnp.maximum((s - 1) * n_t + t, 0), 0))]
        out_shape = [jax.ShapeDtypeStruct((SEG, D), F32), jax.ShapeDtypeStruct((NTOK - SEG, D), F32)]
    else:
        out_specs = pl.BlockSpec((TK, D), tok_idx)
        out_shape = jax.ShapeDtypeStruct((NTOK, D), F32)
    return pl.pallas_call(
        functools.partial(_scatter_body, split),
        grid_spec=pltpu.PrefetchScalarGridSpec(
            num_scalar_prefetch=1, grid=(NSEG, n_t),
            in_specs=[pl.BlockSpec((1, N_EXP, TK), lambda s, t, c: (s, 0, t)),
                      pl.BlockSpec((1, N_EXP, SLOT_PAD, D), lambda s, t, c: (s, 0, 0, 0)),
                      pl.BlockSpec((TK, D), tok_idx),
                      pl.BlockSpec((None, None, 1, 1, D), lambda s, t, c: (layer, 5, s, 0, 0))],
            out_specs=out_specs),
        out_shape=out_shape,
        compiler_params=_cparams(("arbitrary", "arbitrary"), VMEM_LIMIT),
        name="moe_scatter",
    )(cum, slot, y, x1, mod)


def _block_diag4(w):
    eye = jnp.eye(4, dtype=w.dtype)
    return jnp.einsum('...hij,hg->...higj', w, eye).reshape(w.shape[:-3] + (256, 256))


def _np_block_ones(n, blk):
    i = np.arange(n) // blk
    return (i[:, None] == i[None, :]).astype(np.float32)


def _head_cols():
    j = np.arange(128)
    src = np.full(128, -1)
    src[:64] = j[:64]
    src[64:80] = 64 + 2 * (j[64:80] - 64)
    src[80:96] = 64 + 2 * (j[80:96] - 80) + 1
    return src


def _mla_weights(mla_q_norm, mla_w_uq, mla_kv_norm, mla_w_ukv, mla_qn, mla_kn):
    depth = mla_w_uq.shape[0]
    src = _head_cols()
    valid = src >= 0
    srcc = np.where(valid, src, 0)
    colq = np.concatenate([h * 96 + srcc for h in range(4)])
    maskq = jnp.asarray(np.tile(valid, 4).astype(np.float32))
    wuq = (mla_w_uq[:, :, colq] * maskq).astype(BF16)
    qnw = (jnp.tile(mla_qn[:, srcc], (1, 4)) * maskq).reshape(depth, 1, 512)
    knw = (jnp.tile(mla_kn[:, srcc], (1, 4)) * maskq).reshape(depth, 1, 512)
    jn = np.arange(128)
    nope_valid = jn < 64
    colk = np.concatenate([h * 128 + np.where(nope_valid, jn, 0) for h in range(4)])
    maskk = jnp.asarray(np.tile(nope_valid, 4).astype(np.float32))
    colv = np.concatenate([h * 128 + 64 + np.arange(64) for h in range(4)])
    wukv = jnp.concatenate([mla_w_ukv[:, :, colk] * maskk, mla_w_ukv[:, :, colv]], axis=2).astype(BF16)
    return (mla_q_norm.reshape(depth, 1, 256), wuq, qnw, mla_kv_norm.reshape(depth, 1, 128), wukv, knw,
            jnp.asarray(_np_block_ones(512, 128), BF16))


def _krope128(kr):
    z64 = jnp.zeros(kr.shape[:-1] + (64,), kr.dtype)
    z32 = jnp.zeros(kr.shape[:-1] + (32,), kr.dtype)
    return jnp.concatenate([z64, kr[..., 0::2], kr[..., 1::2], z32], axis=-1)


def _rope_tables(n_tokens, grid_w):
    rows = (np.arange(n_tokens) // grid_w).astype(np.float32)
    cols = (np.arange(n_tokens) % grid_w).astype(np.float32)
    n_freq = 8
    inv = jnp.asarray(10000.0, F32) ** (-jnp.arange(n_freq, dtype=F32) / n_freq)
    ang = jnp.concatenate([jnp.asarray(rows)[:, None] * inv, jnp.asarray(cols)[:, None] * inv], axis=-1)
    cos, sin = jnp.cos(ang), jnp.sin(ang)
    one = jnp.ones((n_tokens, 64), F32)
    zero = jnp.zeros((n_tokens, 64), F32)
    cs = jnp.concatenate([one, cos, cos, one[:, :32]], axis=1)
    sn = jnp.concatenate([zero, -sin, sin, zero[:, :32]], axis=1)
    return cs, sn


def _in_weight(w_in):
    kr = w_in[..., 2432:2464]
    z96 = jnp.zeros(w_in.shape[:-1] + (96,), w_in.dtype)
    return jnp.concatenate([w_in, z96, _krope128(kr)], axis=-1).astype(BF16)


def kernel(x_prompt, x_sample, cache_mla_ckv, cache_mla_krope, state_rglru, state_hgrn, c, c_ctx, norm1_w, norm2_w, w_ada, b_ada, w_in, conv_w, conv_b, lru_wa, lru_ba, lru_wx, lru_bx, lru_lambda, pool_w, pool_scale, hgrn_lower_bounds, hgrn_norm_w, mla_q_norm, mla_w_uq, mla_kv_norm, mla_w_ukv, mla_qk_norm_q, mla_qk_norm_k, w_out, w_router, w_exp_gate, w_exp_up, w_exp_down):
    nbp, t_p = x_prompt.shape[0], x_prompt.shape[1]
    nbs, t_s = x_sample.shape[0], x_sample.shape[1]
    depth = w_in.shape[0]

    n_p = nbp * t_p
    cond8 = jnp.concatenate([c_ctx[None], c, jnp.zeros((5, D), F32)], axis=0)
    mod = _ada_call(cond8, w_ada, b_ada)
    mod = jnp.swapaxes(mod.reshape(depth, 8, 6, 1, D), 1, 2)

    f32 = lambda a: a.astype(F32)
    norm1 = norm1_w.reshape(depth, 1, D)
    norm2 = norm2_w.reshape(depth, 1, D)
    w_in_b = _in_weight(w_in)
    lru_w = (conv_w, conv_b.reshape(depth, 1, 256),
             _block_diag4(lru_wa).astype(BF16), lru_ba.reshape(depth, 2, 1, 256),
             _block_diag4(lru_wx).astype(BF16), lru_bx.reshape(depth, 2, 1, 256),
             lru_lambda.reshape(depth, 2, 1, 256))
    pw_bd = _block_diag4(pool_w).astype(BF16)
    pscale = pool_scale.reshape(depth, 1, 256)
    lb_soft = jax.nn.softmax(f32(hgrn_lower_bounds), axis=1)
    lower = jnp.swapaxes(jnp.cumsum(lb_soft, axis=1) - lb_soft[:, :1], 0, 1).reshape(depth, 2, 1, 256)
    hg_w = (lower, jnp.tile(hgrn_norm_w, (1, 4)).reshape(depth, 1, 256))
    eye4 = jnp.eye(4, dtype=F32)
    s0t = jnp.einsum('blzhdv,hg->blzhvgd', f32(state_hgrn), eye4).reshape(nbs, depth, 2, 256, 256)
    mw = _mla_weights(mla_q_norm, mla_w_uq, mla_kv_norm, mla_w_ukv, mla_qk_norm_q, mla_qk_norm_k)
    kr_ctx = _krope128(cache_mla_krope)
    wout_b = w_out.astype(BF16)
    wr_h, wr_l = _split_bf16(jnp.pad(w_router, ((0, 0), (0, 0), (0, 128 - N_EXP))))
    h0_lat = f32(state_rglru)

    ones64 = jnp.asarray(_np_block_ones(256, 64), BF16)
    tri = jnp.asarray(np.triu(np.ones((256, 256), np.float32)), BF16)
    segt = jnp.asarray((np.arange(SEG)[:, None] < TK * np.arange(128)[None, :]).astype(np.float32), BF16)
    cs_s, sn_s = _rope_tables(t_s, 64)
    cs_p, sn_p = jnp.ones((t_p, 128), F32), jnp.zeros((t_p, 128), F32)
    zero_ctx = jnp.zeros((nbp, 256, 128), F32)
    zero_h0 = jnp.zeros((nbp, 2, 256), F32)

    ckvs, krs, lru_states, hgrn_states = [], [], [], []
    src = (x_prompt.reshape(n_p, D), x_sample.reshape(nbs * t_s, D))
    for l in range(depth):
        res = _in_call(src, l, mod, norm1, w_in_b)
        x = res[0] if len(src) == 2 else src[0]
        pa, pb, pc, pd = res[-4:]

        oa_c, lru_fin = _lru_call(pa, t_p, nbp, 0, l, *lru_w, zero_h0)
        oa_l, _ = _lru_call(pa, t_s, nbs, 1, l, *lru_w, h0_lat)
        ob_c = _pool_call(pb, t_p, nbp, 0, l, pw_bd, pscale)
        ob_l = _pool_call(pb, t_s, nbs, 1, l, pw_bd, pscale)
        oc_c, st_p = _hgrn_call(pc, t_p, nbp, 0, l, *hg_w, None, ones64)
        (oc_l,) = _hgrn_call(pc, t_s, nbs, 1, l, *hg_w, s0t, ones64)
        od_c, ckvn = _mla_call(pd, t_p, nbp, 0, l, 0, False, zero_ctx, zero_ctx, cs_p, sn_p, mw)
        od_l, _ = _mla_call(pd, t_s, nbs, 1, l, 256, True, f32(cache_mla_ckv), kr_ctx, cs_s, sn_s, mw)

        x1, h2e, aff = _out_call((oa_c, oa_l, ob_c, ob_l, oc_c, oc_l, od_c, od_l), x, l, mod,
                                 norm2, wout_b, wr_h, wr_l)

        slot_p, cum_p = _sel_call(aff, 1, 0, nbp, 2 * t_p // N_EXP, tri, segt)
        slot_s, cum_s = _sel_call(aff, nbs, 1, 1, 2 * t_s // N_EXP, tri, segt)
        slot = jnp.concatenate([slot_p, slot_s], axis=0)
        cum = jnp.concatenate([cum_p, cum_s], axis=0)
        gath = _gather_call(cum, slot, h2e)
        y = _ffn_call(gath, l, w_exp_gate, w_exp_up, w_exp_down)
        src = _scatter_call(cum, slot, y, x1, l, mod, split=(l == depth - 1))
        src = tuple(src) if l == depth - 1 else (src,)

        ckvs.append(ckvn.reshape(nbp, t_p, 128))
        krs.append(pd[:n_p, 384:416].reshape(nbp, t_p, 32))
        lru_states.append(lru_fin)
        hgrn_states.append(jnp.swapaxes(st_p[..., :64].reshape(nbp, 2, 4, 64, 64), -1, -2))

    y_c, y_l = src
    return (y_c.reshape(nbp, t_p, D), y_l.reshape(nbs, t_s, D),
            jnp.stack(ckvs, axis=1), jnp.stack(krs, axis=1),
            jnp.stack(lru_states, axis=1), jnp.stack(hgrn_states, axis=1))
```

```python
import functools

import numpy as np
import jax
import jax.numpy as jnp
from jax import lax
from jax.experimental import pallas as pl
from jax.experimental.pallas import tpu as pltpu

F32 = jnp.float32
BF16 = jnp.bfloat16
I32 = jnp.int32

D = 1024
NTOK = 12288
SEG = 4096
NSEG = 3
EPS = 1e-6
TINY = 1e-30
LRU_C = 8.0
N_EXP = 16
CAP_SEG = 512
SLOT_PAD = 576
TILE = 256
TK = 256
WIN = 64
WIN_SHIFT = 6
HEXT = D + 128
VMEM_LIMIT = 56 * 1024 * 1024

PA_W, PB_W, PC_W, PD_W = 512, 256, 1280, 640
IN_PAD_W = PA_W + PB_W + PC_W + PD_W


def _cparams(sem, vmem=None):
    return pltpu.CompilerParams(dimension_semantics=sem, vmem_limit_bytes=vmem)


def _lspec(shape, layer):
    nd = len(shape)
    return pl.BlockSpec((None,) + tuple(shape), lambda *g: (layer,) + (0,) * nd)


def _mod_spec(layer, k, rows_per_seg):
    return pl.BlockSpec((None, None, 1, 1, D), lambda i, *_: (layer, k, i // rows_per_seg, 0, 0))


def _dot(a, b):
    return jnp.dot(a, b, preferred_element_type=F32)


def _dot_nt(a, b):
    return lax.dot_general(a, b, (((1,), (1,)), ((), ())), preferred_element_type=F32)


def _dot_tn(a, b):
    return lax.dot_general(a, b, (((0,), (0,)), ((), ())), preferred_element_type=F32)


def _rms(x, w):
    ms = jnp.mean(x * x, axis=-1, keepdims=True)
    return x * lax.rsqrt(ms + EPS) * w


def _silu(x):
    return x * jax.nn.sigmoid(x)


def _split_bf16(x):
    hi = x.astype(BF16)
    lo = (x - hi.astype(F32)).astype(BF16)
    return hi, lo


def _segsum2(x, ones_blk):
    hi, lo = _split_bf16(x)
    return _dot(hi, ones_blk) + _dot(lo, ones_blk)


def _cumsum_rows(x, r8, rev):
    n = x.shape[0]
    for s in (1, 2, 4):
        if not rev:
            x = jnp.where(r8 >= s, x + pltpu.roll(x, s, 0), x)
        else:
            x = jnp.where(r8 < 8 - s, x + pltpu.roll(x, n - s, 0), x)
    ng = n // 8
    outs = [None] * ng
    c = None
    for g in (range(ng) if not rev else reversed(range(ng))):
        xg = x[8 * g:8 * g + 8]
        if c is not None:
            xg = xg + c
        c = xg[7:8] if not rev else xg[0:1]
        outs[g] = xg
    return jnp.concatenate(outs, axis=0)


def _ada_body(c_ref, w_ref, b_ref, o_ref):
    s = _silu(c_ref[...])
    o_ref[0] = _dot(s.astype(BF16), w_ref[0].astype(BF16)) + b_ref[0]


def _ada_call(cond8, w_ada, b_ada):
    nj = 4
    wj = 6 * D // nj
    return pl.pallas_call(
        _ada_body,
        grid=(2, nj),
        in_specs=[pl.BlockSpec((8, D), lambda l, j: (0, 0)),
                  pl.BlockSpec((1, D, wj), lambda l, j: (l, 0, j)),
                  pl.BlockSpec((1, 1, wj), lambda l, j: (l, 0, j))],
        out_specs=pl.BlockSpec((1, 8, wj), lambda l, j: (l, 0, j)),
        out_shape=jax.ShapeDtypeStruct((2, 8, 6 * D), F32),
        compiler_params=_cparams(("arbitrary", "arbitrary"), VMEM_LIMIT),
        name="ada",
    )(cond8, w_ada, b_ada.reshape(2, 1, 6 * D))


def _in_body(n_ctx_steps, *refs):
    if n_ctx_steps is None:
        x_ref, sc_ref, sh_ref, nw_ref, w_ref, pa_ref, pb_ref, pc_ref, pd_ref = refs
    else:
        xc_ref, xl_ref, sc_ref, sh_ref, nw_ref, w_ref, x_ref, pa_ref, pb_ref, pc_ref, pd_ref = refs
    rc = 256
    for r0 in range(0, pa_ref.shape[0], rc):
        rows = slice(r0, r0 + rc)
        if n_ctx_steps is None:
            x = x_ref[rows, :]
        else:
            x = jnp.where(pl.program_id(0) < n_ctx_steps, xc_ref[rows, :], xl_ref[rows, :])
            x_ref[rows, :] = x
        h = _rms(x, nw_ref[...]) * (1.0 + sc_ref[0]) + sh_ref[0]
        hb = h.astype(BF16)
        o = 0
        for ref, w in ((pa_ref, PA_W), (pb_ref, PB_W), (pc_ref, PC_W), (pd_ref, PD_W)):
            ref[rows, :] = _dot(hb, w_ref[:, o:o + w])
            o += w


def _in_call(src, layer, mod, norm1, w_in_b):
    tm = 512
    n = NTOK // tm
    per_seg = SEG // tm
    row = lambda i: (i, 0)
    tokspec = pl.BlockSpec((tm, D), row)
    widths = (PA_W, PB_W, PC_W, PD_W)
    common_specs = [_mod_spec(layer, 1, per_seg), _mod_spec(layer, 0, per_seg), _lspec((1, D), layer),
                    _lspec((D, IN_PAD_W), layer)]
    args = (*src, mod, mod, norm1, w_in_b)
    if len(src) == 2:
        n_ctx_steps = src[0].shape[0] // tm
        widths = (D,) + widths
        in_specs = [pl.BlockSpec((tm, D), lambda i: (jnp.minimum(i, n_ctx_steps - 1), 0)),
                    pl.BlockSpec((tm, D), lambda i: (jnp.maximum(i - n_ctx_steps, 0), 0))] + common_specs
    else:
        n_ctx_steps = None
        in_specs = [tokspec] + common_specs
    outs = [jax.ShapeDtypeStruct((NTOK, w), F32) for w in widths]
    out_specs = [pl.BlockSpec((tm, w), row) for w in widths]
    return pl.pallas_call(
        functools.partial(_in_body, n_ctx_steps),
        grid=(n,), in_specs=in_specs, out_specs=out_specs, out_shape=outs,
        compiler_params=_cparams(("arbitrary",), VMEM_LIMIT),
        name="in_proj",
    )(*args)


def _halo_tile(ref, c0, c1, t0, t_len, static_single):
    xa = ref[pl.ds(t0, TILE), c0:c1]
    if static_single:
        z = jnp.zeros((8, c1 - c0), F32)
        return xa, jnp.concatenate([z, xa, z], axis=0)
    ps = pl.multiple_of(jnp.maximum(t0 - 8, 0), 8)
    ns = pl.multiple_of(jnp.minimum(t0 + TILE, t_len - 8), 8)
    prev = jnp.where(t0 > 0, ref[pl.ds(ps, 8), c0:c1], 0.0)
    nxt = jnp.where(t0 + TILE < t_len, ref[pl.ds(ns, 8), c0:c1], 0.0)
    return xa, jnp.concatenate([prev, xa, nxt], axis=0)


def _seq_spec(t_len, width, blk_off):
    idx = lambda b: (b + blk_off, 0)
    if t_len > TILE:
        return pl.BlockSpec((t_len, width), idx, pipeline_mode=pl.Buffered(1))
    return pl.BlockSpec((t_len, width), idx)


def _gelu_tanh(x):
    return 0.5 * x * (1.0 + jnp.tanh(0.7978845608028654 * (x + 0.044715 * (x * x * x))))


def _softplus(x):
    return jnp.maximum(x, 0.0) + jnp.log1p(jnp.exp(-jnp.abs(x)))


def _lru_scan(a, u, c, r8, rev):
    n = a.shape[0]
    for s in (1, 2, 4):
        if not rev:
            m = r8 >= s
            a_sh, u_sh = pltpu.roll(a, s, 0), pltpu.roll(u, s, 0)
        else:
            m = r8 < 8 - s
            a_sh, u_sh = pltpu.roll(a, n - s, 0), pltpu.roll(u, n - s, 0)
        u = jnp.where(m, a * u_sh + u, u)
        a = jnp.where(m, a * a_sh, a)
    ng = n // 8
    outs = [None] * ng
    for g in (range(ng) if not rev else reversed(range(ng))):
        hg = u[8 * g:8 * g + 8] + a[8 * g:8 * g + 8] * c
        c = hg[7:8] if not rev else hg[0:1]
        outs[g] = hg
    return jnp.concatenate(outs, axis=0), c


def _lru_body(t_len, pa_ref, cw_ref, cb_ref, wa_ref, ba_ref, wx_ref, bx_ref, lam_ref, h0_ref,
              oa_ref, hfin_ref, hf_s, ab_s, ub_s):
    nt = t_len // TILE
    single = nt == 1
    r8 = lax.broadcasted_iota(I32, (TILE, 1), 0) & 7
    n_ext = TILE + 16

    def gates(xc, xb, d):
        r = jax.nn.sigmoid(_dot(xb, wa_ref[d]) + ba_ref[d])
        i = jax.nn.sigmoid(_dot(xb, wx_ref[d]) + bx_ref[d])
        log_a = -LRU_C * r * _softplus(-lam_ref[d])
        a = jnp.exp(log_a)
        th = jnp.tanh(log_a)
        mult = jnp.sqrt(jnp.maximum(-2.0 * th / (1.0 - th), 0.0))
        return a, mult * (i * xc)

    def fwd_tile(i, c):
        t0 = pl.multiple_of(i * TILE, TILE)
        xa, ext = _halo_tile(pa_ref, 0, 256, t0, t_len, single)
        xc = cb_ref[...] + xa * cw_ref[1:2, :]
        for j in (0, 2, 3):
            xc = xc + pltpu.roll(ext, n_ext - 7 - j, 0)[0:TILE] * cw_ref[j:j + 1, :]
        xb = xc.astype(BF16)
        a_f, u_f = gates(xc, xb, 0)
        h, c = _lru_scan(a_f, u_f, c, r8, False)
        hf_s[pl.ds(t0, TILE), :] = h
        a_b, u_b = gates(xc, xb, 1)
        ab_s[pl.ds(t0, TILE), :] = a_b
        ub_s[pl.ds(t0, TILE), :] = u_b
        return c

    def bwd_tile(k, c):
        t0 = pl.multiple_of((nt - 1 - k) * TILE, TILE)
        h_b, c = _lru_scan(ab_s[pl.ds(t0, TILE), :], ub_s[pl.ds(t0, TILE), :], c, r8, True)
        gate = pa_ref[pl.ds(t0, TILE), 256:512]
        oa_ref[pl.ds(t0, TILE), :] = (hf_s[pl.ds(t0, TILE), :] + h_b) * _gelu_tanh(gate)
        return c

    h0 = h0_ref[0]
    if single:
        c_f = fwd_tile(0, h0[0:1])
        c_b = bwd_tile(0, h0[1:2])
    else:
        c_f = lax.fori_loop(0, nt, fwd_tile, h0[0:1])
        c_b = lax.fori_loop(0, nt, bwd_tile, h0[1:2])
    hfin_ref[0, 0:1, :] = c_f
    hfin_ref[0, 1:2, :] = c_b


def _lru_call(pa, t_len, nb, blk_off, layer, conv_w, conv_b, wa_bd, ba, wx_bd, bx, lam, h0):
    full2 = lambda shp: _lspec(shp, layer)
    if h0.ndim == 4:
        h0_spec = pl.BlockSpec((1, None, 2, 256), lambda b: (b, layer, 0, 0))
    else:
        h0_spec = pl.BlockSpec((1, 2, 256), lambda b: (b, 0, 0))
    return pl.pallas_call(
        functools.partial(_lru_body, t_len),
        grid=(nb,),
        in_specs=[_seq_spec(t_len, PA_W, blk_off),
                  full2((4, 256)), full2((1, 256)), full2((2, 256, 256)), full2((2, 1, 256)),
                  full2((2, 256, 256)), full2((2, 1, 256)), full2((2, 1, 256)),
                  h0_spec],
        out_specs=[pl.BlockSpec((t_len, 256), lambda b: (b, 0)),
                   pl.BlockSpec((1, 2, 256), lambda b: (b, 0, 0))],
        out_shape=[jax.ShapeDtypeStruct((nb * t_len, 256), F32), jax.ShapeDtypeStruct((nb, 2, 256), F32)],
        scratch_shapes=[pltpu.VMEM((t_len, 256), F32)] * 3,
        compiler_params=_cparams(("arbitrary",), VMEM_LIMIT),
        name="lru",
    )(pa, conv_w, conv_b, wa_bd, ba, wx_bd, bx, lam, h0)


def _pool_body(t_len, pb_ref, pw_ref, ps_ref, ob_ref):
    nt = t_len // TILE
    single = nt == 1
    n_ext = TILE + 16
    lane = lax.broadcasted_iota(I32, (1, 256), 1)
    rowi = lax.broadcasted_iota(I32, (TILE, 1), 0)

    def ahead(x, k):
        return pltpu.roll(x, n_ext - k, 0)

    def tile(i, carry):
        t0 = pl.multiple_of(i * TILE, TILE)
        xa, ext = _halo_tile(pb_ref, 0, 256, t0, t_len, single)
        p2 = ext + ahead(ext, 1)
        p4 = p2 + ahead(p2, 2)
        p8 = p4 + ahead(p4, 4)
        p16 = p8 + ahead(p8, 8)
        sums = (ahead(p2, 7)[0:TILE], ahead(p4, 6)[0:TILE], ahead(p8, 4)[0:TILE], p16[0:TILE])
        tpos = t0 + rowi
        means = []
        for w, s in zip((2, 4, 8, 16), sums):
            cnt = jnp.minimum(tpos + w // 2, t_len) - jnp.maximum(tpos - w // 2, 0)
            means.append(s / cnt.astype(F32))
        mean = jnp.where(lane < 64, means[0], jnp.where(lane < 128, means[1],
                                                          jnp.where(lane < 192, means[2], means[3])))
        pooled = mean - xa
        ob_ref[pl.ds(t0, TILE), :] = _dot(pooled.astype(BF16), pw_ref[...]) * ps_ref[...]
        return carry

    if single:
        tile(0, 0)
    else:
        lax.fori_loop(0, nt, tile, 0)


def _pool_call(pb, t_len, nb, blk_off, layer, pw_bd, pscale):
    return pl.pallas_call(
        functools.partial(_pool_body, t_len),
        grid=(nb,),
        in_specs=[_seq_spec(t_len, PB_W, blk_off), _lspec((256, 256), layer), _lspec((1, 256), layer)],
        out_specs=pl.BlockSpec((t_len, 256), lambda b: (b, 0)),
        out_shape=jax.ShapeDtypeStruct((nb * t_len, 256), F32),
        compiler_params=_cparams(("arbitrary",), VMEM_LIMIT),
        name="pool",
    )(pb, pw_bd, pscale)


HGRN_LEVELS = (1, 2, 4, 8, 16, 32, 64, 128)


def _hgrn_dir(rev, q, k, v, lf, st_s, att_s, ones_ref, pm_ref, rowi, r8, lane_head):
    c_rows = TILE
    half = TILE // 2
    g = _cumsum_rows(lf, r8, rev)
    vb = v.astype(BF16)
    seg = g
    head_on = [jnp.where(lane_head == h, 1.0, 0.0).astype(BF16) for h in range(4)]
    for li, m in enumerate(HGRN_LEVELS):
        up = (rowi & (2 * m - 1)) >= m
        if not rev:
            ref_q = pltpu.roll(seg, m, 0)
            qsel, ksel = up, jnp.logical_not(up)
            seg_next = jnp.where(up, seg, pltpu.roll(seg, c_rows - m, 0))
        else:
            ref_q = pltpu.roll(seg, c_rows - m, 0)
            qsel, ksel = jnp.logical_not(up), up
            seg_next = jnp.where(up, pltpu.roll(seg, m, 0), seg)
        qp = jnp.where(qsel, q * jnp.exp(g - ref_q), 0.0).astype(BF16)
        kp = jnp.where(ksel, k * jnp.exp(seg - g), 0.0).astype(BF16)

        def scores(qh, kh):
            qs = jnp.concatenate([qp[half * qh:half * (qh + 1)] * head_on[h] for h in range(4)], axis=0)
            return _dot_nt(qs, kp[half * kh:half * (kh + 1)])

        if 2 * m == c_rows:
            cross_q, cross_k = (1, 0) if not rev else (0, 1)
            cross = scores(cross_q, cross_k)
        else:
            for b in range(2):
                prod = scores(b, b)
                for h in range(4):
                    rows = slice(h * half, (h + 1) * half)
                    att_s[li, b, rows, :] = prod[rows] if 2 * m == half else prod[rows] * pm_ref[li]
        seg = seg_next
    vm = [[vb[half * b:half * (b + 1)] * head_on[h] for h in range(4)] for b in range(2)]
    o_halves = []
    for b in range(2):
        ob = None
        for h in range(4):
            rows = slice(h * half, (h + 1) * half)
            att = att_s[0, b, rows, :]
            for li in range(1, len(HGRN_LEVELS) - 1):
                att = att + att_s[li, b, rows, :]
            term = _dot(att.astype(BF16), vm[b][h])
            if b == cross_q:
                term = term + _dot(cross[rows].astype(BF16), vm[cross_k][h])
            ob = term if ob is None else ob + term
        o_halves.append(ob)
    o = _dot((q * k).astype(BF16), ones_ref[...]) * v + jnp.concatenate(o_halves, axis=0)
    st = st_s[...]
    o = o + _dot_nt((q * jnp.exp(g)).astype(BF16), st.astype(BF16))
    g_end = g[c_rows - 1:c_rows] if not rev else g[0:1]
    kd = k * jnp.exp(g_end - g)
    upd = _dot_tn(vb, kd.astype(BF16))
    blk = (lax.broadcasted_iota(I32, (256, 1), 0) >> 6) == (lax.broadcasted_iota(I32, (1, 256), 1) >> 6)
    st_s[...] = st * jnp.exp(g_end) + jnp.where(blk, upd, 0.0)
    return o


def _hgrn_body(t_len, ctx_pass, pc_ref, lb_ref, nw_ref, s_ref, ones_ref, pm_ref, oc_ref, *rest):
    if ctx_pass:
        sfin_ref, of_s, st_s, att_s = rest
    else:
        of_s, st_s, att_s = rest
    nt = t_len // TILE
    rowi = lax.broadcasted_iota(I32, (TILE, 1), 0)
    r8 = rowi & 7
    lane_head = lax.broadcasted_iota(I32, (1, 256), 1) >> 6

    def load(t0, d):
        q = _silu(pc_ref[pl.ds(t0, TILE), 0:256]) * 0.125
        f_raw = pc_ref[pl.ds(t0, TILE), 256 * (1 + d):256 * (2 + d)]
        v = pc_ref[pl.ds(t0, TILE), 768:1024]
        lb = lb_ref[d]
        f_val = lb + (1.0 - lb) * jax.nn.sigmoid(f_raw)
        lf = jnp.log(jnp.maximum(f_val, TINY))
        return q, 1.0 - f_val, v, lf

    def fwd_tile(i, carry):
        t0 = pl.multiple_of(i * TILE, TILE)
        q, k, v, lf = load(t0, 0)
        of_s[pl.ds(t0, TILE), :] = _hgrn_dir(False, q, k, v, lf, st_s, att_s, ones_ref, pm_ref, rowi, r8, lane_head)
        return carry

    def bwd_tile(kk, carry):
        t0 = pl.multiple_of((nt - 1 - kk) * TILE, TILE)
        q, k, v, lf = load(t0, 1)
        o = of_s[pl.ds(t0, TILE), :] + _hgrn_dir(True, q, k, v, lf, st_s, att_s, ones_ref, pm_ref, rowi, r8, lane_head)
        ms = _segsum2(o * o, ones_ref[...]) * (1.0 / 64.0)
        y = o * lax.rsqrt(ms + EPS) * nw_ref[...]
        oc_ref[pl.ds(t0, TILE), :] = y * _silu(pc_ref[pl.ds(t0, TILE), 1024:1280])
        return carry

    def put_state(d):
        st = st_s[...]
        hi = st.astype(BF16)
        r1 = st - hi.astype(F32)
        mid = r1.astype(BF16)
        lo = (r1 - mid.astype(F32)).astype(BF16)
        sfin_ref[0, d] = _dot(hi, s_ref[...]) + _dot(mid, s_ref[...]) + _dot(lo, s_ref[...])

    for d, tile_fn in ((0, fwd_tile), (1, bwd_tile)):
        st_s[...] = jnp.zeros((256, 256), F32) if ctx_pass else s_ref[0, d]
        if nt == 1:
            tile_fn(0, 0)
        else:
            lax.fori_loop(0, nt, tile_fn, 0, unroll=2)
        if ctx_pass:
            put_state(d)


def _pair_masks():
    i = np.arange(TILE // 2)
    return np.stack([((i[:, None] // (2 * m)) == (i[None, :] // (2 * m))).astype(np.float32)
                     for m in HGRN_LEVELS[:-1]])


def _hgrn_call(pc, t_len, nb, blk_off, layer, lower, normw, s0t, ones64):
    n_lv = len(HGRN_LEVELS) - 1
    ctx_pass = s0t is None
    out_specs = [pl.BlockSpec((t_len, 256), lambda b: (b, 0))]
    out_shape = [jax.ShapeDtypeStruct((nb * t_len, 256), F32)]
    if ctx_pass:
        fold = np.zeros((256, 128), np.float32)
        fold[np.arange(256), np.arange(256) % 64] = 1.0
        s_arg, s_spec = jnp.asarray(fold, BF16), pl.BlockSpec((256, 128), lambda b: (0, 0))
        out_specs.append(pl.BlockSpec((1, 2, 256, 128), lambda b: (b, 0, 0, 0)))
        out_shape.append(jax.ShapeDtypeStruct((nb, 2, 256, 128), F32))
    else:
        s_arg, s_spec = s0t, pl.BlockSpec((1, None, 2, 256, 256), lambda b: (b, layer, 0, 0, 0))
    return pl.pallas_call(
        functools.partial(_hgrn_body, t_len, ctx_pass),
        grid=(nb,),
        in_specs=[_seq_spec(t_len, PC_W, blk_off),
                  _lspec((2, 1, 256), layer),
                  _lspec((1, 256), layer),
                  s_spec,
                  pl.BlockSpec((256, 256), lambda b: (0, 0)),
                  pl.BlockSpec((n_lv, TILE // 2, TILE // 2), lambda b: (0, 0, 0))],
        out_specs=out_specs,
        out_shape=out_shape,
        scratch_shapes=[pltpu.VMEM((t_len, 256), F32), pltpu.VMEM((256, 256), F32),
                        pltpu.VMEM((len(HGRN_LEVELS) - 1, 2, 2 * TILE, TILE // 2), F32)],
        compiler_params=_cparams(("arbitrary",), VMEM_LIMIT),
        name="hgrn",
    )(pc, lower, normw, s_arg, ones64, jnp.asarray(_pair_masks()))


ATT_SCALE = 96.0 ** -0.5
KEY_BLK = 512
EXP_SAFE = 40.0


def _rope512(x, cs128, sn128, lane128):
    cs = jnp.concatenate([cs128] * 4, axis=1)
    sn = jnp.concatenate([sn128] * 4, axis=1)
    partner = jnp.where(lane128 < 80, pltpu.roll(x, 512 - 16, 1), pltpu.roll(x, 16, 1))
    return x * cs + partner * sn


def _mla_body(t_len, n_ctx, use_rope, pd_ref, ckv_c_ref, kr_c_ref, cs_ref, sn_ref, qnorm_ref, wuq_ref, qnw_ref,
              kvnorm_ref, wukv_ref, knw_ref, ones_ref, od_ref, ckvn_ref, k_s, v_s, m_s, l_s, acc_s):
    nt = t_len // TILE
    t_k = n_ctx + t_len
    assert (t_k - TILE) % KEY_BLK == 0
    n_kb = (t_k - TILE) // KEY_BLK
    lane128 = lax.broadcasted_iota(I32, (1, 512), 1) & 127
    lane_head = lax.broadcasted_iota(I32, (1, 256), 1) >> 6

    def head_norm(x, w_ref):
        ss = _segsum2(x * x, ones_ref[...])
        return x * lax.rsqrt(ss * (1.0 / 96.0) + EPS) * w_ref[...]

    def put_kv(r0, ckv_n, kr128, rope_rows):
        kv = _dot(ckv_n.astype(BF16), wukv_ref[...])
        k_all = kv[:, 0:512] + jnp.concatenate([kr128] * 4, axis=1)
        kn = head_norm(k_all, knw_ref)
        if rope_rows is not None:
            kn = _rope512(kn, cs_ref[pl.ds(rope_rows, TILE), :], sn_ref[pl.ds(rope_rows, TILE), :], lane128)
        k_s[pl.ds(r0, TILE), :] = kn.astype(BF16)
        v_s[pl.ds(r0, TILE), :] = kv[:, 512:768].astype(BF16)

    if n_ctx:
        put_kv(0, ckv_c_ref[0], kr_c_ref[0], None)

    def kv_tile(i, carry):
        t0 = pl.multiple_of(i * TILE, TILE)
        ckv_n = _rms(pd_ref[pl.ds(t0, TILE), 256:384], kvnorm_ref[...])
        ckvn_ref[pl.ds(t0, TILE), :] = ckv_n
        put_kv(pl.multiple_of(n_ctx + t0, TILE), ckv_n, pd_ref[pl.ds(t0, TILE), 512:640],
               t0 if use_rope else None)
        return carry

    if nt == 1:
        kv_tile(0, 0)
    else:
        lax.fori_loop(0, nt, kv_tile, 0)

    def q_tile(bounded, i, carry):
        t0 = pl.multiple_of(i * TILE, TILE)
        qn = _rms(pd_ref[pl.ds(t0, TILE), 0:256], qnorm_ref[...])
        q = head_norm(_dot(qn.astype(BF16), wuq_ref[...]), qnw_ref)
        if use_rope:
            q = _rope512(q, cs_ref[pl.ds(t0, TILE), :], sn_ref[pl.ds(t0, TILE), :], lane128)
        qb = (q * ATT_SCALE).astype(BF16)
        qhs = [qb[:, 128 * h:128 * (h + 1)] for h in range(4)]

        def keys(h, rows):
            return k_s[rows, 128 * h:128 * (h + 1)]

        def vals(h, rows):
            return v_s[rows, :]

        first = slice(0, TILE)

        def first_block(h):
            s = _dot_nt(qhs[h], keys(h, first))
            m = jnp.max(s, axis=1, keepdims=True)
            p = jnp.exp(s - m)
            return m, jnp.sum(p, axis=1, keepdims=True), _dot(p.astype(BF16), vals(h, first))

        if n_kb == 0:
            outs = []
            for h in range(4):
                _, l, acc = first_block(h)
                outs.append(acc / l)
        elif bounded:
            for h in range(4):
                p = jnp.exp(_dot_nt(qhs[h], keys(h, first)))
                l_s[h] = jnp.broadcast_to(jnp.sum(p, axis=1, keepdims=True), (TILE, 128))
                acc_s[h] = _dot(p.astype(BF16), vals(h, first))

            def kblock_bounded(j, c2):
                rows = pl.ds(pl.multiple_of(TILE + j * KEY_BLK, TILE), KEY_BLK)
                for h in range(4):
                    p = jnp.exp(_dot_nt(qhs[h], keys(h, rows)))
                    l_s[h] = l_s[h] + jnp.sum(p, axis=1, keepdims=True)
                    acc_s[h] = acc_s[h] + _dot(p.astype(BF16), vals(h, rows))
                return c2

            lax.fori_loop(0, n_kb, kblock_bounded, 0, unroll=True)
            outs = [acc_s[h] / jnp.concatenate([l_s[h], l_s[h]], axis=1) for h in range(4)]
        else:
            for h in range(4):
                m, l, acc = first_block(h)
                m_s[h] = jnp.broadcast_to(m, (TILE, 128))
                l_s[h] = jnp.broadcast_to(l, (TILE, 128))
                acc_s[h] = acc

            def kblock(j, c2):
                rows = pl.ds(pl.multiple_of(TILE + j * KEY_BLK, TILE), KEY_BLK)
                for h in range(4):
                    s = _dot_nt(qhs[h], keys(h, rows))
                    m_prev = m_s[h]
                    m_new = jnp.maximum(m_prev, jnp.max(s, axis=1, keepdims=True))
                    alpha = jnp.exp(m_prev - m_new)
                    p = jnp.exp(s - jnp.concatenate([m_new] * (KEY_BLK // 128), axis=1))
                    l_s[h] = alpha * l_s[h] + jnp.sum(p, axis=1, keepdims=True)
                    acc_s[h] = (jnp.concatenate([alpha, alpha], axis=1) * acc_s[h]
                                + _dot(p.astype(BF16), vals(h, rows)))
                    m_s[h] = m_new
                return c2

            lax.fori_loop(0, n_kb, kblock, 0)
            outs = [acc_s[h] / jnp.concatenate([l_s[h], l_s[h]], axis=1) for h in range(4)]
        o = outs[3]
        for h in range(3):
            o = jnp.where(lane_head == h, outs[h], o)
        od_ref[pl.ds(t0, TILE), :] = o
        return carry

    if n_kb == 0:
        if nt == 1:
            q_tile(False, 0, 0)
        else:
            lax.fori_loop(0, nt, functools.partial(q_tile, False), 0)
    else:
        score_bound = (96.0 * ATT_SCALE) * jnp.max(jnp.abs(qnw_ref[...])) * jnp.max(jnp.abs(knw_ref[...]))
        safe = score_bound < EXP_SAFE

        @pl.when(safe)
        def _():
            lax.fori_loop(0, nt, functools.partial(q_tile, True), 0)

        @pl.when(jnp.logical_not(safe))
        def _():
            lax.fori_loop(0, nt, functools.partial(q_tile, False), 0)


def _mla_call(pd, t_len, nb, blk_off, layer, n_ctx, use_rope, ckv_c, kr_c, cs, sn, wts):
    qnorm, wuq, qnw, kvnorm, wukv, knw, ones128 = wts
    t_k = n_ctx + t_len
    c2 = lambda shp: pl.BlockSpec(shp, lambda b: (0,) * len(shp))
    lw = lambda shp: _lspec(shp, layer)
    if ckv_c.ndim == 4:
        ctx_spec = pl.BlockSpec((1, None, 256, 128), lambda b: (b, layer, 0, 0))
    else:
        ctx_spec = pl.BlockSpec((1, 256, 128), lambda b: (b, 0, 0))
    return pl.pallas_call(
        functools.partial(_mla_body, t_len, n_ctx, use_rope),
        grid=(nb,),
        in_specs=[_seq_spec(t_len, PD_W, blk_off), ctx_spec, ctx_spec,
                  c2((t_len, 128)), c2((t_len, 128)),
                  lw((1, 256)), lw((256, 512)), lw((1, 512)), lw((1, 128)), lw((128, 768)), lw((1, 512)),
                  c2((512, 512))],
        out_specs=[pl.BlockSpec((t_len, 256), lambda b: (b, 0)),
                   pl.BlockSpec((t_len, 128), lambda b: (b, 0))],
        out_shape=[jax.ShapeDtypeStruct((nb * t_len, 256), F32), jax.ShapeDtypeStruct((nb * t_len, 128), F32)],
        scratch_shapes=[pltpu.VMEM((t_k, 512), BF16), pltpu.VMEM((t_k, 256), BF16),
                        pltpu.VMEM((4, TILE, 128), F32), pltpu.VMEM((4, TILE, 128), F32),
                        pltpu.VMEM((4, TILE, 256), F32)],
        compiler_params=_cparams(("arbitrary",), VMEM_LIMIT),
        name="mla",
    )(pd, ckv_c, kr_c, cs, sn, qnorm, wuq, qnw, kvnorm, wukv, knw, ones128)


def _out_body(n_ctx_steps, *refs):
    mix_refs = refs[:8]
    (x_ref, g1_ref, sc_ref, sh_ref, nw_ref, wout_ref, wrh_ref, wrl_ref,
     x1_ref, h2e_ref, aff_ref) = refs[8:]
    is_ctx = pl.program_id(0) < n_ctx_steps
    lane = lax.broadcasted_iota(I32, (1, 128), 1)
    rc = 256
    for r0 in range(0, x_ref.shape[0], rc):
        rows = slice(r0, r0 + rc)
        m = None
        for k in range(4):
            ok = jnp.where(is_ctx, mix_refs[2 * k][rows, :], mix_refs[2 * k + 1][rows, :]).astype(BF16)
            mk = _dot(ok, wout_ref[256 * k:256 * (k + 1), :])
            m = mk if m is None else m + mk
        x1 = x_ref[rows, :] + g1_ref[0] * m
        x1_ref[rows, :] = x1
        h2 = _rms(x1, nw_ref[...]) * (1.0 + sc_ref[0]) + sh_ref[0]
        hh, hl = _split_bf16(h2)
        lg = _dot(hh, wrh_ref[...]) + _dot(hl, wrh_ref[...]) + _dot(hh, wrl_ref[...])
        lg = jnp.where(lane < N_EXP, lg, -jnp.inf)
        ex = jnp.exp(lg - jnp.max(lg, axis=1, keepdims=True))
        aff = ex / jnp.sum(ex, axis=1, keepdims=True)
        a_hi = aff.astype(BF16).astype(F32)
        a_lo = aff - a_hi
        ext = a_hi + pltpu.roll(a_lo, N_EXP, 1)
        h2e_ref[rows, 0:D] = hh
        h2e_ref[rows, D:HEXT] = ext.astype(BF16)
        aff_ref[rows, :] = aff


def _out_call(mix, x, layer, mod, norm2, wout_b, wr_h, wr_l):
    tm = 512
    n = NTOK // tm
    per_seg = SEG // tm
    n_ctx_steps = mix[0].shape[0] // tm
    row = lambda i: (i, 0)
    c2 = lambda shp: _lspec(shp, layer)
    ctx_spec = pl.BlockSpec((tm, 256), lambda i: (jnp.minimum(i, n_ctx_steps - 1), 0))
    lat_spec = pl.BlockSpec((tm, 256), lambda i: (jnp.maximum(i - n_ctx_steps, 0), 0))
    return pl.pallas_call(
        functools.partial(_out_body, n_ctx_steps),
        grid=(n,),
        in_specs=[ctx_spec, lat_spec] * 4 + [pl.BlockSpec((tm, D), row), _mod_spec(layer, 2, per_seg),
                  _mod_spec(layer, 4, per_seg), _mod_spec(layer, 3, per_seg),
                  c2((1, D)), c2((D, D)), c2((D, 128)), c2((D, 128))],
        out_specs=[pl.BlockSpec((tm, D), row), pl.BlockSpec((tm, HEXT), row), pl.BlockSpec((tm, 128), row)],
        out_shape=[jax.ShapeDtypeStruct((NTOK, D), F32), jax.ShapeDtypeStruct((NTOK, HEXT), BF16),
                   jax.ShapeDtypeStruct((NTOK, 128), F32)],
        compiler_params=_cparams(("arbitrary",), VMEM_LIMIT),
        name="out_proj",
    )(*mix, x, mod, mod, mod, norm2, wout_b, wr_h, wr_l)


def _sel_body(n_grp, cap, aff_ref, tri_ref, segt_ref, slot_ref, cum_ref, pref_s):
    w = SEG // n_grp
    nblk = SEG // 256
    aff = jnp.transpose(aff_ref[...])[0:N_EXP, :]
    pref_s[...] = jnp.zeros((N_EXP, SEG), I32)

    def grp_cols(fn):
        return jnp.concatenate([jnp.broadcast_to(fn(g), (N_EXP, w)) for g in range(n_grp)], axis=1)

    def it(i, carry):
        bit = lax.shift_left(jnp.int32(1), 30 - i)
        cand = pref_s[...] | bit
        ge = jnp.where(aff >= pltpu.bitcast(cand, F32), 1.0, 0.0)
        ok = grp_cols(lambda g: jnp.where(
            jnp.sum(ge[:, g * w:(g + 1) * w], axis=1, keepdims=True) >= cap, 1.0, 0.0))
        pref_s[...] = jnp.where(ok > 0.5, cand, pref_s[...])
        return carry

    lax.fori_loop(0, 31, it, 0)
    thr = pref_s[...]

    def grp_cumsum(x):
        outs, off = [], None
        for b in range(nblk):
            loc = _dot(x[:, 256 * b:256 * (b + 1)].astype(BF16), tri_ref[...])
            if (256 * b) % w == 0:
                off = None
            if off is not None:
                loc = loc + off
            off = loc[:, 255:256]
            outs.append(loc)
        return jnp.concatenate(outs, axis=1)

    gt = jnp.where(aff >= pltpu.bitcast(thr + 1, F32), 1.0, 0.0)
    eq = jnp.where(aff >= pltpu.bitcast(thr, F32), 1.0, 0.0) - gt
    room = grp_cols(lambda g: cap - jnp.sum(gt[:, g * w:(g + 1) * w], axis=1, keepdims=True))
    sel = jnp.where((gt > 0.5) | ((eq > 0.5) & (grp_cumsum(eq) <= room)), 1.0, 0.0)
    base = grp_cols(lambda g: jnp.full((N_EXP, 1), float(g * cap), F32))
    slot = base + grp_cumsum(sel) - 1.0
    slot_ref[0] = jnp.where(sel > 0.5, slot, -1.0).astype(I32)
    cum_ref[0] = _dot(sel.astype(BF16), segt_ref[...]).astype(I32)


def _sel_call(afft, n_seg, seg_off, n_grp, cap, tri, segt):
    c2 = lambda shp: pl.BlockSpec(shp, lambda s: (0,) * len(shp))
    return pl.pallas_call(
        functools.partial(_sel_body, n_grp, cap),
        grid=(n_seg,),
        in_specs=[pl.BlockSpec((SEG, 128), lambda s: (s + seg_off, 0)), c2((256, 256)), c2((SEG, 128))],
        out_specs=[pl.BlockSpec((1, N_EXP, SEG), lambda s: (s, 0, 0)),
                   pl.BlockSpec((1, N_EXP, 128), lambda s: (s, 0, 0))],
        out_shape=[jax.ShapeDtypeStruct((n_seg, N_EXP, SEG), I32), jax.ShapeDtypeStruct((n_seg, N_EXP, 128), I32)],
        scratch_shapes=[pltpu.VMEM((N_EXP, SEG), I32)],
        compiler_params=_cparams(("arbitrary",), VMEM_LIMIT),
        name="select",
    )(afft, tri, segt)


def _windows(cum_ref, s, e, tk):
    lo = cum_ref[s, e, tk]
    hi = cum_ref[s, e, tk + 1]
    w0 = lax.shift_left(lax.shift_right_logical(lo, 4), 4)
    nw = jnp.where(hi > lo, lax.shift_right_logical(hi - w0 + (WIN - 1), WIN_SHIFT), 0)
    return w0, nw


def _onehot(base, slot_row):
    rows = base + lax.broadcasted_iota(I32, (WIN, 1), 0)
    return jnp.where(rows == slot_row, 1.0, 0.0).astype(BF16)


GATHER_EXPERTS = 16


def _gather_body(cum_ref, slot_ref, h2e_ref, g_ref):
    s, half, tk = pl.program_id(0), pl.program_id(1), pl.program_id(2)

    @pl.when(tk == 0)
    def _():
        g_ref[...] = jnp.zeros(g_ref.shape, BF16)

    wins = [_windows(cum_ref, s, half * GATHER_EXPERTS + j, tk) for j in range(GATHER_EXPERTS)]
    bases = [pl.multiple_of(w0, 16) for w0, _ in wins]
    sel = jnp.concatenate([_onehot(bases[j], slot_ref[0, j:j + 1, :]) for j in range(GATHER_EXPERTS)], axis=0)
    got = _dot(sel, h2e_ref[...]).astype(BF16)
    for j in range(GATHER_EXPERTS):
        g_ref[0, j, pl.ds(bases[j], WIN), :] = (g_ref[0, j, pl.ds(bases[j], WIN), :]
                                                  + got[j * WIN:(j + 1) * WIN, :])
    for j in range(GATHER_EXPERTS):
        def wbody(w, carry, j=j):
            base = pl.multiple_of(wins[j][0] + w * WIN, 16)
            more = _dot(_onehot(base, slot_ref[0, j:j + 1, :]), h2e_ref[...])
            g_ref[0, j, pl.ds(base, WIN), :] = g_ref[0, j, pl.ds(base, WIN), :] + more.astype(BF16)
            return carry

        lax.fori_loop(1, wins[j][1], wbody, 0)


def _gather_call(cum, slot, h2e):
    n_half = N_EXP // GATHER_EXPERTS
    return pl.pallas_call(
        _gather_body,
        grid_spec=pltpu.PrefetchScalarGridSpec(
            num_scalar_prefetch=1, grid=(NSEG, n_half, SEG // TK),
            in_specs=[pl.BlockSpec((1, GATHER_EXPERTS, TK), lambda s, h, t, c: (s, h, t)),
                      pl.BlockSpec((TK, HEXT), lambda s, h, t, c: (s * (SEG // TK) + t, 0))],
            out_specs=pl.BlockSpec((1, GATHER_EXPERTS, SLOT_PAD, HEXT), lambda s, h, t, c: (s, h, 0, 0),
                                   pipeline_mode=pl.Buffered(1))),
        out_shape=jax.ShapeDtypeStruct((NSEG, N_EXP, SLOT_PAD, HEXT), BF16),
        compiler_params=_cparams(("arbitrary", "arbitrary", "arbitrary"), VMEM_LIMIT),
        name="moe_gather",
    )(cum, slot, h2e)


def _ffn_body(g_ref, wg_ref, wu_ref, wd_ref, y_ref, wgb_s, wub_s, wdb_s):
    e = pl.program_id(0)
    wgb_s[...] = wg_ref[0, 0].astype(BF16)
    wub_s[...] = wu_ref[0, 0].astype(BF16)
    wdb_s[...] = wd_ref[0, 0].astype(BF16)
    lane = lax.broadcasted_iota(I32, (1, 128), 1)
    pick = (lane == e) | (lane == e + N_EXP)
    for s in range(NSEG):
        xs = g_ref[s, 0, :, 0:D]
        ext = g_ref[s, 0, :, D:HEXT].astype(F32)
        gate = jnp.sum(jnp.where(pick, ext, 0.0), axis=1, keepdims=True)
        a = _dot(xs, wgb_s[...])
        u = _dot(xs, wub_s[...])
        y = _dot((_silu(a) * u).astype(BF16), wdb_s[...]) * gate
        y_ref[s, 0, 0:CAP_SEG, :] = y.astype(BF16)
        y_ref[s, 0, CAP_SEG:SLOT_PAD, :] = jnp.zeros((SLOT_PAD - CAP_SEG, D), BF16)


def _ffn_call(gath, layer, wg, wu, wd):
    wspec = pl.BlockSpec((1, 1, D, D), lambda e: (layer, e, 0, 0))
    return pl.pallas_call(
        _ffn_body,
        grid=(N_EXP,),
        in_specs=[pl.BlockSpec((NSEG, 1, CAP_SEG, HEXT), lambda e: (0, e, 0, 0)), wspec, wspec, wspec],
        out_specs=pl.BlockSpec((NSEG, 1, SLOT_PAD, D), lambda e: (0, e, 0, 0)),
        out_shape=jax.ShapeDtypeStruct((NSEG, N_EXP, SLOT_PAD, D), BF16),
        scratch_shapes=[pltpu.VMEM((D, D), BF16)] * 3,
        compiler_params=_cparams(("arbitrary",), VMEM_LIMIT),
        name="moe_ffn",
    )(gath, wg, wu, wd)


def _scatter_body(split, cum_ref, slot_ref, y_ref, x1_ref, g2_ref, *o_refs):
    s, tk = pl.program_id(0), pl.program_id(1)
    wins = [_windows(cum_ref, s, e, tk) for e in range(N_EXP)]
    bases = [pl.multiple_of(w0, 16) for w0, _ in wins]
    g2 = g2_ref[0]

    def run(o_ref):
        sel = jnp.concatenate([_onehot(bases[e], slot_ref[0, e:e + 1, :]) for e in range(N_EXP)], axis=0)
        rows = jnp.concatenate([y_ref[0, e, pl.ds(bases[e], WIN), :] for e in range(N_EXP)], axis=0)
        o_ref[...] = x1_ref[...] + g2 * _dot_tn(sel, rows)
        for e in range(N_EXP):
            def wbody(w, carry, e=e):
                base = pl.multiple_of(wins[e][0] + w * WIN, 16)
                o_ref[...] = o_ref[...] + g2 * _dot_tn(_onehot(base, slot_ref[0, e:e + 1, :]),
                                                       y_ref[0, e, pl.ds(base, WIN), :])
                return carry

            lax.fori_loop(1, wins[e][1], wbody, 0)

    if not split:
        run(o_refs[0])
    else:
        pl.when(s == 0)(lambda: run(o_refs[0]))
        pl.when(s > 0)(lambda: run(o_refs[1]))


def _scatter_call(cum, slot, y, x1, layer, mod, split):
    n_t = SEG // TK
    tok_idx = lambda s, t, c: (s * n_t + t, 0)
    if split:
        out_specs = [pl.BlockSpec((TK, D), lambda s, t, c: (jnp.where(s == 0, t, n_t - 1), 0)),
                     pl.BlockSpec((TK, D), lambda s, t, c: (jnp.maximum((s - 1) * n_t + t, 0), 0))]
        out_shape = [jax.ShapeDtypeStruct((SEG, D), F32), jax.ShapeDtypeStruct((NTOK - SEG, D), F32)]
    else:
        out_specs = pl.BlockSpec((TK, D), tok_idx)
        out_shape = jax.ShapeDtypeStruct((NTOK, D), F32)
    return pl.pallas_call(
        functools.partial(_scatter_body, split),
        grid_spec=pltpu.PrefetchScalarGridSpec(
            num_scalar_prefetch=1, grid=(NSEG, n_t),
            in_specs=[pl.BlockSpec((1, N_EXP, TK), lambda s, t, c: (s, 0, t)),
                      pl.BlockSpec((1, N_EXP, SLOT_PAD, D), lambda s, t, c: (s, 0, 0, 0)),
                      pl.BlockSpec((TK, D), tok_idx),
                      pl.BlockSpec((None, None, 1, 1, D), lambda s, t, c: (layer, 5, s, 0, 0))],
            out_specs=out_specs),
        out_shape=out_shape,
        compiler_params=_cparams(("arbitrary", "arbitrary"), VMEM_LIMIT),
        name="moe_scatter",
    )(cum, slot, y, x1, mod)


def _block_diag4(w):
    eye = jnp.eye(4, dtype=w.dtype)
    return jnp.einsum('...hij,hg->...higj', w, eye).reshape(w.shape[:-3] + (256, 256))


def _np_block_ones(n, blk):
    i = np.arange(n) // blk
    return (i[:, None] == i[None, :]).astype(np.float32)


def _head_cols():
    j = np.arange(128)
    src = np.full(128, -1)
    src[:64] = j[:64]
    src[64:80] = 64 + 2 * (j[64:80] - 64)
    src[80:96] = 64 + 2 * (j[80:96] - 80) + 1
    return src


def _mla_weights(mla_q_norm, mla_w_uq, mla_kv_norm, mla_w_ukv, mla_qn, mla_kn):
    depth = mla_w_uq.shape[0]
    src = _head_cols()
    valid = src >= 0
    srcc = np.where(valid, src, 0)
    colq = np.concatenate([h * 96 + srcc for h in range(4)])
    maskq = jnp.asarray(np.tile(valid, 4).astype(np.float32))
    wuq = (mla_w_uq[:, :, colq] * maskq).astype(BF16)
    qnw = (jnp.tile(mla_qn[:, srcc], (1, 4)) * maskq).reshape(depth, 1, 512)
    knw = (jnp.tile(mla_kn[:, srcc], (1, 4)) * maskq).reshape(depth, 1, 512)
    jn = np.arange(128)
    nope_valid = jn < 64
    colk = np.concatenate([h * 128 + np.where(nope_valid, jn, 0) for h in range(4)])
    maskk = jnp.asarray(np.tile(nope_valid, 4).astype(np.float32))
    colv = np.concatenate([h * 128 + 64 + np.arange(64) for h in range(4)])
    wukv = jnp.concatenate([mla_w_ukv[:, :, colk] * maskk, mla_w_ukv[:, :, colv]], axis=2).astype(BF16)
    return (mla_q_norm.reshape(depth, 1, 256), wuq, qnw, mla_kv_norm.reshape(depth, 1, 128), wukv, knw,
            jnp.asarray(_np_block_ones(512, 128), BF16))


def _krope128(kr):
    z64 = jnp.zeros(kr.shape[:-1] + (64,), kr.dtype)
    z32 = jnp.zeros(kr.shape[:-1] + (32,), kr.dtype)
    return jnp.concatenate([z64, kr[..., 0::2], kr[..., 1::2], z32], axis=-1)


def _rope_tables(n_tokens, grid_w):
    rows = (np.arange(n_tokens) // grid_w).astype(np.float32)
    cols = (np.arange(n_tokens) % grid_w).astype(np.float32)
    n_freq = 8
    inv = jnp.asarray(10000.0, F32) ** (-jnp.arange(n_freq, dtype=F32) / n_freq)
    ang = jnp.concatenate([jnp.asarray(rows)[:, None] * inv, jnp.asarray(cols)[:, None] * inv], axis=-1)
    cos, sin = jnp.cos(ang), jnp.sin(ang)
    one = jnp.ones((n_tokens, 64), F32)
    zero = jnp.zeros((n_tokens, 64), F32)
    cs = jnp.concatenate([one, cos, cos, one[:, :32]], axis=1)
    sn = jnp.concatenate([zero, -sin, sin, zero[:, :32]], axis=1)
    return cs, sn


def _in_weight(w_in):
    kr = w_in[..., 2432:2464]
    z96 = jnp.zeros(w_in.shape[:-1] + (96,), w_in.dtype)
    return jnp.concatenate([w_in, z96, _krope128(kr)], axis=-1).astype(BF16)


def kernel(x_prompt, x_sample, cache_mla_ckv, cache_mla_krope, state_rglru, state_hgrn, c, c_ctx, norm1_w, norm2_w, w_ada, b_ada, w_in, conv_w, conv_b, lru_wa, lru_ba, lru_wx, lru_bx, lru_lambda, pool_w, pool_scale, hgrn_lower_bounds, hgrn_norm_w, mla_q_norm, mla_w_uq, mla_kv_norm, mla_w_ukv, mla_qk_norm_q, mla_qk_norm_k, w_out, w_router, w_exp_gate, w_exp_up, w_exp_down):
    nbp, t_p = x_prompt.shape[0], x_prompt.shape[1]
    nbs, t_s = x_sample.shape[0], x_sample.shape[1]
    depth = w_in.shape[0]

    n_p = nbp * t_p
    cond8 = jnp.concatenate([c_ctx[None], c, jnp.zeros((5, D), F32)], axis=0)
    mod = _ada_call(cond8, w_ada, b_ada)
    mod = jnp.swapaxes(mod.reshape(depth, 8, 6, 1, D), 1, 2)

    f32 = lambda a: a.astype(F32)
    norm1 = norm1_w.reshape(depth, 1, D)
    norm2 = norm2_w.reshape(depth, 1, D)
    w_in_b = _in_weight(w_in)
    lru_w = (conv_w, conv_b.reshape(depth, 1, 256),
             _block_diag4(lru_wa).astype(BF16), lru_ba.reshape(depth, 2, 1, 256),
             _block_diag4(lru_wx).astype(BF16), lru_bx.reshape(depth, 2, 1, 256),
             lru_lambda.reshape(depth, 2, 1, 256))
    pw_bd = _block_diag4(pool_w).astype(BF16)
    pscale = pool_scale.reshape(depth, 1, 256)
    lb_soft = jax.nn.softmax(f32(hgrn_lower_bounds), axis=1)
    lower = jnp.swapaxes(jnp.cumsum(lb_soft, axis=1) - lb_soft[:, :1], 0, 1).reshape(depth, 2, 1, 256)
    hg_w = (lower, jnp.tile(hgrn_norm_w, (1, 4)).reshape(depth, 1, 256))
    eye4 = jnp.eye(4, dtype=F32)
    s0t = jnp.einsum('blzhdv,hg->blzhvgd', f32(state_hgrn), eye4).reshape(nbs, depth, 2, 256, 256)
    mw = _mla_weights(mla_q_norm, mla_w_uq, mla_kv_norm, mla_w_ukv, mla_qk_norm_q, mla_qk_norm_k)
    kr_ctx = _krope128(cache_mla_krope)
    wout_b = w_out.astype(BF16)
    wr_h, wr_l = _split_bf16(jnp.pad(w_router, ((0, 0), (0, 0), (0, 128 - N_EXP))))
    h0_lat = f32(state_rglru)

    ones64 = jnp.asarray(_np_block_ones(256, 64), BF16)
    tri = jnp.asarray(np.triu(np.ones((256, 256), np.float32)), BF16)
    segt = jnp.asarray((np.arange(SEG)[:, None] < TK * np.arange(128)[None, :]).astype(np.float32), BF16)
    cs_s, sn_s = _rope_tables(t_s, 64)
    cs_p, sn_p = jnp.ones((t_p, 128), F32), jnp.zeros((t_p, 128), F32)
    zero_ctx = jnp.zeros((nbp, 256, 128), F32)
    zero_h0 = jnp.zeros((nbp, 2, 256), F32)

    ckvs, krs, lru_states, hgrn_states = [], [], [], []
    src = (x_prompt.reshape(n_p, D), x_sample.reshape(nbs * t_s, D))
    for l in range(depth):
        res = _in_call(src, l, mod, norm1, w_in_b)
        x = res[0] if len(src) == 2 else src[0]
        pa, pb, pc, pd = res[-4:]

        oa_c, lru_fin = _lru_call(pa, t_p, nbp, 0, l, *lru_w, zero_h0)
        oa_l, _ = _lru_call(pa, t_s, nbs, 1, l, *lru_w, h0_lat)
        ob_c = _pool_call(pb, t_p, nbp, 0, l, pw_bd, pscale)
        ob_l = _pool_call(pb, t_s, nbs, 1, l, pw_bd, pscale)
        oc_c, st_p = _hgrn_call(pc, t_p, nbp, 0, l, *hg_w, None, ones64)
        (oc_l,) = _hgrn_call(pc, t_s, nbs, 1, l, *hg_w, s0t, ones64)
        od_c, ckvn = _mla_call(pd, t_p, nbp, 0, l, 0, False, zero_ctx, zero_ctx, cs_p, sn_p, mw)
        od_l, _ = _mla_call(pd, t_s, nbs, 1, l, 256, True, f32(cache_mla_ckv), kr_ctx, cs_s, sn_s, mw)

        x1, h2e, aff = _out_call((oa_c, oa_l, ob_c, ob_l, oc_c, oc_l, od_c, od_l), x, l, mod,
                                 norm2, wout_b, wr_h, wr_l)

        slot_p, cum_p = _sel_call(aff, 1, 0, nbp, 2 * t_p // N_EXP, tri, segt)
        slot_s, cum_s = _sel_call(aff, nbs, 1, 1, 2 * t_s // N_EXP, tri, segt)
        slot = jnp.concatenate([slot_p, slot_s], axis=0)
        cum = jnp.concatenate([cum_p, cum_s], axis=0)
        gath = _gather_call(cum, slot, h2e)
        y = _ffn_call(gath, l, w_exp_gate, w_exp_up, w_exp_down)
        src = _scatter_call(cum, slot, y, x1, l, mod, split=(l == depth - 1))
        src = tuple(src) if l == depth - 1 else (src,)

        ckvs.append(ckvn.reshape(nbp, t_p, 128))
        krs.append(pd[:n_p, 384:416].reshape(nbp, t_p, 32))
        lru_states.append(lru_fin)
        hgrn_states.append(jnp.swapaxes(st_p[..., :64].reshape(nbp, 2, 4, 64, 64), -1, -2))

    y_c, y_l = src
    return (y_c.reshape(nbp, t_p, D), y_l.reshape(nbs, t_s, D),
            jnp.stack(ckvs, axis=1), jnp.stack(krs, axis=1),
            jnp.stack(lru_states, axis=1), jnp.stack(hgrn_states, axis=1))
```

```python
import functools

import numpy as np
import jax
import jax.numpy as jnp
from jax import lax
from jax.experimental import pallas as pl
from jax.experimental.pallas import tpu as pltpu

F32 = jnp.float32
BF16 = jnp.bfloat16
I32 = jnp.int32

D = 1024
NTOK = 12288
SEG = 4096
NSEG = 3
EPS = 1e-6
TINY = 1e-30
LRU_C = 8.0
N_EXP = 16
CAP_SEG = 512
SLOT_PAD = 576
TILE = 256
TK = 256
WIN = 64
WIN_SHIFT = 6
HEXT = D + 128
VMEM_LIMIT = 56 * 1024 * 1024

PA_W, PB_W, PC_W, PD_W = 512, 256, 1280, 640
IN_PAD_W = PA_W + PB_W + PC_W + PD_W


def _cparams(sem, vmem=None):
    return pltpu.CompilerParams(dimension_semantics=sem, vmem_limit_bytes=vmem)


def _lspec(shape, layer):
    nd = len(shape)
    return pl.BlockSpec((None,) + tuple(shape), lambda *g: (layer,) + (0,) * nd)


def _mod_spec(layer, k, rows_per_seg):
    return pl.BlockSpec((None, None, 1, 1, D), lambda i, *_: (layer, k, i // rows_per_seg, 0, 0))


def _dot(a, b):
    return jnp.dot(a, b, preferred_element_type=F32)


def _dot_nt(a, b):
    return lax.dot_general(a, b, (((1,), (1,)), ((), ())), preferred_element_type=F32)


def _dot_tn(a, b):
    return lax.dot_general(a, b, (((0,), (0,)), ((), ())), preferred_element_type=F32)


def _rms(x, w):
    ms = jnp.mean(x * x, axis=-1, keepdims=True)
    return x * lax.rsqrt(ms + EPS) * w


def _silu(x):
    return x * jax.nn.sigmoid(x)


def _split_bf16(x):
    hi = x.astype(BF16)
    lo = (x - hi.astype(F32)).astype(BF16)
    return hi, lo


def _segsum2(x, ones_blk):
    hi, lo = _split_bf16(x)
    return _dot(hi, ones_blk) + _dot(lo, ones_blk)


def _cumsum_rows(x, r8, rev):
    n = x.shape[0]
    for s in (1, 2, 4):
        if not rev:
            x = jnp.where(r8 >= s, x + pltpu.roll(x, s, 0), x)
        else:
            x = jnp.where(r8 < 8 - s, x + pltpu.roll(x, n - s, 0), x)
    ng = n // 8
    outs = [None] * ng
    c = None
    for g in (range(ng) if not rev else reversed(range(ng))):
        xg = x[8 * g:8 * g + 8]
        if c is not None:
            xg = xg + c
        c = xg[7:8] if not rev else xg[0:1]
        outs[g] = xg
    return jnp.concatenate(outs, axis=0)


def _ada_body(c_ref, w_ref, b_ref, o_ref):
    s = _silu(c_ref[...])
    o_ref[0] = _dot(s.astype(BF16), w_ref[0].astype(BF16)) + b_ref[0]


def _ada_call(cond8, w_ada, b_ada):
    nj = 4
    wj = 6 * D // nj
    return pl.pallas_call(
        _ada_body,
        grid=(2, nj),
        in_specs=[pl.BlockSpec((8, D), lambda l, j: (0, 0)),
                  pl.BlockSpec((1, D, wj), lambda l, j: (l, 0, j)),
                  pl.BlockSpec((1, 1, wj), lambda l, j: (l, 0, j))],
        out_specs=pl.BlockSpec((1, 8, wj), lambda l, j: (l, 0, j)),
        out_shape=jax.ShapeDtypeStruct((2, 8, 6 * D), F32),
        compiler_params=_cparams(("arbitrary", "arbitrary"), VMEM_LIMIT),
        name="ada",
    )(cond8, w_ada, b_ada.reshape(2, 1, 6 * D))


def _in_body(n_ctx_steps, *refs):
    if n_ctx_steps is None:
        x_ref, sc_ref, sh_ref, nw_ref, w_ref, pa_ref, pb_ref, pc_ref, pd_ref = refs
        x = x_ref[...]
    else:
        xc_ref, xl_ref, sc_ref, sh_ref, nw_ref, w_ref, x_ref, pa_ref, pb_ref, pc_ref, pd_ref = refs
        x = jnp.where(pl.program_id(0) < n_ctx_steps, xc_ref[...], xl_ref[...])
        x_ref[...] = x
    h = _rms(x, nw_ref[...]) * (1.0 + sc_ref[0]) + sh_ref[0]
    hb = h.astype(BF16)
    o = 0
    for ref, w in ((pa_ref, PA_W), (pb_ref, PB_W), (pc_ref, PC_W), (pd_ref, PD_W)):
        ref[...] = _dot(hb, w_ref[:, o:o + w])
        o += w


def _in_call(src, layer, mod, norm1, w_in_b):
    tm = 512
    n = NTOK // tm
    per_seg = SEG // tm
    row = lambda i: (i, 0)
    tokspec = pl.BlockSpec((tm, D), row)
    widths = (PA_W, PB_W, PC_W, PD_W)
    common_specs = [_mod_spec(layer, 1, per_seg), _mod_spec(layer, 0, per_seg), _lspec((1, D), layer),
                    _lspec((D, IN_PAD_W), layer)]
    args = (*src, mod, mod, norm1, w_in_b)
    if len(src) == 2:
        n_ctx_steps = src[0].shape[0] // tm
        widths = (D,) + widths
        in_specs = [pl.BlockSpec((tm, D), lambda i: (jnp.minimum(i, n_ctx_steps - 1), 0)),
                    pl.BlockSpec((tm, D), lambda i: (jnp.maximum(i - n_ctx_steps, 0), 0))] + common_specs
    else:
        n_ctx_steps = None
        in_specs = [tokspec] + common_specs
    outs = [jax.ShapeDtypeStruct((NTOK, w), F32) for w in widths]
    out_specs = [pl.BlockSpec((tm, w), row) for w in widths]
    return pl.pallas_call(
        functools.partial(_in_body, n_ctx_steps),
        grid=(n,), in_specs=in_specs, out_specs=out_specs, out_shape=outs,
        compiler_params=_cparams(("arbitrary",), VMEM_LIMIT),
        name="in_proj",
    )(*args)


def _halo_tile(ref, c0, c1, t0, t_len, static_single):
    xa = ref[pl.ds(t0, TILE), c0:c1]
    if static_single:
        z = jnp.zeros((8, c1 - c0), F32)
        return xa, jnp.concatenate([z, xa, z], axis=0)
    ps = pl.multiple_of(jnp.maximum(t0 - 8, 0), 8)
    ns = pl.multiple_of(jnp.minimum(t0 + TILE, t_len - 8), 8)
    prev = jnp.where(t0 > 0, ref[pl.ds(ps, 8), c0:c1], 0.0)
    nxt = jnp.where(t0 + TILE < t_len, ref[pl.ds(ns, 8), c0:c1], 0.0)
    return xa, jnp.concatenate([prev, xa, nxt], axis=0)


def _seq_spec(t_len, width, blk_off):
    idx = lambda b: (b + blk_off, 0)
    if t_len > TILE:
        return pl.BlockSpec((t_len, width), idx, pipeline_mode=pl.Buffered(1))
    return pl.BlockSpec((t_len, width), idx)


def _gelu_tanh(x):
    return 0.5 * x * (1.0 + jnp.tanh(0.7978845608028654 * (x + 0.044715 * (x * x * x))))


def _softplus(x):
    return jnp.maximum(x, 0.0) + jnp.log1p(jnp.exp(-jnp.abs(x)))


def _lru_scan(a, u, c, r8, rev):
    n = a.shape[0]
    for s in (1, 2, 4):
        if not rev:
            m = r8 >= s
            a_sh, u_sh = pltpu.roll(a, s, 0), pltpu.roll(u, s, 0)
        else:
            m = r8 < 8 - s
            a_sh, u_sh = pltpu.roll(a, n - s, 0), pltpu.roll(u, n - s, 0)
        u = jnp.where(m, a * u_sh + u, u)
        a = jnp.where(m, a * a_sh, a)
    ng = n // 8
    outs = [None] * ng
    for g in (range(ng) if not rev else reversed(range(ng))):
        hg = u[8 * g:8 * g + 8] + a[8 * g:8 * g + 8] * c
        c = hg[7:8] if not rev else hg[0:1]
        outs[g] = hg
    return jnp.concatenate(outs, axis=0), c


def _lru_body(t_len, pa_ref, cw_ref, cb_ref, wa_ref, ba_ref, wx_ref, bx_ref, lam_ref, h0_ref,
              oa_ref, hfin_ref, hf_s, ab_s, ub_s):
    nt = t_len // TILE
    single = nt == 1
    r8 = lax.broadcasted_iota(I32, (TILE, 1), 0) & 7
    n_ext = TILE + 16

    def gates(xc, xb, d):
        r = jax.nn.sigmoid(_dot(xb, wa_ref[d]) + ba_ref[d])
        i = jax.nn.sigmoid(_dot(xb, wx_ref[d]) + bx_ref[d])
        log_a = -LRU_C * r * _softplus(-lam_ref[d])
        a = jnp.exp(log_a)
        th = jnp.tanh(log_a)
        mult = jnp.sqrt(jnp.maximum(-2.0 * th / (1.0 - th), 0.0))
        return a, mult * (i * xc)

    def fwd_tile(i, c):
        t0 = pl.multiple_of(i * TILE, TILE)
        xa, ext = _halo_tile(pa_ref, 0, 256, t0, t_len, single)
        xc = cb_ref[...] + xa * cw_ref[1:2, :]
        for j in (0, 2, 3):
            xc = xc + pltpu.roll(ext, n_ext - 7 - j, 0)[0:TILE] * cw_ref[j:j + 1, :]
        xb = xc.astype(BF16)
        a_f, u_f = gates(xc, xb, 0)
        h, c = _lru_scan(a_f, u_f, c, r8, False)
        hf_s[pl.ds(t0, TILE), :] = h
        a_b, u_b = gates(xc, xb, 1)
        ab_s[pl.ds(t0, TILE), :] = a_b
        ub_s[pl.ds(t0, TILE), :] = u_b
        return c

    def bwd_tile(k, c):
        t0 = pl.multiple_of((nt - 1 - k) * TILE, TILE)
        h_b, c = _lru_scan(ab_s[pl.ds(t0, TILE), :], ub_s[pl.ds(t0, TILE), :], c, r8, True)
        gate = pa_ref[pl.ds(t0, TILE), 256:512]
        oa_ref[pl.ds(t0, TILE), :] = (hf_s[pl.ds(t0, TILE), :] + h_b) * _gelu_tanh(gate)
        return c

    h0 = h0_ref[0]
    if single:
        c_f = fwd_tile(0, h0[0:1])
        c_b = bwd_tile(0, h0[1:2])
    else:
        c_f = lax.fori_loop(0, nt, fwd_tile, h0[0:1])
        c_b = lax.fori_loop(0, nt, bwd_tile, h0[1:2])
    hfin_ref[0, 0:1, :] = c_f
    hfin_ref[0, 1:2, :] = c_b


def _lru_call(pa, t_len, nb, blk_off, layer, conv_w, conv_b, wa_bd, ba, wx_bd, bx, lam, h0):
    full2 = lambda shp: _lspec(shp, layer)
    if h0.ndim == 4:
        h0_spec = pl.BlockSpec((1, None, 2, 256), lambda b: (b, layer, 0, 0))
    else:
        h0_spec = pl.BlockSpec((1, 2, 256), lambda b: (b, 0, 0))
    return pl.pallas_call(
        functools.partial(_lru_body, t_len),
        grid=(nb,),
        in_specs=[_seq_spec(t_len, PA_W, blk_off),
                  full2((4, 256)), full2((1, 256)), full2((2, 256, 256)), full2((2, 1, 256)),
                  full2((2, 256, 256)), full2((2, 1, 256)), full2((2, 1, 256)),
                  h0_spec],
        out_specs=[pl.BlockSpec((t_len, 256), lambda b: (b, 0)),
                   pl.BlockSpec((1, 2, 256), lambda b: (b, 0, 0))],
        out_shape=[jax.ShapeDtypeStruct((nb * t_len, 256), F32), jax.ShapeDtypeStruct((nb, 2, 256), F32)],
        scratch_shapes=[pltpu.VMEM((t_len, 256), F32)] * 3,
        compiler_params=_cparams(("arbitrary",), VMEM_LIMIT),
        name="lru",
    )(pa, conv_w, conv_b, wa_bd, ba, wx_bd, bx, lam, h0)


def _pool_body(t_len, pb_ref, pw_ref, ps_ref, ob_ref):
    nt = t_len // TILE
    single = nt == 1
    n_ext = TILE + 16
    lane = lax.broadcasted_iota(I32, (1, 256), 1)
    rowi = lax.broadcasted_iota(I32, (TILE, 1), 0)

    def ahead(x, k):
        return pltpu.roll(x, n_ext - k, 0)

    def tile(i, carry):
        t0 = pl.multiple_of(i * TILE, TILE)
        xa, ext = _halo_tile(pb_ref, 0, 256, t0, t_len, single)
        p2 = ext + ahead(ext, 1)
        p4 = p2 + ahead(p2, 2)
        p8 = p4 + ahead(p4, 4)
        p16 = p8 + ahead(p8, 8)
        sums = (ahead(p2, 7)[0:TILE], ahead(p4, 6)[0:TILE], ahead(p8, 4)[0:TILE], p16[0:TILE])
        tpos = t0 + rowi
        means = []
        for w, s in zip((2, 4, 8, 16), sums):
            cnt = jnp.minimum(tpos + w // 2, t_len) - jnp.maximum(tpos - w // 2, 0)
            means.append(s / cnt.astype(F32))
        mean = jnp.where(lane < 64, means[0], jnp.where(lane < 128, means[1],
                                                          jnp.where(lane < 192, means[2], means[3])))
        pooled = mean - xa
        ob_ref[pl.ds(t0, TILE), :] = _dot(pooled.astype(BF16), pw_ref[...]) * ps_ref[...]
        return carry

    if single:
        tile(0, 0)
    else:
        lax.fori_loop(0, nt, tile, 0)


def _pool_call(pb, t_len, nb, blk_off, layer, pw_bd, pscale):
    return pl.pallas_call(
        functools.partial(_pool_body, t_len),
        grid=(nb,),
        in_specs=[_seq_spec(t_len, PB_W, blk_off), _lspec((256, 256), layer), _lspec((1, 256), layer)],
        out_specs=pl.BlockSpec((t_len, 256), lambda b: (b, 0)),
        out_shape=jax.ShapeDtypeStruct((nb * t_len, 256), F32),
        compiler_params=_cparams(("arbitrary",), VMEM_LIMIT),
        name="pool",
    )(pb, pw_bd, pscale)


HGRN_LEVELS = (1, 2, 4, 8, 16, 32, 64, 128)


def _hgrn_dir(rev, q, k, v, lf, st_s, att_s, ones_ref, pm_ref, rowi, r8, lane_head):
    c_rows = TILE
    half = TILE // 2
    g = _cumsum_rows(lf, r8, rev)
    vb = v.astype(BF16)
    seg = g
    head_on = [jnp.where(lane_head == h, 1.0, 0.0).astype(BF16) for h in range(4)]
    for li, m in enumerate(HGRN_LEVELS):
        up = (rowi & (2 * m - 1)) >= m
        if not rev:
            ref_q = pltpu.roll(seg, m, 0)
            qsel, ksel = up, jnp.logical_not(up)
            seg_next = jnp.where(up, seg, pltpu.roll(seg, c_rows - m, 0))
        else:
            ref_q = pltpu.roll(seg, c_rows - m, 0)
            qsel, ksel = jnp.logical_not(up), up
            seg_next = jnp.where(up, pltpu.roll(seg, m, 0), seg)
        qp = jnp.where(qsel, q * jnp.exp(g - ref_q), 0.0).astype(BF16)
        kp = jnp.where(ksel, k * jnp.exp(seg - g), 0.0).astype(BF16)

        def scores(qh, kh):
            qs = jnp.concatenate([qp[half * qh:half * (qh + 1)] * head_on[h] for h in range(4)], axis=0)
            return _dot_nt(qs, kp[half * kh:half * (kh + 1)])

        if 2 * m == c_rows:
            cross_q, cross_k = (1, 0) if not rev else (0, 1)
            cross = scores(cross_q, cross_k)
        else:
            for b in range(2):
                prod = scores(b, b)
                for h in range(4):
                    rows = slice(h * half, (h + 1) * half)
                    att_s[li, b, rows, :] = prod[rows] * pm_ref[li]
        seg = seg_next
    vm = [[vb[half * b:half * (b + 1)] * head_on[h] for h in range(4)] for b in range(2)]
    o_halves = []
    for b in range(2):
        ob = None
        for h in range(4):
            rows = slice(h * half, (h + 1) * half)
            att = att_s[0, b, rows, :]
            for li in range(1, len(HGRN_LEVELS) - 1):
                att = att + att_s[li, b, rows, :]
            term = _dot(att.astype(BF16), vm[b][h])
            if b == cross_q:
                term = term + _dot(cross[rows].astype(BF16), vm[cross_k][h])
            ob = term if ob is None else ob + term
        o_halves.append(ob)
    o = _dot((q * k).astype(BF16), ones_ref[...]) * v + jnp.concatenate(o_halves, axis=0)
    st = st_s[...]
    o = o + _dot_nt((q * jnp.exp(g)).astype(BF16), st.astype(BF16))
    g_end = g[c_rows - 1:c_rows] if not rev else g[0:1]
    kd = k * jnp.exp(g_end - g)
    upd = _dot_tn(vb, kd.astype(BF16))
    blk = (lax.broadcasted_iota(I32, (256, 1), 0) >> 6) == (lax.broadcasted_iota(I32, (1, 256), 1) >> 6)
    st_s[...] = st * jnp.exp(g_end) + jnp.where(blk, upd, 0.0)
    return o


def _hgrn_body(t_len, ctx_pass, pc_ref, lb_ref, nw_ref, s_ref, ones_ref, pm_ref, oc_ref, *rest):
    if ctx_pass:
        sfin_ref, of_s, st_s, att_s = rest
    else:
        of_s, st_s, att_s = rest
    nt = t_len // TILE
    rowi = lax.broadcasted_iota(I32, (TILE, 1), 0)
    r8 = rowi & 7
    lane_head = lax.broadcasted_iota(I32, (1, 256), 1) >> 6

    def load(t0, d):
        q = _silu(pc_ref[pl.ds(t0, TILE), 0:256]) * 0.125
        f_raw = pc_ref[pl.ds(t0, TILE), 256 * (1 + d):256 * (2 + d)]
        v = pc_ref[pl.ds(t0, TILE), 768:1024]
        lb = lb_ref[d]
        f_val = lb + (1.0 - lb) * jax.nn.sigmoid(f_raw)
        lf = jnp.log(jnp.maximum(f_val, TINY))
        return q, 1.0 - f_val, v, lf

    def fwd_tile(i, carry):
        t0 = pl.multiple_of(i * TILE, TILE)
        q, k, v, lf = load(t0, 0)
        of_s[pl.ds(t0, TILE), :] = _hgrn_dir(False, q, k, v, lf, st_s, att_s, ones_ref, pm_ref, rowi, r8, lane_head)
        return carry

    def bwd_tile(kk, carry):
        t0 = pl.multiple_of((nt - 1 - kk) * TILE, TILE)
        q, k, v, lf = load(t0, 1)
        o = of_s[pl.ds(t0, TILE), :] + _hgrn_dir(True, q, k, v, lf, st_s, att_s, ones_ref, pm_ref, rowi, r8, lane_head)
        ms = _segsum2(o * o, ones_ref[...]) * (1.0 / 64.0)
        y = o * lax.rsqrt(ms + EPS) * nw_ref[...]
        oc_ref[pl.ds(t0, TILE), :] = y * _silu(pc_ref[pl.ds(t0, TILE), 1024:1280])
        return carry

    def put_state(d):
        st = st_s[...]
        hi = st.astype(BF16)
        r1 = st - hi.astype(F32)
        mid = r1.astype(BF16)
        lo = (r1 - mid.astype(F32)).astype(BF16)
        sfin_ref[0, d] = _dot(hi, s_ref[...]) + _dot(mid, s_ref[...]) + _dot(lo, s_ref[...])

    for d, tile_fn in ((0, fwd_tile), (1, bwd_tile)):
        st_s[...] = jnp.zeros((256, 256), F32) if ctx_pass else s_ref[0, d]
        if nt == 1:
            tile_fn(0, 0)
        else:
            lax.fori_loop(0, nt, tile_fn, 0, unroll=2)
        if ctx_pass:
            put_state(d)


def _pair_masks():
    i = np.arange(TILE // 2)
    return np.stack([((i[:, None] // (2 * m)) == (i[None, :] // (2 * m))).astype(np.float32)
                     for m in HGRN_LEVELS[:-1]])


def _hgrn_call(pc, t_len, nb, blk_off, layer, lower, normw, s0t, ones64):
    n_lv = len(HGRN_LEVELS) - 1
    ctx_pass = s0t is None
    out_specs = [pl.BlockSpec((t_len, 256), lambda b: (b, 0))]
    out_shape = [jax.ShapeDtypeStruct((nb * t_len, 256), F32)]
    if ctx_pass:
        fold = np.zeros((256, 128), np.float32)
        fold[np.arange(256), np.arange(256) % 64] = 1.0
        s_arg, s_spec = jnp.asarray(fold, BF16), pl.BlockSpec((256, 128), lambda b: (0, 0))
        out_specs.append(pl.BlockSpec((1, 2, 256, 128), lambda b: (b, 0, 0, 0)))
        out_shape.append(jax.ShapeDtypeStruct((nb, 2, 256, 128), F32))
    else:
        s_arg, s_spec = s0t, pl.BlockSpec((1, None, 2, 256, 256), lambda b: (b, layer, 0, 0, 0))
    return pl.pallas_call(
        functools.partial(_hgrn_body, t_len, ctx_pass),
        grid=(nb,),
        in_specs=[_seq_spec(t_len, PC_W, blk_off),
                  _lspec((2, 1, 256), layer),
                  _lspec((1, 256), layer),
                  s_spec,
                  pl.BlockSpec((256, 256), lambda b: (0, 0)),
                  pl.BlockSpec((n_lv, TILE // 2, TILE // 2), lambda b: (0, 0, 0))],
        out_specs=out_specs,
        out_shape=out_shape,
        scratch_shapes=[pltpu.VMEM((t_len, 256), F32), pltpu.VMEM((256, 256), F32),
                        pltpu.VMEM((len(HGRN_LEVELS) - 1, 2, 2 * TILE, TILE // 2), F32)],
        compiler_params=_cparams(("arbitrary",), VMEM_LIMIT),
        name="hgrn",
    )(pc, lower, normw, s_arg, ones64, jnp.asarray(_pair_masks()))


ATT_SCALE = 96.0 ** -0.5
KEY_BLK = 512
EXP_SAFE = 40.0


def _rope512(x, cs128, sn128, lane128):
    cs = jnp.concatenate([cs128] * 4, axis=1)
    sn = jnp.concatenate([sn128] * 4, axis=1)
    partner = jnp.where(lane128 < 80, pltpu.roll(x, 512 - 16, 1), pltpu.roll(x, 16, 1))
    return x * cs + partner * sn


def _mla_body(t_len, n_ctx, use_rope, pd_ref, ckv_c_ref, kr_c_ref, cs_ref, sn_ref, qnorm_ref, wuq_ref, qnw_ref,
              kvnorm_ref, wukv_ref, knw_ref, ones_ref, od_ref, ckvn_ref, k_s, v_s, m_s, l_s, acc_s):
    nt = t_len // TILE
    t_k = n_ctx + t_len
    assert (t_k - TILE) % KEY_BLK == 0
    n_kb = (t_k - TILE) // KEY_BLK
    lane128 = lax.broadcasted_iota(I32, (1, 512), 1) & 127
    lane_head = lax.broadcasted_iota(I32, (1, 256), 1) >> 6

    def head_norm(x, w_ref):
        ss = _segsum2(x * x, ones_ref[...])
        return x * lax.rsqrt(ss * (1.0 / 96.0) + EPS) * w_ref[...]

    def put_kv(r0, ckv_n, kr128, rope_rows):
        kv = _dot(ckv_n.astype(BF16), wukv_ref[...])
        k_all = kv[:, 0:512] + jnp.concatenate([kr128] * 4, axis=1)
        kn = head_norm(k_all, knw_ref)
        if rope_rows is not None:
            kn = _rope512(kn, cs_ref[pl.ds(rope_rows, TILE), :], sn_ref[pl.ds(rope_rows, TILE), :], lane128)
        k_s[pl.ds(r0, TILE), :] = kn.astype(BF16)
        v_s[pl.ds(r0, TILE), :] = kv[:, 512:768].astype(BF16)

    if n_ctx:
        put_kv(0, ckv_c_ref[0], kr_c_ref[0], None)

    def kv_tile(i, carry):
        t0 = pl.multiple_of(i * TILE, TILE)
        ckv_n = _rms(pd_ref[pl.ds(t0, TILE), 256:384], kvnorm_ref[...])
        ckvn_ref[pl.ds(t0, TILE), :] = ckv_n
        put_kv(pl.multiple_of(n_ctx + t0, TILE), ckv_n, pd_ref[pl.ds(t0, TILE), 512:640],
               t0 if use_rope else None)
        return carry

    if nt == 1:
        kv_tile(0, 0)
    else:
        lax.fori_loop(0, nt, kv_tile, 0)

    def q_tile(bounded, i, carry):
        t0 = pl.multiple_of(i * TILE, TILE)
        qn = _rms(pd_ref[pl.ds(t0, TILE), 0:256], qnorm_ref[...])
        q = head_norm(_dot(qn.astype(BF16), wuq_ref[...]), qnw_ref)
        if use_rope:
            q = _rope512(q, cs_ref[pl.ds(t0, TILE), :], sn_ref[pl.ds(t0, TILE), :], lane128)
        qb = (q * ATT_SCALE).astype(BF16)
        qhs = [qb[:, 128 * h:128 * (h + 1)] for h in range(4)]

        def keys(h, rows):
            return k_s[rows, 128 * h:128 * (h + 1)]

        def vals(h, rows):
            return v_s[rows, :]

        first = slice(0, TILE)

        def first_block(h):
            s = _dot_nt(qhs[h], keys(h, first))
            m = jnp.max(s, axis=1, keepdims=True)
            p = jnp.exp(s - m)
            return m, jnp.sum(p, axis=1, keepdims=True), _dot(p.astype(BF16), vals(h, first))

        if n_kb == 0:
            outs = []
            for h in range(4):
                _, l, acc = first_block(h)
                outs.append(acc / l)
        elif bounded:
            for h in range(4):
                p = jnp.exp(_dot_nt(qhs[h], keys(h, first)))
                l_s[h] = jnp.broadcast_to(jnp.sum(p, axis=1, keepdims=True), (TILE, 128))
                acc_s[h] = _dot(p.astype(BF16), vals(h, first))

            def kblock_bounded(j, c2):
                rows = pl.ds(pl.multiple_of(TILE + j * KEY_BLK, TILE), KEY_BLK)
                for h in range(4):
                    p = jnp.exp(_dot_nt(qhs[h], keys(h, rows)))
                    l_s[h] = l_s[h] + jnp.sum(p, axis=1, keepdims=True)
                    acc_s[h] = acc_s[h] + _dot(p.astype(BF16), vals(h, rows))
                return c2

            lax.fori_loop(0, n_kb, kblock_bounded, 0, unroll=True)
            outs = [acc_s[h] / jnp.concatenate([l_s[h], l_s[h]], axis=1) for h in range(4)]
        else:
            for h in range(4):
                m, l, acc = first_block(h)
                m_s[h] = jnp.broadcast_to(m, (TILE, 128))
                l_s[h] = jnp.broadcast_to(l, (TILE, 128))
                acc_s[h] = acc

            def kblock(j, c2):
                rows = pl.ds(pl.multiple_of(TILE + j * KEY_BLK, TILE), KEY_BLK)
                for h in range(4):
                    s = _dot_nt(qhs[h], keys(h, rows))
                    m_prev = m_s[h]
                    m_new = jnp.maximum(m_prev, jnp.max(s, axis=1, keepdims=True))
                    alpha = jnp.exp(m_prev - m_new)
                    p = jnp.exp(s - jnp.concatenate([m_new] * (KEY_BLK // 128), axis=1))
                    l_s[h] = alpha * l_s[h] + jnp.sum(p, axis=1, keepdims=True)
                    acc_s[h] = (jnp.concatenate([alpha, alpha], axis=1) * acc_s[h]
                                + _dot(p.astype(BF16), vals(h, rows)))
                    m_s[h] = m_new
                return c2

            lax.fori_loop(0, n_kb, kblock, 0)
            outs = [acc_s[h] / jnp.concatenate([l_s[h], l_s[h]], axis=1) for h in range(4)]
        o = outs[3]
        for h in range(3):
            o = jnp.where(lane_head == h, outs[h], o)
        od_ref[pl.ds(t0, TILE), :] = o
        return carry

    if n_kb == 0:
        if nt == 1:
            q_tile(False, 0, 0)
        else:
            lax.fori_loop(0, nt, functools.partial(q_tile, False), 0)
    else:
        score_bound = (96.0 * ATT_SCALE) * jnp.max(jnp.abs(qnw_ref[...])) * jnp.max(jnp.abs(knw_ref[...]))
        safe = score_bound < EXP_SAFE

        @pl.when(safe)
        def _():
            lax.fori_loop(0, nt, functools.partial(q_tile, True), 0)

        @pl.when(jnp.logical_not(safe))
        def _():
            lax.fori_loop(0, nt, functools.partial(q_tile, False), 0)


def _mla_call(pd, t_len, nb, blk_off, layer, n_ctx, use_rope, ckv_c, kr_c, cs, sn, wts):
    qnorm, wuq, qnw, kvnorm, wukv, knw, ones128 = wts
    t_k = n_ctx + t_len
    c2 = lambda shp: pl.BlockSpec(shp, lambda b: (0,) * len(shp))
    lw = lambda shp: _lspec(shp, layer)
    if ckv_c.ndim == 4:
        ctx_spec = pl.BlockSpec((1, None, 256, 128), lambda b: (b, layer, 0, 0))
    else:
        ctx_spec = pl.BlockSpec((1, 256, 128), lambda b: (b, 0, 0))
    return pl.pallas_call(
        functools.partial(_mla_body, t_len, n_ctx, use_rope),
        grid=(nb,),
        in_specs=[_seq_spec(t_len, PD_W, blk_off), ctx_spec, ctx_spec,
                  c2((t_len, 128)), c2((t_len, 128)),
                  lw((1, 256)), lw((256, 512)), lw((1, 512)), lw((1, 128)), lw((128, 768)), lw((1, 512)),
                  c2((512, 512))],
        out_specs=[pl.BlockSpec((t_len, 256), lambda b: (b, 0)),
                   pl.BlockSpec((t_len, 128), lambda b: (b, 0))],
        out_shape=[jax.ShapeDtypeStruct((nb * t_len, 256), F32), jax.ShapeDtypeStruct((nb * t_len, 128), F32)],
        scratch_shapes=[pltpu.VMEM((t_k, 512), BF16), pltpu.VMEM((t_k, 256), BF16),
                        pltpu.VMEM((4, TILE, 128), F32), pltpu.VMEM((4, TILE, 128), F32),
                        pltpu.VMEM((4, TILE, 256), F32)],
        compiler_params=_cparams(("arbitrary",), VMEM_LIMIT),
        name="mla",
    )(pd, ckv_c, kr_c, cs, sn, qnorm, wuq, qnw, kvnorm, wukv, knw, ones128)


def _out_body(n_ctx_steps, *refs):
    mix_refs = refs[:8]
    (x_ref, g1_ref, sc_ref, sh_ref, nw_ref, wout_ref, wrh_ref, wrl_ref,
     x1_ref, h2e_ref, aff_ref) = refs[8:]
    is_ctx = pl.program_id(0) < n_ctx_steps
    lane = lax.broadcasted_iota(I32, (1, 128), 1)
    rc = 256
    for r0 in range(0, x_ref.shape[0], rc):
        rows = slice(r0, r0 + rc)
        m = None
        for k in range(4):
            ok = jnp.where(is_ctx, mix_refs[2 * k][rows, :], mix_refs[2 * k + 1][rows, :]).astype(BF16)
            mk = _dot(ok, wout_ref[256 * k:256 * (k + 1), :])
            m = mk if m is None else m + mk
        x1 = x_ref[rows, :] + g1_ref[0] * m
        x1_ref[rows, :] = x1
        h2 = _rms(x1, nw_ref[...]) * (1.0 + sc_ref[0]) + sh_ref[0]
        hh, hl = _split_bf16(h2)
        lg = _dot(hh, wrh_ref[...]) + _dot(hl, wrh_ref[...]) + _dot(hh, wrl_ref[...])
        lg = jnp.where(lane < N_EXP, lg, -jnp.inf)
        ex = jnp.exp(lg - jnp.max(lg, axis=1, keepdims=True))
        aff = ex / jnp.sum(ex, axis=1, keepdims=True)
        a_hi = aff.astype(BF16).astype(F32)
        a_lo = aff - a_hi
        ext = a_hi + pltpu.roll(a_lo, N_EXP, 1)
        h2e_ref[rows, 0:D] = hh
        h2e_ref[rows, D:HEXT] = ext.astype(BF16)
        aff_ref[rows, :] = aff


def _out_call(mix, x, layer, mod, norm2, wout_b, wr_h, wr_l):
    tm = 512
    n = NTOK // tm
    per_seg = SEG // tm
    n_ctx_steps = mix[0].shape[0] // tm
    row = lambda i: (i, 0)
    c2 = lambda shp: _lspec(shp, layer)
    ctx_spec = pl.BlockSpec((tm, 256), lambda i: (jnp.minimum(i, n_ctx_steps - 1), 0))
    lat_spec = pl.BlockSpec((tm, 256), lambda i: (jnp.maximum(i - n_ctx_steps, 0), 0))
    return pl.pallas_call(
        functools.partial(_out_body, n_ctx_steps),
        grid=(n,),
        in_specs=[ctx_spec, lat_spec] * 4 + [pl.BlockSpec((tm, D), row), _mod_spec(layer, 2, per_seg),
                  _mod_spec(layer, 4, per_seg), _mod_spec(layer, 3, per_seg),
                  c2((1, D)), c2((D, D)), c2((D, 128)), c2((D, 128))],
        out_specs=[pl.BlockSpec((tm, D), row), pl.BlockSpec((tm, HEXT), row), pl.BlockSpec((tm, 128), row)],
        out_shape=[jax.ShapeDtypeStruct((NTOK, D), F32), jax.ShapeDtypeStruct((NTOK, HEXT), BF16),
                   jax.ShapeDtypeStruct((NTOK, 128), F32)],
        compiler_params=_cparams(("arbitrary",), VMEM_LIMIT),
        name="out_proj",
    )(*mix, x, mod, mod, mod, norm2, wout_b, wr_h, wr_l)


def _sel_body(n_grp, cap, aff_ref, tri_ref, segt_ref, slot_ref, cum_ref, pref_s):
    w = SEG // n_grp
    nblk = SEG // 256
    aff = jnp.transpose(aff_ref[...])[0:N_EXP, :]
    pref_s[...] = jnp.zeros((N_EXP, SEG), I32)

    def grp_cols(fn):
        return jnp.concatenate([jnp.broadcast_to(fn(g), (N_EXP, w)) for g in range(n_grp)], axis=1)

    def it(i, carry):
        bit = lax.shift_left(jnp.int32(1), 30 - i)
        cand = pref_s[...] | bit
        ge = jnp.where(aff >= pltpu.bitcast(cand, F32), 1.0, 0.0)
        ok = grp_cols(lambda g: jnp.where(
            jnp.sum(ge[:, g * w:(g + 1) * w], axis=1, keepdims=True) >= cap, 1.0, 0.0))
        pref_s[...] = jnp.where(ok > 0.5, cand, pref_s[...])
        return carry

    lax.fori_loop(0, 31, it, 0)
    thr = pref_s[...]

    def grp_cumsum(x):
        outs, off = [], None
        for b in range(nblk):
            loc = _dot(x[:, 256 * b:256 * (b + 1)].astype(BF16), tri_ref[...])
            if (256 * b) % w == 0:
                off = None
            if off is not None:
                loc = loc + off
            off = loc[:, 255:256]
            outs.append(loc)
        return jnp.concatenate(outs, axis=1)

    gt = jnp.where(aff >= pltpu.bitcast(thr + 1, F32), 1.0, 0.0)
    eq = jnp.where(aff >= pltpu.bitcast(thr, F32), 1.0, 0.0) - gt
    room = grp_cols(lambda g: cap - jnp.sum(gt[:, g * w:(g + 1) * w], axis=1, keepdims=True))
    sel = jnp.where((gt > 0.5) | ((eq > 0.5) & (grp_cumsum(eq) <= room)), 1.0, 0.0)
    base = grp_cols(lambda g: jnp.full((N_EXP, 1), float(g * cap), F32))
    slot = base + grp_cumsum(sel) - 1.0
    slot_ref[0] = jnp.where(sel > 0.5, slot, -1.0).astype(I32)
    cum_ref[0] = _dot(sel.astype(BF16), segt_ref[...]).astype(I32)


def _sel_call(afft, n_seg, seg_off, n_grp, cap, tri, segt):
    c2 = lambda shp: pl.BlockSpec(shp, lambda s: (0,) * len(shp))
    return pl.pallas_call(
        functools.partial(_sel_body, n_grp, cap),
        grid=(n_seg,),
        in_specs=[pl.BlockSpec((SEG, 128), lambda s: (s + seg_off, 0)), c2((256, 256)), c2((SEG, 128))],
        out_specs=[pl.BlockSpec((1, N_EXP, SEG), lambda s: (s, 0, 0)),
                   pl.BlockSpec((1, N_EXP, 128), lambda s: (s, 0, 0))],
        out_shape=[jax.ShapeDtypeStruct((n_seg, N_EXP, SEG), I32), jax.ShapeDtypeStruct((n_seg, N_EXP, 128), I32)],
        scratch_shapes=[pltpu.VMEM((N_EXP, SEG), I32)],
        compiler_params=_cparams(("arbitrary",), VMEM_LIMIT),
        name="select",
    )(afft, tri, segt)


def _windows(cum_ref, s, e, tk):
    lo = cum_ref[s, e, tk]
    hi = cum_ref[s, e, tk + 1]
    w0 = lax.shift_left(lax.shift_right_logical(lo, 4), 4)
    nw = jnp.where(hi > lo, lax.shift_right_logical(hi - w0 + (WIN - 1), WIN_SHIFT), 0)
    return w0, nw


def _onehot(base, slot_row):
    rows = base + lax.broadcasted_iota(I32, (WIN, 1), 0)
    return jnp.where(rows == slot_row, 1.0, 0.0).astype(BF16)


GATHER_EXPERTS = 16
GATHER_TILES = 2


def _gather_body(cum_ref, slot_ref, h2e_ref, g_ref):
    s, half, step = pl.program_id(0), pl.program_id(1), pl.program_id(2)

    @pl.when(step == 0)
    def _():
        g_ref[...] = jnp.zeros(g_ref.shape, BF16)

    for sub in range(GATHER_TILES):
        tk = step * GATHER_TILES + sub
        cols = slice(sub * TK, (sub + 1) * TK)
        wins = [_windows(cum_ref, s, half * GATHER_EXPERTS + j, tk) for j in range(GATHER_EXPERTS)]
        bases = [pl.multiple_of(w0, 16) for w0, _ in wins]
        sel = jnp.concatenate([_onehot(bases[j], slot_ref[0, j:j + 1, cols]) for j in range(GATHER_EXPERTS)], axis=0)
        got = _dot(sel, h2e_ref[cols, :]).astype(BF16)
        for j in range(GATHER_EXPERTS):
            g_ref[0, j, pl.ds(bases[j], WIN), :] = (g_ref[0, j, pl.ds(bases[j], WIN), :]
                                                      + got[j * WIN:(j + 1) * WIN, :])
        for j in range(GATHER_EXPERTS):
            def wbody(w, carry, j=j, wins=wins, cols=cols):
                base = pl.multiple_of(wins[j][0] + w * WIN, 16)
                more = _dot(_onehot(base, slot_ref[0, j:j + 1, cols]), h2e_ref[cols, :])
                g_ref[0, j, pl.ds(base, WIN), :] = g_ref[0, j, pl.ds(base, WIN), :] + more.astype(BF16)
                return carry

            lax.fori_loop(1, wins[j][1], wbody, 0)


def _gather_call(cum, slot, h2e):
    n_half = N_EXP // GATHER_EXPERTS
    rows = TK * GATHER_TILES
    return pl.pallas_call(
        _gather_body,
        grid_spec=pltpu.PrefetchScalarGridSpec(
            num_scalar_prefetch=1, grid=(NSEG, n_half, SEG // rows),
            in_specs=[pl.BlockSpec((1, GATHER_EXPERTS, rows), lambda s, h, t, c: (s, h, t)),
                      pl.BlockSpec((rows, HEXT), lambda s, h, t, c: (s * (SEG // rows) + t, 0))],
            out_specs=pl.BlockSpec((1, GATHER_EXPERTS, SLOT_PAD, HEXT), lambda s, h, t, c: (s, h, 0, 0),
                                   pipeline_mode=pl.Buffered(1))),
        out_shape=jax.ShapeDtypeStruct((NSEG, N_EXP, SLOT_PAD, HEXT), BF16),
        compiler_params=_cparams(("arbitrary", "arbitrary", "arbitrary"), VMEM_LIMIT),
        name="moe_gather",
    )(cum, slot, h2e)


def _ffn_body(g_ref, wg_ref, wu_ref, wd_ref, y_ref, wgb_s, wub_s, wdb_s):
    e = pl.program_id(0)
    wgb_s[...] = wg_ref[0, 0].astype(BF16)
    wub_s[...] = wu_ref[0, 0].astype(BF16)
    wdb_s[...] = wd_ref[0, 0].astype(BF16)
    lane = lax.broadcasted_iota(I32, (1, 128), 1)
    pick = (lane == e) | (lane == e + N_EXP)
    for s in range(NSEG):
        xs = g_ref[s, 0, :, 0:D]
        ext = g_ref[s, 0, :, D:HEXT].astype(F32)
        gate = jnp.sum(jnp.where(pick, ext, 0.0), axis=1, keepdims=True)
        a = _dot(xs, wgb_s[...])
        u = _dot(xs, wub_s[...])
        y = _dot((_silu(a) * u).astype(BF16), wdb_s[...]) * gate
        y_ref[s, 0, 0:CAP_SEG, :] = y.astype(BF16)
        y_ref[s, 0, CAP_SEG:SLOT_PAD, :] = jnp.zeros((SLOT_PAD - CAP_SEG, D), BF16)


def _ffn_call(gath, layer, wg, wu, wd):
    wspec = pl.BlockSpec((1, 1, D, D), lambda e: (layer, e, 0, 0))
    return pl.pallas_call(
        _ffn_body,
        grid=(N_EXP,),
        in_specs=[pl.BlockSpec((NSEG, 1, CAP_SEG, HEXT), lambda e: (0, e, 0, 0)), wspec, wspec, wspec],
        out_specs=pl.BlockSpec((NSEG, 1, SLOT_PAD, D), lambda e: (0, e, 0, 0)),
        out_shape=jax.ShapeDtypeStruct((NSEG, N_EXP, SLOT_PAD, D), BF16),
        scratch_shapes=[pltpu.VMEM((D, D), BF16)] * 3,
        compiler_params=_cparams(("arbitrary",), VMEM_LIMIT),
        name="moe_ffn",
    )(gath, wg, wu, wd)


def _scatter_body(split, cum_ref, slot_ref, y_ref, x1_ref, g2_ref, *o_refs):
    s, tk = pl.program_id(0), pl.program_id(1)
    wins = [_windows(cum_ref, s, e, tk) for e in range(N_EXP)]
    bases = [pl.multiple_of(w0, 16) for w0, _ in wins]
    g2 = g2_ref[0]

    def run(o_ref):
        sel = jnp.concatenate([_onehot(bases[e], slot_ref[0, e:e + 1, :]) for e in range(N_EXP)], axis=0)
        rows = jnp.concatenate([y_ref[0, e, pl.ds(bases[e], WIN), :] for e in range(N_EXP)], axis=0)
        o_ref[...] = x1_ref[...] + g2 * _dot_tn(sel, rows)
        for e in range(N_EXP):
            def wbody(w, carry, e=e):
                base = pl.multiple_of(wins[e][0] + w * WIN, 16)
                o_ref[...] = o_ref[...] + g2 * _dot_tn(_onehot(base, slot_ref[0, e:e + 1, :]),
                                                       y_ref[0, e, pl.ds(base, WIN), :])
                return carry

            lax.fori_loop(1, wins[e][1], wbody, 0)

    if not split:
        run(o_refs[0])
    else:
        pl.when(s == 0)(lambda: run(o_refs[0]))
        pl.when(s > 0)(lambda: run(o_refs[1]))


def _scatter_call(cum, slot, y, x1, layer, mod, split):
    n_t = SEG // TK
    tok_idx = lambda s, t, c: (s * n_t + t, 0)
    if split:
        out_specs = [pl.BlockSpec((TK, D), lambda s, t, c: (jnp.where(s == 0, t, n_t - 1), 0)),
                     pl.BlockSpec((TK, D), lambda s, t, c: (jnp.maximum((s - 1) * n_t + t, 0), 0))]
        out_shape = [jax.ShapeDtypeStruct((SEG, D), F32), jax.ShapeDtypeStruct((NTOK - SEG, D), F32)]
    else:
        out_specs = pl.BlockSpec((TK, D), tok_idx)
        out_shape = jax.ShapeDtypeStruct((NTOK, D), F32)
    return pl.pallas_call(
        functools.partial(_scatter_body, split),
        grid_spec=pltpu.PrefetchScalarGridSpec(
            num_scalar_prefetch=1, grid=(NSEG, n_t),
            in_specs=[pl.BlockSpec((1, N_EXP, TK), lambda s, t, c: (s, 0, t)),
                      pl.BlockSpec((1, N_EXP, SLOT_PAD, D), lambda s, t, c: (s, 0, 0, 0)),
                      pl.BlockSpec((TK, D), tok_idx),
                      pl.BlockSpec((None, None, 1, 1, D), lambda s, t, c: (layer, 5, s, 0, 0))],
            out_specs=out_specs),
        out_shape=out_shape,
        compiler_params=_cparams(("arbitrary", "arbitrary"), VMEM_LIMIT),
        name="moe_scatter",
    )(cum, slot, y, x1, mod)


def _block_diag4(w):
    eye = jnp.eye(4, dtype=w.dtype)
    return jnp.einsum('...hij,hg->...higj', w, eye).reshape(w.shape[:-3] + (256, 256))


def _np_block_ones(n, blk):
    i = np.arange(n) // blk
    return (i[:, None] == i[None, :]).astype(np.float32)


def _head_cols():
    j = np.arange(128)
    src = np.full(128, -1)
    src[:64] = j[:64]
    src[64:80] = 64 + 2 * (j[64:80] - 64)
    src[80:96] = 64 + 2 * (j[80:96] - 80) + 1
    return src


def _mla_weights(mla_q_norm, mla_w_uq, mla_kv_norm, mla_w_ukv, mla_qn, mla_kn):
    depth = mla_w_uq.shape[0]
    src = _head_cols()
    valid = src >= 0
    srcc = np.where(valid, src, 0)
    colq = np.concatenate([h * 96 + srcc for h in range(4)])
    maskq = jnp.asarray(np.tile(valid, 4).astype(np.float32))
    wuq = (mla_w_uq[:, :, colq] * maskq).astype(BF16)
    qnw = (jnp.tile(mla_qn[:, srcc], (1, 4)) * maskq).reshape(depth, 1, 512)
    knw = (jnp.tile(mla_kn[:, srcc], (1, 4)) * maskq).reshape(depth, 1, 512)
    jn = np.arange(128)
    nope_valid = jn < 64
    colk = np.concatenate([h * 128 + np.where(nope_valid, jn, 0) for h in range(4)])
    maskk = jnp.asarray(np.tile(nope_valid, 4).astype(np.float32))
    colv = np.concatenate([h * 128 + 64 + np.arange(64) for h in range(4)])
    wukv = jnp.concatenate([mla_w_ukv[:, :, colk] * maskk, mla_w_ukv[:, :, colv]], axis=2).astype(BF16)
    return (mla_q_norm.reshape(depth, 1, 256), wuq, qnw, mla_kv_norm.reshape(depth, 1, 128), wukv, knw,
            jnp.asarray(_np_block_ones(512, 128), BF16))


def _krope128(kr):
    z64 = jnp.zeros(kr.shape[:-1] + (64,), kr.dtype)
    z32 = jnp.zeros(kr.shape[:-1] + (32,), kr.dtype)
    return jnp.concatenate([z64, kr[..., 0::2], kr[..., 1::2], z32], axis=-1)


def _rope_tables(n_tokens, grid_w):
    rows = (np.arange(n_tokens) // grid_w).astype(np.float32)
    cols = (np.arange(n_tokens) % grid_w).astype(np.float32)
    n_freq = 8
    inv = jnp.asarray(10000.0, F32) ** (-jnp.arange(n_freq, dtype=F32) / n_freq)
    ang = jnp.concatenate([jnp.asarray(rows)[:, None] * inv, jnp.asarray(cols)[:, None] * inv], axis=-1)
    cos, sin = jnp.cos(ang), jnp.sin(ang)
    one = jnp.ones((n_tokens, 64), F32)
    zero = jnp.zeros((n_tokens, 64), F32)
    cs = jnp.concatenate([one, cos, cos, one[:, :32]], axis=1)
    sn = jnp.concatenate([zero, -sin, sin, zero[:, :32]], axis=1)
    return cs, sn


def _in_weight(w_in):
    kr = w_in[..., 2432:2464]
    z96 = jnp.zeros(w_in.shape[:-1] + (96,), w_in.dtype)
    return jnp.concatenate([w_in, z96, _krope128(kr)], axis=-1).astype(BF16)


def kernel(x_prompt, x_sample, cache_mla_ckv, cache_mla_krope, state_rglru, state_hgrn, c, c_ctx, norm1_w, norm2_w, w_ada, b_ada, w_in, conv_w, conv_b, lru_wa, lru_ba, lru_wx, lru_bx, lru_lambda, pool_w, pool_scale, hgrn_lower_bounds, hgrn_norm_w, mla_q_norm, mla_w_uq, mla_kv_norm, mla_w_ukv, mla_qk_norm_q, mla_qk_norm_k, w_out, w_router, w_exp_gate, w_exp_up, w_exp_down):
    nbp, t_p = x_prompt.shape[0], x_prompt.shape[1]
    nbs, t_s = x_sample.shape[0], x_sample.shape[1]
    depth = w_in.shape[0]

    n_p = nbp * t_p
    cond8 = jnp.concatenate([c_ctx[None], c, jnp.zeros((5, D), F32)], axis=0)
    mod = _ada_call(cond8, w_ada, b_ada)
    mod = jnp.swapaxes(mod.reshape(depth, 8, 6, 1, D), 1, 2)

    f32 = lambda a: a.astype(F32)
    norm1 = norm1_w.reshape(depth, 1, D)
    norm2 = norm2_w.reshape(depth, 1, D)
    w_in_b = _in_weight(w_in)
    lru_w = (conv_w, conv_b.reshape(depth, 1, 256),
             _block_diag4(lru_wa).astype(BF16), lru_ba.reshape(depth, 2, 1, 256),
             _block_diag4(lru_wx).astype(BF16), lru_bx.reshape(depth, 2, 1, 256),
             lru_lambda.reshape(depth, 2, 1, 256))
    pw_bd = _block_diag4(pool_w).astype(BF16)
    pscale = pool_scale.reshape(depth, 1, 256)
    lb_soft = jax.nn.softmax(f32(hgrn_lower_bounds), axis=1)
    lower = jnp.swapaxes(jnp.cumsum(lb_soft, axis=1) - lb_soft[:, :1], 0, 1).reshape(depth, 2, 1, 256)
    hg_w = (lower, jnp.tile(hgrn_norm_w, (1, 4)).reshape(depth, 1, 256))
    eye4 = jnp.eye(4, dtype=F32)
    s0t = jnp.einsum('blzhdv,hg->blzhvgd', f32(state_hgrn), eye4).reshape(nbs, depth, 2, 256, 256)
    mw = _mla_weights(mla_q_norm, mla_w_uq, mla_kv_norm, mla_w_ukv, mla_qk_norm_q, mla_qk_norm_k)
    kr_ctx = _krope128(cache_mla_krope)
    wout_b = w_out.astype(BF16)
    wr_h, wr_l = _split_bf16(jnp.pad(w_router, ((0, 0), (0, 0), (0, 128 - N_EXP))))
    h0_lat = f32(state_rglru)

    ones64 = jnp.asarray(_np_block_ones(256, 64), BF16)
    tri = jnp.asarray(np.triu(np.ones((256, 256), np.float32)), BF16)
    segt = jnp.asarray((np.arange(SEG)[:, None] < TK * np.arange(128)[None, :]).astype(np.float32), BF16)
    cs_s, sn_s = _rope_tables(t_s, 64)
    cs_p, sn_p = jnp.ones((t_p, 128), F32), jnp.zeros((t_p, 128), F32)
    zero_ctx = jnp.zeros((nbp, 256, 128), F32)
    zero_h0 = jnp.zeros((nbp, 2, 256), F32)

    ckvs, krs, lru_states, hgrn_states = [], [], [], []
    src = (x_prompt.reshape(n_p, D), x_sample.reshape(nbs * t_s, D))
    for l in range(depth):
        res = _in_call(src, l, mod, norm1, w_in_b)
        x = res[0] if len(src) == 2 else src[0]
        pa, pb, pc, pd = res[-4:]

        oa_c, lru_fin = _lru_call(pa, t_p, nbp, 0, l, *lru_w, zero_h0)
        oa_l, _ = _lru_call(pa, t_s, nbs, 1, l, *lru_w, h0_lat)
        ob_c = _pool_call(pb, t_p, nbp, 0, l, pw_bd, pscale)
        ob_l = _pool_call(pb, t_s, nbs, 1, l, pw_bd, pscale)
        oc_c, st_p = _hgrn_call(pc, t_p, nbp, 0, l, *hg_w, None, ones64)
        (oc_l,) = _hgrn_call(pc, t_s, nbs, 1, l, *hg_w, s0t, ones64)
        od_c, ckvn = _mla_call(pd, t_p, nbp, 0, l, 0, False, zero_ctx, zero_ctx, cs_p, sn_p, mw)
        od_l, _ = _mla_call(pd, t_s, nbs, 1, l, 256, True, f32(cache_mla_ckv), kr_ctx, cs_s, sn_s, mw)

        x1, h2e, aff = _out_call((oa_c, oa_l, ob_c, ob_l, oc_c, oc_l, od_c, od_l), x, l, mod,
                                 norm2, wout_b, wr_h, wr_l)

        slot_p, cum_p = _sel_call(aff, 1, 0, nbp, 2 * t_p // N_EXP, tri, segt)
        slot_s, cum_s = _sel_call(aff, nbs, 1, 1, 2 * t_s // N_EXP, tri, segt)
        slot = jnp.concatenate([slot_p, slot_s], axis=0)
        cum = jnp.concatenate([cum_p, cum_s], axis=0)
        gath = _gather_call(cum, slot, h2e)
        y = _ffn_call(gath, l, w_exp_gate, w_exp_up, w_exp_down)
        src = _scatter_call(cum, slot, y, x1, l, mod, split=(l == depth - 1))
        src = tuple(src) if l == depth - 1 else (src,)

        ckvs.append(ckvn.reshape(nbp, t_p, 128))
        krs.append(pd[:n_p, 384:416].reshape(nbp, t_p, 32))
        lru_states.append(lru_fin)
        hgrn_states.append(jnp.swapaxes(st_p[..., :64].reshape(nbp, 2, 4, 64, 64), -1, -2))

    y_c, y_l = src
    return (y_c.reshape(nbp, t_p, D), y_l.reshape(nbs, t_s, D),
            jnp.stack(ckvs, axis=1), jnp.stack(krs, axis=1),
            jnp.stack(lru_states, axis=1), jnp.stack(hgrn_states, axis=1))
```
